```python
import math
import jax, jax.numpy as jnp
from jax import lax
import numpy as np

D_MODEL = 2048
BATCH = 1
SEQ = 8192
DEPTH = 4

N_A_LAYERS = DEPTH // 2
N_B_LAYERS = DEPTH - N_A_LAYERS

NSA_HEADS = 16
NSA_KV_HEADS = 4
NSA_GROUP = NSA_HEADS // NSA_KV_HEADS
NSA_HEAD_DIM = D_MODEL // NSA_HEADS
CMP_BLOCK = 32
CMP_STRIDE = 16
CMP_HIDDEN = 4 * NSA_HEAD_DIM
SEL_BLOCK = 64
SEL_TOPK = 16
WINDOW = 512
FORCE_SCORE = 1.0e4
NSA_IN_COLS = NSA_HEADS * NSA_HEAD_DIM + 6 * NSA_KV_HEADS * NSA_HEAD_DIM + 3 * NSA_HEADS

DIFF_HEADS = 16
DIFF_KV_HEADS = 4
DIFF_GROUP = DIFF_HEADS // DIFF_KV_HEADS
DIFF_HEAD_DIM = D_MODEL // (2 * DIFF_HEADS)
DIFF_KV_COLS = DIFF_KV_HEADS * 2 * DIFF_HEAD_DIM * 2

D_FF = 4 * D_MODEL

ROPE_THETA = 500000.0
ROPE_FRACTION = 4
Q_BLOCK = 128
NORM_EPS = 1e-6

kernel_name = "yoco_nsa_diffattn_hybrid"


def rms_norm(x, g):
    xf = x.astype(jnp.float32)
    y = xf * lax.rsqrt(jnp.mean(xf * xf, axis=-1, keepdims=True) + NORM_EPS)
    return (y * g.astype(jnp.float32)).astype(x.dtype)


def rope_tables(seq, head_dim):
    rot = head_dim // ROPE_FRACTION
    inv = 1.0 / (ROPE_THETA ** (jnp.arange(0, rot, 2, dtype=jnp.float32) / rot))
    ang = jnp.arange(seq, dtype=jnp.float32)[:, None] * inv[None, :]
    return jnp.cos(ang), jnp.sin(ang)


def apply_partial_rope(x, cos, sin):
    half = cos.shape[-1]
    x1, x2, xp = x[..., :half], x[..., half:2 * half], x[..., 2 * half:]
    c, s = cos.astype(x.dtype), sin.astype(x.dtype)
    return jnp.concatenate([x1 * c - x2 * s, x1 * s + x2 * c, xp], axis=-1)


def masked_softmax(s, valid):
    s = jnp.where(valid, s.astype(jnp.float32), -jnp.inf)
    m = jnp.max(s, axis=-1, keepdims=True)
    m = jnp.where(jnp.isfinite(m), m, 0.0)
    e = jnp.exp(s - m)
    den = jnp.sum(e, axis=-1, keepdims=True)
    return e / jnp.maximum(den, 1e-30)


def sq_relu_mlp(h, w_up, w_down):
    return jnp.square(jax.nn.relu(h @ w_up)) @ w_down


def nsa_mixer(h, w_in, cmp_pos, cmp_w1, cmp_w2, w_out, cos, sin):
    B, S, _ = h.shape
    H, Hk, G, dk = NSA_HEADS, NSA_KV_HEADS, NSA_GROUP, NSA_HEAD_DIM
    kvw = Hk * dk
    scale = dk ** -0.5
    proj = h @ w_in
    q = proj[..., :H * dk].reshape(B, S, Hk, G, dk).transpose(0, 2, 3, 1, 4)
    parts = [proj[..., H * dk + i * kvw:H * dk + (i + 1) * kvw]
             .reshape(B, S, Hk, dk).transpose(0, 2, 1, 3) for i in range(6)]
    k_c, v_c, k_s, v_s, k_w, v_w = parts
    gates = jax.nn.sigmoid(proj[..., H * dk + 6 * kvw:].astype(jnp.float32))
    gates = gates.reshape(B, S, Hk, G, 3).transpose(0, 2, 3, 1, 4)

    n_cmp = (S - CMP_BLOCK) // CMP_STRIDE + 1
    idx = jnp.arange(n_cmp)[:, None] * CMP_STRIDE + jnp.arange(CMP_BLOCK)[None, :]

    def compress(t, j):
        blk = (t[:, :, idx] + cmp_pos[j]).reshape(B, Hk, n_cmp, CMP_BLOCK * dk)
        return jax.nn.gelu(blk @ cmp_w1[j]) @ cmp_w2[j]

    kc, vc = compress(k_c, 0), compress(v_c, 1)
    cmp_end = idx[:, -1]

    n_sel = S // SEL_BLOCK
    topk = min(SEL_TOPK, n_sel)
    cs = jnp.arange(n_cmp) * CMP_STRIDE
    ss = jnp.arange(n_sel) * SEL_BLOCK
    overlap = ((cs[:, None] < ss[None, :] + SEL_BLOCK) &
               (cs[:, None] + CMP_BLOCK > ss[None, :])).astype(jnp.float32)

    q_rot = apply_partial_rope(q, cos, sin)
    ks_blocks = apply_partial_rope(k_s, cos, sin).reshape(B, Hk, n_sel, SEL_BLOCK, dk)
    vs_blocks = v_s.reshape(B, Hk, n_sel, SEL_BLOCK, dk)
    pad = ((0, 0), (0, 0), (WINDOW, 0), (0, 0))
    kw_pad = jnp.pad(apply_partial_rope(k_w, cos, sin), pad)
    vw_pad = jnp.pad(v_w, pad)
    gather = jax.vmap(jax.vmap(lambda blocks, ix: blocks[ix]))
    sel_ids = jnp.arange(n_sel)

    def block_fn(qb):
        start = qb * Q_BLOCK
        t = start + jnp.arange(Q_BLOCK)
        qr = lax.dynamic_slice_in_dim(q, start, Q_BLOCK, axis=3)
        qo = lax.dynamic_slice_in_dim(q_rot, start, Q_BLOCK, axis=3)
        g = lax.dynamic_slice_in_dim(gates, start, Q_BLOCK, axis=3)

        s_c = jnp.einsum('bhgqd,bhnd->bhgqn', qr, kc) * scale
        p_c = masked_softmax(s_c, cmp_end[None, :] <= t[:, None])
        o_c = jnp.einsum('bhgqn,bhnd->bhgqd', p_c.astype(vc.dtype), vc)

        imp = jnp.einsum('bhgqn,ns->bhqs', p_c, overlap)
        cur = t // SEL_BLOCK
        forced = ((sel_ids[None, :] == 0) | (sel_ids[None, :] == cur[:, None]) |
                  (sel_ids[None, :] == cur[:, None] - 1))
        imp = jnp.where(forced, FORCE_SCORE, imp)
        imp = jnp.where(ss[None, :] <= t[:, None], imp, -1.0)
        _, sel_idx = lax.top_k(imp, topk)
        k_g = gather(ks_blocks, sel_idx).reshape(B, Hk, Q_BLOCK, topk * SEL_BLOCK, dk)
        v_g = gather(vs_blocks, sel_idx).reshape(B, Hk, Q_BLOCK, topk * SEL_BLOCK, dk)
        key_pos = sel_idx[..., None] * SEL_BLOCK + jnp.arange(SEL_BLOCK)
        valid_s = (key_pos <= t[:, None, None]).reshape(B, Hk, 1, Q_BLOCK, topk * SEL_BLOCK)
        s_s = jnp.einsum('bhgqd,bhqkd->bhgqk', qo, k_g) * scale
        p_s = masked_softmax(s_s, valid_s)
        o_s = jnp.einsum('bhgqk,bhqkd->bhgqd', p_s.astype(v_g.dtype), v_g)

        kw = lax.dynamic_slice_in_dim(kw_pad, start, Q_BLOCK + WINDOW, axis=2)
        vw = lax.dynamic_slice_in_dim(vw_pad, start, Q_BLOCK + WINDOW, axis=2)
        kpos = start - WINDOW + jnp.arange(Q_BLOCK + WINDOW)
        valid_w = ((kpos[None, :] <= t[:, None]) & (kpos[None, :] > t[:, None] - WINDOW) &
                   (kpos[None, :] >= 0))
        s_w = jnp.einsum('bhgqd,bhkd->bhgqk', qo, kw) * scale
        p_w = masked_softmax(s_w, valid_w)
        o_w = jnp.einsum('bhgqk,bhkd->bhgqd', p_w.astype(vw.dtype), vw)

        gg = g.astype(o_c.dtype)
        return gg[..., 0:1] * o_c + gg[..., 1:2] * o_s + gg[..., 2:3] * o_w

    o = lax.map(block_fn, jnp.arange(S // Q_BLOCK))
    o = o.transpose(1, 0, 4, 2, 3, 5).reshape(B, S, H * dk)
    return o @ w_out


def shared_kv(x, g, w_kv, cos, sin):
    B, S, _ = x.shape
    Hk, d = DIFF_KV_HEADS, DIFF_HEAD_DIM
    kv = rms_norm(x, g) @ w_kv
    kcols = Hk * 2 * d
    k = kv[..., :kcols].reshape(B, S, Hk, 2, d).transpose(0, 2, 3, 1, 4)
    k = apply_partial_rope(k, cos, sin)
    v = kv[..., kcols:].reshape(B, S, Hk, 2 * d).transpose(0, 2, 1, 3)
    return k, v


def diff_mixer(h, w_q, lam_vecs, subln_g, w_out, k_sh, v_sh, cos, sin, lambda_init):
    B, S, _ = h.shape
    H, Hk, G, d = DIFF_HEADS, DIFF_KV_HEADS, DIFF_GROUP, DIFF_HEAD_DIM
    scale = d ** -0.5
    q = (h @ w_q).reshape(B, S, Hk, G, 2, d).transpose(0, 2, 3, 4, 1, 5)
    q = apply_partial_rope(q, cos, sin)
    lv = lam_vecs.astype(jnp.float32)
    lam = jnp.exp(jnp.sum(lv[0] * lv[1])) - jnp.exp(jnp.sum(lv[2] * lv[3])) + lambda_init
    kpos = jnp.arange(S)

    def block_fn(qb):
        start = qb * Q_BLOCK
        t = start + jnp.arange(Q_BLOCK)
        qblk = lax.dynamic_slice_in_dim(q, start, Q_BLOCK, axis=4)
        s = jnp.einsum('bhgcqd,bhckd->bhgcqk', qblk, k_sh) * scale
        p = masked_softmax(s, kpos[None, :] <= t[:, None])
        a = p[:, :, :, 0] - lam * p[:, :, :, 1]
        return jnp.einsum('bhgqk,bhkd->bhgqd', a.astype(v_sh.dtype), v_sh)

    o = lax.map(block_fn, jnp.arange(S // Q_BLOCK))
    o = o.transpose(1, 0, 4, 2, 3, 5).reshape(B, S, H, 2 * d)
    o = rms_norm(o, subln_g) * (1.0 - lambda_init)
    return o.reshape(B, S, H * 2 * d) @ w_out


def setup_inputs(seed: int = 0) -> dict:
    key = jax.random.key(seed)
    ks = jax.random.split(key, 20)
    f32 = jnp.float32
    nrm = lambda k, shape, s: jax.random.normal(k, shape, f32) * s
    dk, d = NSA_HEAD_DIM, DIFF_HEAD_DIM
    return {
        "x": nrm(ks[0], (BATCH, SEQ, D_MODEL), 1.0),
        "attn_norm_g": 1.0 + nrm(ks[1], (DEPTH, D_MODEL), 0.02),
        "mlp_norm_g": 1.0 + nrm(ks[2], (DEPTH, D_MODEL), 0.02),
        "final_norm_g": 1.0 + nrm(ks[3], (D_MODEL,), 0.02),
        "nsa_w_in": nrm(ks[4], (N_A_LAYERS, D_MODEL, NSA_IN_COLS), D_MODEL ** -0.5),
        "nsa_cmp_pos": nrm(ks[5], (N_A_LAYERS, 2, CMP_BLOCK, dk), 0.1),
        "nsa_cmp_w1": nrm(ks[6], (N_A_LAYERS, 2, CMP_BLOCK * dk, CMP_HIDDEN), (CMP_BLOCK * dk) ** -0.5),
        "nsa_cmp_w2": nrm(ks[7], (N_A_LAYERS, 2, CMP_HIDDEN, dk), CMP_HIDDEN ** -0.5),
        "nsa_w_out": nrm(ks[8], (N_A_LAYERS, D_MODEL, D_MODEL), D_MODEL ** -0.5),
        "kv_norm_g": 1.0 + nrm(ks[9], (D_MODEL,), 0.02),
        "kv_w_shared": nrm(ks[10], (D_MODEL, DIFF_KV_COLS), D_MODEL ** -0.5),
        "diff_w_q": nrm(ks[11], (N_B_LAYERS, D_MODEL, 2 * DIFF_HEADS * d), D_MODEL ** -0.5),
        "diff_lambda": nrm(ks[12], (N_B_LAYERS, 4, d), 0.1),
        "diff_subln_g": 1.0 + nrm(ks[13], (N_B_LAYERS, 2 * d), 0.02),
        "diff_w_out": nrm(ks[14], (N_B_LAYERS, DIFF_HEADS * 2 * d, D_MODEL), D_MODEL ** -0.5),
        "mlp_w_up": nrm(ks[15], (DEPTH, D_MODEL, D_FF), D_MODEL ** -0.5),
        "mlp_w_down": nrm(ks[16], (DEPTH, D_FF, D_MODEL), D_FF ** -0.5),
    }


def reference(x, attn_norm_g, mlp_norm_g, final_norm_g, nsa_w_in, nsa_cmp_pos, nsa_cmp_w1,
              nsa_cmp_w2, nsa_w_out, kv_norm_g, kv_w_shared, diff_w_q, diff_lambda,
              diff_subln_g, diff_w_out, mlp_w_up, mlp_w_down):
    S = x.shape[1]
    cos_a, sin_a = rope_tables(S, NSA_HEAD_DIM)
    cos_b, sin_b = rope_tables(S, DIFF_HEAD_DIM)
    k_sh, v_sh = None, None
    for layer in range(DEPTH):
        h = rms_norm(x, attn_norm_g[layer])
        if layer < N_A_LAYERS:
            x = x + nsa_mixer(h, nsa_w_in[layer], nsa_cmp_pos[layer], nsa_cmp_w1[layer],
                              nsa_cmp_w2[layer], nsa_w_out[layer], cos_a, sin_a)
        else:
            j = layer - N_A_LAYERS
            if j == 0:
                k_sh, v_sh = shared_kv(x, kv_norm_g, kv_w_shared, cos_b, sin_b)
            lambda_init = 0.8 - 0.6 * math.exp(-0.3 * layer)
            x = x + diff_mixer(h, diff_w_q[j], diff_lambda[j], diff_subln_g[j], diff_w_out[j],
                               k_sh, v_sh, cos_b, sin_b, lambda_init)
        x = x + sq_relu_mlp(rms_norm(x, mlp_norm_g[layer]), mlp_w_up[layer], mlp_w_down[layer])
    return rms_norm(x, final_norm_g)
```

```python
import functools
import math

import jax
import jax.numpy as jnp
from jax import lax
from jax.experimental import pallas as pl
from jax.experimental.pallas import tpu as pltpu

F32 = jnp.float32
BF16 = jnp.bfloat16

D_MODEL = 2048
DEPTH = 4
N_A_LAYERS = DEPTH // 2

NSA_HEADS = 16
NSA_KV_HEADS = 4
NSA_GROUP = NSA_HEADS // NSA_KV_HEADS
NSA_HEAD_DIM = D_MODEL // NSA_HEADS
CMP_BLOCK = 32
CMP_STRIDE = 16
CMP_HIDDEN = 4 * NSA_HEAD_DIM
SEL_BLOCK = 64
SEL_SHIFT = SEL_BLOCK.bit_length() - 1
SEL_TOPK = 16
WINDOW = 512
FORCE_SCORE = 1.0e4

DIFF_HEADS = 16
DIFF_KV_HEADS = 4
DIFF_GROUP = DIFF_HEADS // DIFF_KV_HEADS
DIFF_HEAD_DIM = D_MODEL // (2 * DIFF_HEADS)

D_FF = 4 * D_MODEL
ROPE_THETA = 500000.0
ROPE_FRACTION = 4
NORM_EPS = 1e-6

LANES = 128
VMEM_LIMIT = 48 * 1024 * 1024
NEG_BIG = -1.0e30

NSA_TQ = 128
NSA_SEL_CHUNK = 512
DIFF_TQ = 256
ROW_TILE = 512
COL_TILE = 512
FF_TILE = 512


def _params(*sem):
    return pltpu.CompilerParams(dimension_semantics=sem, vmem_limit_bytes=VMEM_LIMIT)


def _dot(a, b):
    return jnp.dot(a, b, preferred_element_type=F32)


def _dot_nt(a, b):
    return lax.dot_general(a, b, (((1,), (1,)), ((), ())), preferred_element_type=F32)


def _rms(x, g):
    ms = jnp.mean(x * x, axis=-1, keepdims=True)
    return x * lax.rsqrt(ms + NORM_EPS) * g


def _norm_proj_kernel(x_ref, g_ref, w_ref, o_ref, h_scr, *, act):
    @pl.when(pl.program_id(1) == 0)
    def _():
        h_scr[...] = _rms(x_ref[...], g_ref[...]).astype(BF16)

    acc = _dot(h_scr[...], w_ref[...])
    if act == "sigmoid":
        acc = jax.nn.sigmoid(acc)
    o_ref[...] = acc.astype(o_ref.dtype)


def _norm_proj(x, g, w, *, act=None, out_dtype=BF16):
    s, d = x.shape
    n = w.shape[1]
    tm, tn = min(ROW_TILE, s), min(COL_TILE, n)
    return pl.pallas_call(
        functools.partial(_norm_proj_kernel, act=act),
        name="norm_proj",
        grid=(s // tm, n // tn),
        in_specs=[pl.BlockSpec((tm, d), lambda i, j: (i, 0)),
                  pl.BlockSpec((1, d), lambda i, j: (0, 0)),
                  pl.BlockSpec((d, tn), lambda i, j: (0, j))],
        out_specs=pl.BlockSpec((tm, tn), lambda i, j: (i, j)),
        out_shape=jax.ShapeDtypeStruct((s, n), out_dtype),
        scratch_shapes=[pltpu.VMEM((tm, d), BF16)],
        compiler_params=_params("parallel", "arbitrary"),
    )(x, g.reshape(1, d), w)


def _norm_proj_rope_kernel(x_ref, g_ref, w_ref, c_ref, sa_ref, sb_ref, *rest, shift, want_raw):
    if want_raw:
        raw_ref, rot_ref, h_scr = rest
    else:
        rot_ref, h_scr = rest

    @pl.when(pl.program_id(1) == 0)
    def _():
        h_scr[...] = _rms(x_ref[...], g_ref[...]).astype(BF16)

    acc = _dot(h_scr[...], w_ref[...])
    if want_raw:
        raw_ref[...] = acc.astype(BF16)
    c, sa, sb = c_ref[...], sa_ref[...], sb_ref[...]
    for blk in range(acc.shape[1] // LANES):
        a = acc[:, blk * LANES:(blk + 1) * LANES]
        r = a * c + pltpu.roll(a, LANES - shift, 1) * sa + pltpu.roll(a, shift, 1) * sb
        rot_ref[:, blk * LANES:(blk + 1) * LANES] = r.astype(BF16)


def _norm_proj_rope(x, g, w, tables, *, shift, want_raw):
    s, d = x.shape
    n = w.shape[1]
    tm, tn = min(ROW_TILE, s), min(COL_TILE, n)
    tab_spec = pl.BlockSpec((tm, LANES), lambda i, j: (i, 0))
    out_spec = pl.BlockSpec((tm, tn), lambda i, j: (i, j))
    out_sd = jax.ShapeDtypeStruct((s, n), BF16)
    return pl.pallas_call(
        functools.partial(_norm_proj_rope_kernel, shift=shift, want_raw=want_raw),
        name="norm_proj_rope",
        grid=(s // tm, n // tn),
        in_specs=[pl.BlockSpec((tm, d), lambda i, j: (i, 0)),
                  pl.BlockSpec((1, d), lambda i, j: (0, 0)),
                  pl.BlockSpec((d, tn), lambda i, j: (0, j)),
                  tab_spec, tab_spec, tab_spec],
        out_specs=[out_spec, out_spec] if want_raw else out_spec,
        out_shape=[out_sd, out_sd] if want_raw else out_sd,
        scratch_shapes=[pltpu.VMEM((tm, d), BF16)],
        compiler_params=_params("parallel", "arbitrary"),
    )(x, g.reshape(1, d), w, *tables)


def _matmul_res_kernel(a_ref, w_ref, r_ref, o_ref):
    o_ref[...] = r_ref[...] + _dot(a_ref[...], w_ref[...])


def _matmul_res(a, w, res):
    s, k = a.shape
    n = w.shape[1]
    tm, tn = min(ROW_TILE, s), min(COL_TILE, n)
    return pl.pallas_call(
        _matmul_res_kernel,
        name="matmul_res",
        grid=(s // tm, n // tn),
        in_specs=[pl.BlockSpec((tm, k), lambda i, j: (i, 0)),
                  pl.BlockSpec((k, tn), lambda i, j: (0, j)),
                  pl.BlockSpec((tm, tn), lambda i, j: (i, j))],
        out_specs=pl.BlockSpec((tm, tn), lambda i, j: (i, j)),
        out_shape=jax.ShapeDtypeStruct((s, n), F32),
        compiler_params=_params("parallel", "arbitrary"),
    )(a, w, res)


def _mlp_kernel(x_ref, g_ref, wu_ref, wd_ref, fg_ref, o_ref, h_scr, acc_scr, *, final_norm):
    j = pl.program_id(1)

    @pl.when(j == 0)
    def _():
        h_scr[...] = _rms(x_ref[...], g_ref[...]).astype(BF16)
        acc_scr[...] = jnp.zeros_like(acc_scr)

    u = _dot(h_scr[...], wu_ref[...])
    u = jnp.square(jnp.maximum(u, 0.0)).astype(BF16)
    acc_scr[...] += _dot(u, wd_ref[...])

    @pl.when(j == pl.num_programs(1) - 1)
    def _():
        y = x_ref[...] + acc_scr[...]
        if final_norm:
            y = _rms(y, fg_ref[...])
        o_ref[...] = y


def _mlp(x, g, w_up, w_down, final_g, *, final_norm):
    s, d = x.shape
    ff = w_up.shape[1]
    tm, tf = min(ROW_TILE, s), min(FF_TILE, ff)
    return pl.pallas_call(
        functools.partial(_mlp_kernel, final_norm=final_norm),
        name="sq_relu_mlp",
        grid=(s // tm, ff // tf),
        in_specs=[pl.BlockSpec((tm, d), lambda i, j: (i, 0)),
                  pl.BlockSpec((1, d), lambda i, j: (0, 0)),
                  pl.BlockSpec((d, tf), lambda i, j: (0, j)),
                  pl.BlockSpec((tf, d), lambda i, j: (j, 0)),
                  pl.BlockSpec((1, d), lambda i, j: (0, 0))],
        out_specs=pl.BlockSpec((tm, d), lambda i, j: (i, 0)),
        out_shape=jax.ShapeDtypeStruct((s, d), F32),
        scratch_shapes=[pltpu.VMEM((tm, d), BF16), pltpu.VMEM((tm, d), F32)],
        compiler_params=_params("parallel", "arbitrary"),
    )(x, g.reshape(1, d), w_up, w_down, final_g.reshape(1, d))


def _compress_kernel(t_ref, pos_ref, w1_ref, w2_ref, o_ref):
    t = t_ref[0, 0].astype(F32)
    pos = pos_ref[0]
    half = t.shape[1]
    a = _dot((t + pos[0:1]).astype(BF16), w1_ref[0, :half, :])
    b = _dot((t + pos[1:2]).astype(BF16), w1_ref[0, half:, :])
    nc = t.shape[0]
    hid = jax.nn.gelu(a + pltpu.roll(b, nc - 1, 0))
    o_ref[0, 0] = _dot(hid.astype(BF16), w2_ref[0]).astype(BF16)


def _compress(t2, pos2, w1, w2):
    _, hk, nc, width = t2.shape
    dk = w2.shape[-1]
    return pl.pallas_call(
        _compress_kernel,
        name="nsa_compress",
        grid=(2, hk),
        in_specs=[pl.BlockSpec((1, 1, nc, width), lambda j, h: (j, h, 0, 0)),
                  pl.BlockSpec((1, 2, width), lambda j, h: (j, 0, 0)),
                  pl.BlockSpec((1, 2 * width, CMP_HIDDEN), lambda j, h: (j, 0, 0)),
                  pl.BlockSpec((1, CMP_HIDDEN, dk), lambda j, h: (j, 0, 0))],
        out_specs=pl.BlockSpec((1, 1, nc, dk), lambda j, h: (j, h, 0, 0)),
        out_shape=jax.ShapeDtypeStruct((2, hk, nc, dk), BF16),
        compiler_params=_params("parallel", "arbitrary"),
    )(t2, pos2, w1, w2)


def _nsa_attn_kernel(qr_ref, qo_ref, kc_ref, vc_ref, ks_ref, vs_ref, kw_ref, vw_ref, gt_ref,
                     o_ref, *, seq):
    g_, tq, dk = NSA_GROUP, NSA_TQ, NSA_HEAD_DIM
    rows = g_ * tq
    nc = seq // CMP_STRIDE
    nsel = seq // SEL_BLOCK
    ch = min(NSA_SEL_CHUNK, seq)
    wlen = WINDOW + tq
    scale = dk ** -0.5
    qb = pl.program_id(1)
    start = qb * tq

    def stack_heads(ref):
        x = ref[...]
        return jnp.concatenate([x[:, g * dk:(g + 1) * dk] for g in range(g_)], axis=0)

    qr = stack_heads(qr_ref)
    qo = stack_heads(qo_ref)
    t_col = start + lax.broadcasted_iota(jnp.int32, (tq, 1), 0)

    s_c = (_dot_nt(qr, kc_ref[0, 0]) * scale).reshape(g_, tq, nc)
    n_id = lax.broadcasted_iota(jnp.int32, (tq, nc), 1)
    valid_c = (n_id * CMP_STRIDE + (CMP_BLOCK - 1)) <= t_col
    s_c = jnp.where(valid_c[None], s_c, -jnp.inf)
    m_c = jnp.max(s_c, axis=-1, keepdims=True)
    m_c = jnp.where(m_c == -jnp.inf, 0.0, m_c)
    e_c = jnp.exp(s_c - m_c)
    p_c = e_c / jnp.maximum(jnp.sum(e_c, axis=-1, keepdims=True), 1e-30)
    o_c = _dot(p_c.reshape(rows, nc).astype(BF16), vc_ref[0, 0]).reshape(g_, tq, dk)

    p_sum = jnp.sum(p_c, axis=0)
    n_i = lax.broadcasted_iota(jnp.int32, (nc, nsel), 0) * CMP_STRIDE
    s_i = lax.broadcasted_iota(jnp.int32, (nc, nsel), 1) * SEL_BLOCK
    overlap = jnp.where((n_i < s_i + SEL_BLOCK) & (n_i + CMP_BLOCK > s_i), 1.0, 0.0).astype(BF16)
    p_hi = p_sum.astype(BF16)
    p_r1 = p_sum - p_hi.astype(F32)
    p_mid = p_r1.astype(BF16)
    p_lo = (p_r1 - p_mid.astype(F32)).astype(BF16)
    imp = _dot(p_hi, overlap) + _dot(p_mid, overlap) + _dot(p_lo, overlap)

    ids = lax.broadcasted_iota(jnp.int32, (tq, nsel), 1)
    cur = t_col >> SEL_SHIFT
    forced = (ids == 0) | (ids == cur) | (ids == cur - 1)
    imp = jnp.where(forced, FORCE_SCORE, imp)
    imp = jnp.where(ids * SEL_BLOCK <= t_col, imp, -1.0)

    work = imp.T
    ids_t = lax.broadcasted_iota(jnp.int32, (nsel, tq), 0)
    sel_t = jnp.zeros((nsel, tq), F32)
    for _ in range(min(SEL_TOPK, nsel)):
        mx = jnp.max(work, axis=0, keepdims=True)
        first = jnp.min(jnp.where(work == mx, ids_t, nsel), axis=0, keepdims=True)
        pick = ids_t == first
        sel_t = jnp.where(pick, 1.0, sel_t)
        work = jnp.where(pick, -3.0e38, work)
    sel = sel_t.T.astype(BF16)

    blocks_per_chunk = ch // SEL_BLOCK
    b_i = lax.broadcasted_iota(jnp.int32, (nsel, ch), 0)
    k_blk = lax.broadcasted_iota(jnp.int32, (nsel, ch), 1) >> SEL_SHIFT
    k_off = lax.broadcasted_iota(jnp.int32, (tq, ch), 1)

    def sel_step(c, carry):
        m, l, acc = carry
        off = pl.multiple_of(c * ch, ch)
        k = ks_ref[pl.ds(off, ch), :]
        v = vs_ref[pl.ds(off, ch), :]
        s = (_dot_nt(qo, k) * scale).reshape(g_, tq, ch)
        expand = jnp.where(b_i == k_blk + c * blocks_per_chunk, 1.0, 0.0).astype(BF16)
        chosen = _dot(sel, expand)
        valid = (chosen > 0.5) & (k_off + off <= t_col)
        s = jnp.where(valid[None], s, NEG_BIG)
        m_new = jnp.maximum(m, jnp.max(s, axis=-1, keepdims=True))
        alpha = jnp.exp(m - m_new)
        p = jnp.exp(s - m_new)
        l = alpha * l + jnp.sum(p, axis=-1, keepdims=True)
        pv = _dot(p.reshape(rows, ch).astype(BF16), v).reshape(g_, tq, dk)
        return m_new, l, alpha * acc + pv

    n_chunks = (start + tq + ch - 1) // ch
    init = (jnp.full((g_, tq, 1), NEG_BIG, F32), jnp.zeros((g_, tq, 1), F32),
            jnp.zeros((g_, tq, dk), F32))
    m_s, l_s, acc_s = lax.fori_loop(0, n_chunks, sel_step, init)
    o_s = jnp.where(m_s > 0.5 * NEG_BIG, acc_s / jnp.maximum(l_s, 1e-30), 0.0)

    w0 = pl.multiple_of(jnp.maximum(start - WINDOW, 0), tq)
    s_w = (_dot_nt(qo, kw_ref[pl.ds(w0, wlen), :]) * scale).reshape(g_, tq, wlen)
    kpos = w0 + lax.broadcasted_iota(jnp.int32, (tq, wlen), 1)
    valid_w = (kpos <= t_col) & (kpos > t_col - WINDOW)
    s_w = jnp.where(valid_w[None], s_w, -jnp.inf)
    m_w = jnp.max(s_w, axis=-1, keepdims=True)
    m_w = jnp.where(m_w == -jnp.inf, 0.0, m_w)
    e_w = jnp.exp(s_w - m_w)
    p_w = e_w / jnp.maximum(jnp.sum(e_w, axis=-1, keepdims=True), 1e-30)
    o_w = _dot(p_w.reshape(rows, wlen).astype(BF16), vw_ref[pl.ds(w0, wlen), :]).reshape(g_, tq, dk)

    gt = gt_ref[0]
    outs = [gt[:, 3 * g:3 * g + 1] * o_c[g] + gt[:, 3 * g + 1:3 * g + 2] * o_s[g]
            + gt[:, 3 * g + 2:3 * g + 3] * o_w[g] for g in range(g_)]
    o_ref[...] = jnp.concatenate(outs, axis=1).astype(BF16)


def _nsa_attention(q_raw_rot, kv_rot_src, plain, cmp_kv, gates):
    raw, rot = q_raw_rot
    s = raw.shape[0]
    hk, g_, dk, tq = NSA_KV_HEADS, NSA_GROUP, NSA_HEAD_DIM, NSA_TQ
    nc = s // CMP_STRIDE
    qspec = pl.BlockSpec((tq, g_ * dk), lambda h, i: (i, h))
    kv_col = lambda base: pl.BlockSpec((s, dk), lambda h, i: (0, base + h))
    return pl.pallas_call(
        functools.partial(_nsa_attn_kernel, seq=s),
        name="nsa_attention",
        grid=(hk, s // tq),
        in_specs=[qspec, qspec,
                  pl.BlockSpec((1, 1, nc, dk), lambda h, i: (0, h, 0, 0)),
                  pl.BlockSpec((1, 1, nc, dk), lambda h, i: (1, h, 0, 0)),
                  kv_col(NSA_HEADS), kv_col(2 * hk),
                  kv_col(NSA_HEADS + hk), kv_col(3 * hk),
                  pl.BlockSpec((1, tq, 3 * g_), lambda h, i: (h, i, 0))],
        out_specs=pl.BlockSpec((tq, g_ * dk), lambda h, i: (i, h)),
        out_shape=jax.ShapeDtypeStruct((s, NSA_HEADS * dk), BF16),
        compiler_params=_params("parallel", "arbitrary"),
    )(raw, rot, cmp_kv, cmp_kv, kv_rot_src, plain, kv_rot_src, plain, gates)


def _diff_attn_kernel(q_ref, k_ref, v_ref, lam_ref, sg_ref, o_ref, *, lambda_init):
    g_, tq, d = DIFF_GROUP, DIFF_TQ, DIFF_HEAD_DIM
    scale = d ** -0.5
    qb = pl.program_id(1)

    lv = lam_ref[...]
    lam = (jnp.exp(jnp.sum(lv[0:1] * lv[1:2], axis=-1, keepdims=True))
           - jnp.exp(jnp.sum(lv[2:3] * lv[3:4], axis=-1, keepdims=True)) + lambda_init)
    first_half = lax.broadcasted_iota(jnp.int32, (1, 2 * d), 1) < d
    row2 = lax.broadcasted_iota(jnp.int32, (2 * tq, tq), 0)
    causal2 = (lax.broadcasted_iota(jnp.int32, (2 * tq, tq), 1)
               <= jnp.where(row2 >= tq, row2 - tq, row2))

    for g in range(g_):
        q = q_ref[:, g * 2 * d:(g + 1) * 2 * d]
        zero = jnp.zeros_like(q)
        qq = jnp.concatenate([jnp.where(first_half, q, zero), jnp.where(first_half, zero, q)], axis=0)

        def step(c, carry, diagonal):
            m, l, acc = carry
            off = pl.multiple_of(c * tq, tq)
            s = _dot_nt(qq, k_ref[pl.ds(off, tq), :]) * scale
            if diagonal:
                s = jnp.where(causal2, s, NEG_BIG)
            m_new = jnp.maximum(m, jnp.max(s, axis=-1, keepdims=True))
            alpha = jnp.exp(m - m_new)
            p = jnp.exp(s - m_new)
            l = alpha * l + jnp.sum(p, axis=-1, keepdims=True)
            return m_new, l, alpha * acc + _dot(p.astype(BF16), v_ref[pl.ds(off, tq), :])

        init = (jnp.full((2 * tq, 1), NEG_BIG, F32), jnp.zeros((2 * tq, 1), F32),
                jnp.zeros((2 * tq, 2 * d), F32))
        carry = lax.fori_loop(0, qb, functools.partial(step, diagonal=False), init)
        _, l, acc = step(qb, carry, True)
        o = acc / jnp.maximum(l, 1e-30)
        a = o[:tq] - lam * o[tq:]
        a = _rms(a, sg_ref[...]) * (1.0 - lambda_init)
        o_ref[:, g * 2 * d:(g + 1) * 2 * d] = a.astype(BF16)


def _diff_attention(q_rot, k_rot, v, lam_vecs, subln_g, lambda_init):
    s = q_rot.shape[0]
    hk, g_, d, tq = DIFF_KV_HEADS, DIFF_GROUP, DIFF_HEAD_DIM, DIFF_TQ
    return pl.pallas_call(
        functools.partial(_diff_attn_kernel, lambda_init=lambda_init),
        name="diff_attention",
        grid=(hk, s // tq),
        in_specs=[pl.BlockSpec((tq, g_ * 2 * d), lambda h, i: (i, h)),
                  pl.BlockSpec((s, 2 * d), lambda h, i: (0, h)),
                  pl.BlockSpec((s, 2 * d), lambda h, i: (0, h)),
                  pl.BlockSpec((4, d), lambda h, i: (0, 0)),
                  pl.BlockSpec((1, 2 * d), lambda h, i: (0, 0))],
        out_specs=pl.BlockSpec((tq, g_ * 2 * d), lambda h, i: (i, h)),
        out_shape=jax.ShapeDtypeStruct((s, DIFF_HEADS * 2 * d), BF16),
        compiler_params=_params("parallel", "arbitrary"),
    )(q_rot, k_rot, v, lam_vecs, subln_g.reshape(1, 2 * d))


def _rope_tables(seq, head_dim, reps):
    rot = head_dim // ROPE_FRACTION
    half = rot // 2
    inv = 1.0 / (ROPE_THETA ** (jnp.arange(0, rot, 2, dtype=F32) / rot))
    ang = jnp.arange(seq, dtype=F32)[:, None] * inv[None, :]
    cos, sin = jnp.cos(ang), jnp.sin(ang)
    rest = head_dim - rot
    zeros_h = jnp.zeros((seq, half), F32)
    c = jnp.concatenate([cos, cos, jnp.ones((seq, rest), F32)], axis=-1)
    sa = jnp.concatenate([-sin, zeros_h, jnp.zeros((seq, rest), F32)], axis=-1)
    sb = jnp.concatenate([zeros_h, sin, jnp.zeros((seq, rest), F32)], axis=-1)
    return tuple(jnp.tile(t, (1, reps)) for t in (c, sa, sb)), half


def _nsa_layer(x, norm_g, w_in, cmp_pos, cmp_w1, cmp_w2, w_out, tables, shift):
    s = x.shape[0]
    hk, g_, dk = NSA_KV_HEADS, NSA_GROUP, NSA_HEAD_DIM
    qw, kvw = NSA_HEADS * dk, hk * dk
    part = lambda i: w_in[:, qw + i * kvw:qw + (i + 1) * kvw]
    w_rope = jnp.concatenate([w_in[:, :qw], part(2), part(4)], axis=1).astype(BF16)
    w_plain = jnp.concatenate([part(0), part(1), part(3), part(5)], axis=1).astype(BF16)
    n_gate = 3 * NSA_HEADS
    w_gate = jnp.pad(w_in[:, qw + 6 * kvw:], ((0, 0), (0, LANES - n_gate))).astype(BF16)

    raw, rot = _norm_proj_rope(x, norm_g, w_rope, tables, shift=shift, want_raw=True)
    plain = _norm_proj(x, norm_g, w_plain)
    gates = _norm_proj(x, norm_g, w_gate, act="sigmoid", out_dtype=F32)
    gates = gates[:, :n_gate].reshape(s, hk, 3 * g_).transpose(1, 0, 2)

    nc = s // CMP_STRIDE
    t2 = plain[:, :2 * kvw].reshape(nc, CMP_STRIDE, 2, hk, dk).transpose(2, 3, 0, 1, 4)
    t2 = t2.reshape(2, hk, nc, CMP_STRIDE * dk)
    cmp_kv = _compress(t2, cmp_pos.reshape(2, 2, CMP_STRIDE * dk), cmp_w1.astype(BF16),
                       cmp_w2.astype(BF16))
    o = _nsa_attention((raw, rot), rot, plain, cmp_kv, gates)
    return _matmul_res(o, w_out.astype(BF16), x)


def kernel(x, attn_norm_g, mlp_norm_g, final_norm_g, nsa_w_in, nsa_cmp_pos, nsa_cmp_w1, nsa_cmp_w2, nsa_w_out, kv_norm_g, kv_w_shared, diff_w_q, diff_lambda, diff_subln_g, diff_w_out, mlp_w_up, mlp_w_down):
    b, s, d = x.shape
    tables_a, shift_a = _rope_tables(s, NSA_HEAD_DIM, 1)
    tables_b, shift_b = _rope_tables(s, DIFF_HEAD_DIM, 2)
    kcols = DIFF_KV_HEADS * 2 * DIFF_HEAD_DIM
    outs = []
    for bi in range(b):
        xs = x[bi]
        k_sh = v_sh = None
        for layer in range(DEPTH):
            if layer < N_A_LAYERS:
                xs = _nsa_layer(xs, attn_norm_g[layer], nsa_w_in[layer], nsa_cmp_pos[layer],
                                nsa_cmp_w1[layer], nsa_cmp_w2[layer], nsa_w_out[layer],
                                tables_a, shift_a)
            else:
                j = layer - N_A_LAYERS
                if j == 0:
                    k_sh = _norm_proj_rope(xs, kv_norm_g, kv_w_shared[:, :kcols].astype(BF16),
                                           tables_b, shift=shift_b, want_raw=False)
                    v_sh = _norm_proj(xs, kv_norm_g, kv_w_shared[:, kcols:].astype(BF16))
                lambda_init = 0.8 - 0.6 * math.exp(-0.3 * layer)
                q = _norm_proj_rope(xs, attn_norm_g[layer], diff_w_q[j].astype(BF16), tables_b,
                                    shift=shift_b, want_raw=False)
                o = _diff_attention(q, k_sh, v_sh, diff_lambda[j], diff_subln_g[j], lambda_init)
                xs = _matmul_res(o, diff_w_out[j].astype(BF16), xs)
            xs = _mlp(xs, mlp_norm_g[layer], mlp_w_up[layer].astype(BF16),
                      mlp_w_down[layer].astype(BF16), final_norm_g,
                      final_norm=(layer == DEPTH - 1))
        outs.append(xs)
    return jnp.stack(outs, axis=0)
```

```python
import functools
import math

import jax
import jax.numpy as jnp
from jax import lax
from jax.experimental import pallas as pl
from jax.experimental.pallas import tpu as pltpu

F32 = jnp.float32
BF16 = jnp.bfloat16

D_MODEL = 2048
DEPTH = 4
N_A_LAYERS = DEPTH // 2

NSA_HEADS = 16
NSA_KV_HEADS = 4
NSA_GROUP = NSA_HEADS // NSA_KV_HEADS
NSA_HEAD_DIM = D_MODEL // NSA_HEADS
CMP_BLOCK = 32
CMP_STRIDE = 16
CMP_HIDDEN = 4 * NSA_HEAD_DIM
SEL_BLOCK = 64
SEL_SHIFT = SEL_BLOCK.bit_length() - 1
SEL_TOPK = 16
WINDOW = 512
FORCE_SCORE = 1.0e4

DIFF_HEADS = 16
DIFF_KV_HEADS = 4
DIFF_GROUP = DIFF_HEADS // DIFF_KV_HEADS
DIFF_HEAD_DIM = D_MODEL // (2 * DIFF_HEADS)

D_FF = 4 * D_MODEL
ROPE_THETA = 500000.0
ROPE_FRACTION = 4
NORM_EPS = 1e-6

LANES = 128
VMEM_LIMIT = 48 * 1024 * 1024
NEG_BIG = -1.0e30

NSA_TQ = 128
NSA_SEL_CHUNK = 512
DIFF_TQ = 256
ROW_TILE = 512
COL_TILE = 512
FF_TILE = 512


def _params(*sem):
    return pltpu.CompilerParams(dimension_semantics=sem, vmem_limit_bytes=VMEM_LIMIT)


def _dot(a, b):
    return jnp.dot(a, b, preferred_element_type=F32)


def _dot_nt(a, b):
    return lax.dot_general(a, b, (((1,), (1,)), ((), ())), preferred_element_type=F32)


def _rms(x, g):
    ms = jnp.mean(x * x, axis=-1, keepdims=True)
    return x * lax.rsqrt(ms + NORM_EPS) * g


def _norm_proj_kernel(x_ref, g_ref, w_ref, o_ref, h_scr, *, act):
    @pl.when(pl.program_id(1) == 0)
    def _():
        h_scr[...] = _rms(x_ref[...], g_ref[...]).astype(BF16)

    acc = _dot(h_scr[...], w_ref[...])
    if act == "sigmoid":
        acc = jax.nn.sigmoid(acc)
    o_ref[...] = acc.astype(o_ref.dtype)


def _norm_proj(x, g, w, *, act=None, out_dtype=BF16):
    s, d = x.shape
    n = w.shape[1]
    tm, tn = min(ROW_TILE, s), min(COL_TILE, n)
    return pl.pallas_call(
        functools.partial(_norm_proj_kernel, act=act),
        name="norm_proj",
        grid=(s // tm, n // tn),
        in_specs=[pl.BlockSpec((tm, d), lambda i, j: (i, 0)),
                  pl.BlockSpec((1, d), lambda i, j: (0, 0)),
                  pl.BlockSpec((d, tn), lambda i, j: (0, j))],
        out_specs=pl.BlockSpec((tm, tn), lambda i, j: (i, j)),
        out_shape=jax.ShapeDtypeStruct((s, n), out_dtype),
        scratch_shapes=[pltpu.VMEM((tm, d), BF16)],
        compiler_params=_params("parallel", "arbitrary"),
    )(x, g.reshape(1, d), w)


def _norm_proj_rope_kernel(x_ref, g_ref, w_ref, c_ref, sa_ref, sb_ref, *rest, shift, want_raw):
    if want_raw:
        raw_ref, rot_ref, h_scr = rest
    else:
        rot_ref, h_scr = rest

    @pl.when(pl.program_id(1) == 0)
    def _():
        h_scr[...] = _rms(x_ref[...], g_ref[...]).astype(BF16)

    acc = _dot(h_scr[...], w_ref[...])
    if want_raw:
        raw_ref[...] = acc.astype(BF16)
    c, sa, sb = c_ref[...], sa_ref[...], sb_ref[...]
    for blk in range(acc.shape[1] // LANES):
        a = acc[:, blk * LANES:(blk + 1) * LANES]
        r = a * c + pltpu.roll(a, LANES - shift, 1) * sa + pltpu.roll(a, shift, 1) * sb
        rot_ref[:, blk * LANES:(blk + 1) * LANES] = r.astype(BF16)


def _norm_proj_rope(x, g, w, tables, *, shift, want_raw):
    s, d = x.shape
    n = w.shape[1]
    tm, tn = min(ROW_TILE, s), min(COL_TILE, n)
    tab_spec = pl.BlockSpec((tm, LANES), lambda i, j: (i, 0))
    out_spec = pl.BlockSpec((tm, tn), lambda i, j: (i, j))
    out_sd = jax.ShapeDtypeStruct((s, n), BF16)
    return pl.pallas_call(
        functools.partial(_norm_proj_rope_kernel, shift=shift, want_raw=want_raw),
        name="norm_proj_rope",
        grid=(s // tm, n // tn),
        in_specs=[pl.BlockSpec((tm, d), lambda i, j: (i, 0)),
                  pl.BlockSpec((1, d), lambda i, j: (0, 0)),
                  pl.BlockSpec((d, tn), lambda i, j: (0, j)),
                  tab_spec, tab_spec, tab_spec],
        out_specs=[out_spec, out_spec] if want_raw else out_spec,
        out_shape=[out_sd, out_sd] if want_raw else out_sd,
        scratch_shapes=[pltpu.VMEM((tm, d), BF16)],
        compiler_params=_params("parallel", "arbitrary"),
    )(x, g.reshape(1, d), w, *tables)


def _matmul_res_kernel(a_ref, w_ref, r_ref, o_ref):
    o_ref[...] = r_ref[...] + _dot(a_ref[...], w_ref[...])


def _matmul_res(a, w, res):
    s, k = a.shape
    n = w.shape[1]
    tm, tn = min(ROW_TILE, s), min(COL_TILE, n)
    return pl.pallas_call(
        _matmul_res_kernel,
        name="matmul_res",
        grid=(s // tm, n // tn),
        in_specs=[pl.BlockSpec((tm, k), lambda i, j: (i, 0)),
                  pl.BlockSpec((k, tn), lambda i, j: (0, j)),
                  pl.BlockSpec((tm, tn), lambda i, j: (i, j))],
        out_specs=pl.BlockSpec((tm, tn), lambda i, j: (i, j)),
        out_shape=jax.ShapeDtypeStruct((s, n), F32),
        compiler_params=_params("parallel", "arbitrary"),
    )(a, w, res)


def _mlp_kernel(x_ref, g_ref, wu_ref, wd_ref, fg_ref, o_ref, h_scr, acc_scr, *, final_norm):
    j = pl.program_id(1)

    @pl.when(j == 0)
    def _():
        h_scr[...] = _rms(x_ref[...], g_ref[...]).astype(BF16)
        acc_scr[...] = jnp.zeros_like(acc_scr)

    u = _dot(h_scr[...], wu_ref[...])
    u = jnp.square(jnp.maximum(u, 0.0)).astype(BF16)
    acc_scr[...] += _dot(u, wd_ref[...])

    @pl.when(j == pl.num_programs(1) - 1)
    def _():
        y = x_ref[...] + acc_scr[...]
        if final_norm:
            y = _rms(y, fg_ref[...])
        o_ref[...] = y


def _mlp(x, g, w_up, w_down, final_g, *, final_norm):
    s, d = x.shape
    ff = w_up.shape[1]
    tm, tf = min(ROW_TILE, s), min(FF_TILE, ff)
    return pl.pallas_call(
        functools.partial(_mlp_kernel, final_norm=final_norm),
        name="sq_relu_mlp",
        grid=(s // tm, ff // tf),
        in_specs=[pl.BlockSpec((tm, d), lambda i, j: (i, 0)),
                  pl.BlockSpec((1, d), lambda i, j: (0, 0)),
                  pl.BlockSpec((d, tf), lambda i, j: (0, j)),
                  pl.BlockSpec((tf, d), lambda i, j: (j, 0)),
                  pl.BlockSpec((1, d), lambda i, j: (0, 0))],
        out_specs=pl.BlockSpec((tm, d), lambda i, j: (i, 0)),
        out_shape=jax.ShapeDtypeStruct((s, d), F32),
        scratch_shapes=[pltpu.VMEM((tm, d), BF16), pltpu.VMEM((tm, d), F32)],
        compiler_params=_params("parallel", "arbitrary"),
    )(x, g.reshape(1, d), w_up, w_down, final_g.reshape(1, d))


def _compress_kernel(t_ref, pos_ref, w1_ref, w2_ref, o_ref):
    t = t_ref[0, 0].astype(F32)
    pos = pos_ref[0]
    half = t.shape[1]
    a = _dot((t + pos[0:1]).astype(BF16), w1_ref[0, :half, :])
    b = _dot((t + pos[1:2]).astype(BF16), w1_ref[0, half:, :])
    nc = t.shape[0]
    hid = jax.nn.gelu(a + pltpu.roll(b, nc - 1, 0))
    o_ref[0, 0] = _dot(hid.astype(BF16), w2_ref[0]).astype(BF16)


def _compress(t2, pos2, w1, w2):
    _, hk, nc, width = t2.shape
    dk = w2.shape[-1]
    return pl.pallas_call(
        _compress_kernel,
        name="nsa_compress",
        grid=(2, hk),
        in_specs=[pl.BlockSpec((1, 1, nc, width), lambda j, h: (j, h, 0, 0)),
                  pl.BlockSpec((1, 2, width), lambda j, h: (j, 0, 0)),
                  pl.BlockSpec((1, 2 * width, CMP_HIDDEN), lambda j, h: (j, 0, 0)),
                  pl.BlockSpec((1, CMP_HIDDEN, dk), lambda j, h: (j, 0, 0))],
        out_specs=pl.BlockSpec((1, 1, nc, dk), lambda j, h: (j, h, 0, 0)),
        out_shape=jax.ShapeDtypeStruct((2, hk, nc, dk), BF16),
        compiler_params=_params("parallel", "arbitrary"),
    )(t2, pos2, w1, w2)


def _nsa_attn_kernel(qr_ref, qo_ref, kc_ref, vc_ref, ks_ref, vs_ref, kw_ref, vw_ref, gt_ref,
                     o_ref, m_scr, l_scr, acc_scr, *, seq):
    g_, tq, dk = NSA_GROUP, NSA_TQ, NSA_HEAD_DIM
    n_pair = g_ // 2
    nc = seq // CMP_STRIDE
    nsel = seq // SEL_BLOCK
    ch = min(NSA_SEL_CHUNK, seq)
    wlen = WINDOW + tq
    scale = dk ** -0.5
    qb = pl.program_id(1)
    start = qb * tq
    t_row = start + lax.broadcasted_iota(jnp.int32, (1, tq), 1)

    def heads_on_lanes(ref, lo, hi):
        return jnp.concatenate([ref[g * dk:(g + 1) * dk, :] for g in range(lo, hi)], axis=1)

    def tile_heads(x, n):
        return jnp.concatenate([x] * n, axis=1)

    def softmax_keys(s):
        m = jnp.max(s, axis=0, keepdims=True)
        m = jnp.where(m == -jnp.inf, 0.0, m)
        e = jnp.exp(s - m)
        return e * (1.0 / jnp.maximum(jnp.sum(e, axis=0, keepdims=True), 1e-30))

    qr_t = heads_on_lanes(qr_ref, 0, g_)
    qo_t = heads_on_lanes(qo_ref, 0, g_)

    n_end = lax.broadcasted_iota(jnp.int32, (nc, tq), 0) * CMP_STRIDE + (CMP_BLOCK - 1)
    bias_c = jnp.where(n_end <= t_row, 0.0, -jnp.inf)
    p_c = softmax_keys(_dot(kc_ref[0, 0], qr_t) * scale + tile_heads(bias_c, g_))
    o_c = _dot(vc_ref[0], p_c.astype(BF16))

    p_sum = p_c[:, :tq]
    for g in range(1, g_):
        p_sum = p_sum + p_c[:, g * tq:(g + 1) * tq]
    n_i = lax.broadcasted_iota(jnp.int32, (nsel, nc), 1) * CMP_STRIDE
    s_i = lax.broadcasted_iota(jnp.int32, (nsel, nc), 0) * SEL_BLOCK
    overlap = jnp.where((n_i < s_i + SEL_BLOCK) & (n_i + CMP_BLOCK > s_i), 1.0, 0.0).astype(BF16)
    p_hi = p_sum.astype(BF16)
    p_r1 = p_sum - p_hi.astype(F32)
    p_mid = p_r1.astype(BF16)
    p_lo = (p_r1 - p_mid.astype(F32)).astype(BF16)
    imp = _dot(overlap, p_hi) + _dot(overlap, p_mid) + _dot(overlap, p_lo)

    ids = lax.broadcasted_iota(jnp.int32, (nsel, tq), 0)
    cur = t_row >> SEL_SHIFT
    forced = (ids == 0) | (ids == cur) | (ids == cur - 1)
    imp = jnp.where(forced, FORCE_SCORE, imp)
    work = jnp.where(ids * SEL_BLOCK <= t_row, imp, -1.0)

    sel = jnp.zeros((nsel, tq), F32)
    for _ in range(min(SEL_TOPK, nsel)):
        mx = jnp.max(work, axis=0, keepdims=True)
        first = jnp.min(jnp.where(work == mx, ids, nsel), axis=0, keepdims=True)
        pick = ids == first
        sel = jnp.where(pick, 1.0, sel)
        work = jnp.where(pick, -3.0e38, work)
    sel = sel.astype(BF16)

    m_scr[...] = jnp.full(m_scr.shape, NEG_BIG, F32)
    l_scr[...] = jnp.zeros(l_scr.shape, F32)
    acc_scr[...] = jnp.zeros(acc_scr.shape, F32)
    blocks_per_chunk = ch // SEL_BLOCK
    k_blk = lax.broadcasted_iota(jnp.int32, (ch, nsel), 0) >> SEL_SHIFT
    b_i = lax.broadcasted_iota(jnp.int32, (ch, nsel), 1)
    k_off = lax.broadcasted_iota(jnp.int32, (ch, tq), 0)
    qo_pairs = [qo_t[:, p * 2 * tq:(p + 1) * 2 * tq] for p in range(n_pair)]

    def sel_step(c, carry):
        off = pl.multiple_of(c * ch, ch)
        k = ks_ref[pl.ds(off, ch), :]
        v_t = vs_ref[:, pl.ds(off, ch)]
        scores = [_dot(k, qo_pairs[p]) for p in range(n_pair)]
        state = [(m_scr[p], l_scr[p], acc_scr[p]) for p in range(n_pair)]
        expand = jnp.where(b_i == k_blk + c * blocks_per_chunk, 1.0, 0.0).astype(BF16)
        chosen = _dot(expand, sel)
        bias = jnp.where((chosen > 0.5) & (k_off + off <= t_row), 0.0, NEG_BIG)
        bias = tile_heads(bias, 2)
        new_state = []
        for p in range(n_pair):
            s = scores[p] * scale + bias
            m_old, l_old, acc_old = state[p]
            m_new = jnp.maximum(m_old, jnp.max(s, axis=0, keepdims=True))
            alpha = jnp.exp(m_old - m_new)
            e = jnp.exp(s - m_new)
            l_new = alpha * l_old + jnp.sum(e, axis=0, keepdims=True)
            acc_new = alpha * acc_old + _dot(v_t, e.astype(BF16))
            new_state.append((m_new, l_new, acc_new))
        for p in range(n_pair):
            m_scr[p], l_scr[p], acc_scr[p] = new_state[p]
        return carry

    lax.fori_loop(0, (start + tq + ch - 1) // ch, sel_step, 0)

    w0 = pl.multiple_of(jnp.maximum(start - WINDOW, 0), tq)
    kpos = w0 + lax.broadcasted_iota(jnp.int32, (wlen, tq), 0)
    bias_w = jnp.where((kpos <= t_row) & (kpos > t_row - WINDOW), 0.0, -jnp.inf)
    p_w = softmax_keys(_dot(kw_ref[pl.ds(w0, wlen), :], qo_t) * scale + tile_heads(bias_w, g_))
    o_w = _dot(vw_ref[:, pl.ds(w0, wlen)], p_w.astype(BF16))

    gt = gt_ref[0]
    for g in range(g_):
        p, half = divmod(g, 2)
        lanes = slice(half * tq, (half + 1) * tq)
        o_s = jnp.where(m_scr[p][:, lanes] > 0.5 * NEG_BIG,
                        acc_scr[p][:, lanes] * (1.0 / jnp.maximum(l_scr[p][:, lanes], 1e-30)), 0.0)
        cols = slice(g * tq, (g + 1) * tq)
        o_t = (gt[3 * g:3 * g + 1] * o_c[:, cols] + gt[3 * g + 1:3 * g + 2] * o_s
               + gt[3 * g + 2:3 * g + 3] * o_w[:, cols])
        o_ref[:, g * dk:(g + 1) * dk] = o_t.T.astype(BF16)


def _nsa_attention(raw, rot, plain, cmp_kv, gates):
    s = raw.shape[0]
    hk, g_, dk, tq = NSA_KV_HEADS, NSA_GROUP, NSA_HEAD_DIM, NSA_TQ
    nc = s // CMP_STRIDE
    qw = NSA_HEADS * dk
    qr_t, qo_t = raw[:, :qw].T, rot[:, :qw].T
    v_t = plain[:, 2 * hk * dk:].T
    vc_t = cmp_kv[1].transpose(0, 2, 1)
    gates_t = gates[:, :3 * NSA_HEADS].T.reshape(hk, 3 * g_, s)
    qspec = pl.BlockSpec((g_ * dk, tq), lambda h, i: (h, i))
    k_col = lambda base: pl.BlockSpec((s, dk), lambda h, i: (0, base + h))
    v_row = lambda base: pl.BlockSpec((dk, s), lambda h, i: (base + h, 0))
    n_pair = g_ // 2
    return pl.pallas_call(
        functools.partial(_nsa_attn_kernel, seq=s),
        name="nsa_attention",
        grid=(hk, s // tq),
        in_specs=[qspec, qspec,
                  pl.BlockSpec((1, 1, nc, dk), lambda h, i: (0, h, 0, 0)),
                  pl.BlockSpec((1, dk, nc), lambda h, i: (h, 0, 0)),
                  k_col(NSA_HEADS), v_row(0),
                  k_col(NSA_HEADS + hk), v_row(hk),
                  pl.BlockSpec((1, 3 * g_, tq), lambda h, i: (h, 0, i))],
        out_specs=pl.BlockSpec((tq, g_ * dk), lambda h, i: (i, h)),
        out_shape=jax.ShapeDtypeStruct((s, NSA_HEADS * dk), BF16),
        scratch_shapes=[pltpu.VMEM((n_pair, 1, 2 * tq), F32),
                        pltpu.VMEM((n_pair, 1, 2 * tq), F32),
                        pltpu.VMEM((n_pair, dk, 2 * tq), F32)],
        compiler_params=_params("parallel", "arbitrary"),
    )(qr_t, qo_t, cmp_kv, vc_t, rot, v_t, rot, v_t, gates_t)


def _diff_attn_kernel(q_ref, k_ref, v_ref, lam_ref, sg_ref, o_ref, qq_scr, m_scr, l_scr, acc_scr,
                      *, lambda_init):
    g_, tq, d = DIFF_GROUP, DIFF_TQ, DIFF_HEAD_DIM
    n_chain = 2 * g_
    scale = d ** -0.5
    qb = pl.program_id(1)

    first_half = lax.broadcasted_iota(jnp.int32, (2 * d, 1), 0) < d
    for g in range(g_):
        q_t = q_ref[g * 2 * d:(g + 1) * 2 * d, :] * scale
        zero = jnp.zeros_like(q_t)
        qq_scr[2 * g] = jnp.where(first_half, q_t, zero)
        qq_scr[2 * g + 1] = jnp.where(first_half, zero, q_t)
    m_scr[...] = jnp.full(m_scr.shape, NEG_BIG, F32)
    l_scr[...] = jnp.zeros(l_scr.shape, F32)
    acc_scr[...] = jnp.zeros(acc_scr.shape, F32)

    def chunk(off, width, diagonal):
        k = k_ref[pl.ds(off, width), :]
        v_t = v_ref[:, pl.ds(off, width)]
        if diagonal:
            key = lax.broadcasted_iota(jnp.int32, (width, tq), 0) + (off - qb * tq)
            causal = key <= lax.broadcasted_iota(jnp.int32, (width, tq), 1)
        state = [(m_scr[j], l_scr[j], acc_scr[j]) for j in range(n_chain)]
        new_state = []
        ahead = 3
        scores = [_dot(k, qq_scr[j]) for j in range(ahead)]
        for j in range(n_chain):
            if j + ahead < n_chain:
                scores.append(_dot(k, qq_scr[j + ahead]))
            s = scores[j]
            if diagonal:
                s = jnp.where(causal, s, NEG_BIG)
            m_old, l_old, acc_old = state[j]
            m_new = jnp.maximum(m_old, jnp.max(s, axis=0, keepdims=True))
            alpha = jnp.exp(m_old - m_new)
            p = jnp.exp(s - m_new)
            l_new = alpha * l_old + jnp.sum(p, axis=0, keepdims=True)
            acc_new = alpha * acc_old + _dot(v_t, p.astype(BF16))
            new_state.append((m_new, l_new, acc_new))
        for j in range(n_chain):
            m_scr[j], l_scr[j], acc_scr[j] = new_state[j]

    def pair_step(c, carry):
        chunk(pl.multiple_of(c * 2 * tq, 2 * tq), 2 * tq, False)
        return carry

    lax.fori_loop(0, qb // 2, pair_step, 0)

    @pl.when(qb % 2 == 1)
    def _():
        chunk(pl.multiple_of((qb - 1) * tq, tq), 2 * tq, True)

    @pl.when(qb % 2 == 0)
    def _():
        chunk(pl.multiple_of(qb * tq, tq), tq, True)

    lv = lam_ref[...]
    lam = (jnp.exp(jnp.sum(lv[0:1] * lv[1:2], axis=-1, keepdims=True))
           - jnp.exp(jnp.sum(lv[2:3] * lv[3:4], axis=-1, keepdims=True)) + lambda_init)
    sub_g = sg_ref[...]
    for g in range(g_):
        o1 = acc_scr[2 * g] / jnp.maximum(l_scr[2 * g], 1e-30)
        o2 = acc_scr[2 * g + 1] / jnp.maximum(l_scr[2 * g + 1], 1e-30)
        a = o1 - lam * o2
        ms = jnp.mean(a * a, axis=0, keepdims=True)
        a = a * lax.rsqrt(ms + NORM_EPS) * sub_g * (1.0 - lambda_init)
        o_ref[:, g * 2 * d:(g + 1) * 2 * d] = a.T.astype(BF16)


def _diff_attention(q_rot, k_rot, v, lam_vecs, subln_g, lambda_init):
    s = q_rot.shape[0]
    hk, g_, d, tq = DIFF_KV_HEADS, DIFF_GROUP, DIFF_HEAD_DIM, DIFF_TQ
    return pl.pallas_call(
        functools.partial(_diff_attn_kernel, lambda_init=lambda_init),
        name="diff_attention",
        grid=(hk, s // tq),
        in_specs=[pl.BlockSpec((g_ * 2 * d, tq), lambda h, i: (h, i)),
                  pl.BlockSpec((s, 2 * d), lambda h, i: (0, h)),
                  pl.BlockSpec((2 * d, s), lambda h, i: (h, 0)),
                  pl.BlockSpec((4, d), lambda h, i: (0, 0)),
                  pl.BlockSpec((2 * d, 1), lambda h, i: (0, 0))],
        out_specs=pl.BlockSpec((tq, g_ * 2 * d), lambda h, i: (i, h)),
        out_shape=jax.ShapeDtypeStruct((s, DIFF_HEADS * 2 * d), BF16),
        scratch_shapes=[pltpu.VMEM((2 * g_, 2 * d, tq), BF16),
                        pltpu.VMEM((2 * g_, 1, tq), F32),
                        pltpu.VMEM((2 * g_, 1, tq), F32),
                        pltpu.VMEM((2 * g_, 2 * d, tq), F32)],
        compiler_params=_params("parallel", "arbitrary"),
    )(q_rot.T, k_rot, v.T, lam_vecs, subln_g.reshape(2 * d, 1))


def _rope_tables(seq, head_dim, reps):
    rot = head_dim // ROPE_FRACTION
    half = rot // 2
    inv = 1.0 / (ROPE_THETA ** (jnp.arange(0, rot, 2, dtype=F32) / rot))
    ang = jnp.arange(seq, dtype=F32)[:, None] * inv[None, :]
    cos, sin = jnp.cos(ang), jnp.sin(ang)
    rest = head_dim - rot
    zeros_h = jnp.zeros((seq, half), F32)
    c = jnp.concatenate([cos, cos, jnp.ones((seq, rest), F32)], axis=-1)
    sa = jnp.concatenate([-sin, zeros_h, jnp.zeros((seq, rest), F32)], axis=-1)
    sb = jnp.concatenate([zeros_h, sin, jnp.zeros((seq, rest), F32)], axis=-1)
    return tuple(jnp.tile(t, (1, reps)) for t in (c, sa, sb)), half


def _nsa_layer(x, norm_g, w_in, cmp_pos, cmp_w1, cmp_w2, w_out, tables, shift):
    s = x.shape[0]
    hk, g_, dk = NSA_KV_HEADS, NSA_GROUP, NSA_HEAD_DIM
    qw, kvw = NSA_HEADS * dk, hk * dk
    part = lambda i: w_in[:, qw + i * kvw:qw + (i + 1) * kvw]
    w_rope = jnp.concatenate([w_in[:, :qw], part(2), part(4)], axis=1).astype(BF16)
    w_plain = jnp.concatenate([part(0), part(1), part(3), part(5)], axis=1).astype(BF16)
    n_gate = 3 * NSA_HEADS
    w_gate = jnp.pad(w_in[:, qw + 6 * kvw:], ((0, 0), (0, LANES - n_gate))).astype(BF16)

    raw, rot = _norm_proj_rope(x, norm_g, w_rope, tables, shift=shift, want_raw=True)
    plain = _norm_proj(x, norm_g, w_plain)
    gates = _norm_proj(x, norm_g, w_gate, act="sigmoid", out_dtype=F32)

    nc = s // CMP_STRIDE
    t2 = plain[:, :2 * kvw].reshape(nc, CMP_STRIDE, 2, hk, dk).transpose(2, 3, 0, 1, 4)
    t2 = t2.reshape(2, hk, nc, CMP_STRIDE * dk)
    cmp_kv = _compress(t2, cmp_pos.reshape(2, 2, CMP_STRIDE * dk), cmp_w1.astype(BF16),
                       cmp_w2.astype(BF16))
    o = _nsa_attention(raw, rot, plain, cmp_kv, gates)
    return _matmul_res(o, w_out.astype(BF16), x)


def kernel(x, attn_norm_g, mlp_norm_g, final_norm_g, nsa_w_in, nsa_cmp_pos, nsa_cmp_w1, nsa_cmp_w2, nsa_w_out, kv_norm_g, kv_w_shared, diff_w_q, diff_lambda, diff_subln_g, diff_w_out, mlp_w_up, mlp_w_down):
    b, s, d = x.shape
    tables_a, shift_a = _rope_tables(s, NSA_HEAD_DIM, 1)
    tables_b, shift_b = _rope_tables(s, DIFF_HEAD_DIM, 2)
    kcols = DIFF_KV_HEADS * 2 * DIFF_HEAD_DIM
    outs = []
    for bi in range(b):
        xs = x[bi]
        k_sh = v_sh = None
        for layer in range(DEPTH):
            if layer < N_A_LAYERS:
                xs = _nsa_layer(xs, attn_norm_g[layer], nsa_w_in[layer], nsa_cmp_pos[layer],
                                nsa_cmp_w1[layer], nsa_cmp_w2[layer], nsa_w_out[layer],
                                tables_a, shift_a)
            else:
                j = layer - N_A_LAYERS
                if j == 0:
                    k_sh = _norm_proj_rope(xs, kv_norm_g, kv_w_shared[:, :kcols].astype(BF16),
                                           tables_b, shift=shift_b, want_raw=False)
                    v_sh = _norm_proj(xs, kv_norm_g, kv_w_shared[:, kcols:].astype(BF16))
                lambda_init = 0.8 - 0.6 * math.exp(-0.3 * layer)
                q = _norm_proj_rope(xs, attn_norm_g[layer], diff_w_q[j].astype(BF16), tables_b,
                                    shift=shift_b, want_raw=False)
                o = _diff_attention(q, k_sh, v_sh, diff_lambda[j], diff_subln_g[j], lambda_init)
                xs = _matmul_res(o, diff_w_out[j].astype(BF16), xs)
            xs = _mlp(xs, mlp_norm_g[layer], mlp_w_up[layer].astype(BF16),
                      mlp_w_down[layer].astype(BF16), final_norm_g,
                      final_norm=(layer == DEPTH - 1))
        outs.append(xs)
    return jnp.stack(outs, axis=0)
```

```python
import functools
import math

import jax
import jax.numpy as jnp
from jax import lax
from jax.experimental import pallas as pl
from jax.experimental.pallas import tpu as pltpu

F32 = jnp.float32
BF16 = jnp.bfloat16

D_MODEL = 2048
DEPTH = 4
N_A_LAYERS = DEPTH // 2

NSA_HEADS = 16
NSA_KV_HEADS = 4
NSA_GROUP = NSA_HEADS // NSA_KV_HEADS
NSA_HEAD_DIM = D_MODEL // NSA_HEADS
CMP_BLOCK = 32
CMP_STRIDE = 16
CMP_HIDDEN = 4 * NSA_HEAD_DIM
SEL_BLOCK = 64
SEL_SHIFT = SEL_BLOCK.bit_length() - 1
SEL_TOPK = 16
WINDOW = 512
FORCE_SCORE = 1.0e4

DIFF_HEADS = 16
DIFF_KV_HEADS = 4
DIFF_GROUP = DIFF_HEADS // DIFF_KV_HEADS
DIFF_HEAD_DIM = D_MODEL // (2 * DIFF_HEADS)

D_FF = 4 * D_MODEL
ROPE_THETA = 500000.0
ROPE_FRACTION = 4
NORM_EPS = 1e-6

LANES = 128
VMEM_LIMIT = 48 * 1024 * 1024
VMEM_BUDGET = 40 * 1024 * 1024
NEG_BIG = -1.0e30
SCORE_FLOOR = -1.0e28
LOG2E = 1.4426950408889634

NSA_TQ = 256
NSA_SEL_CHUNK = 512
NSA_CHAIN_LANES = 256
DIFF_TQ = 256
ROW_TILE = 512
COL_TILE = 512
FF_TILE = 512


def _params(*sem):
    return pltpu.CompilerParams(dimension_semantics=sem, vmem_limit_bytes=VMEM_LIMIT)


def _dot(a, b):
    return jnp.dot(a, b, preferred_element_type=F32)


def _dot_nt(a, b):
    return lax.dot_general(a, b, (((1,), (1,)), ((), ())), preferred_element_type=F32)


def _rms(x, g):
    ms = jnp.mean(x * x, axis=-1, keepdims=True)
    return x * lax.rsqrt(ms + NORM_EPS) * g


def _proj_tiles(s, k, n, row_bytes):
    weight = 2 * k * n * 2
    for tm in (ROW_TILE, ROW_TILE // 2):
        if weight + tm * row_bytes <= VMEM_BUDGET:
            return min(tm, s), n
    return min(ROW_TILE, s), min(COL_TILE, n)


def _norm_proj_kernel(x_ref, g_ref, w_ref, o_ref, h_scr, *, act):
    @pl.when(pl.program_id(1) == 0)
    def _():
        h_scr[...] = _rms(x_ref[...], g_ref[...]).astype(BF16)

    acc = _dot(h_scr[...], w_ref[...])
    if act == "sigmoid":
        acc = jax.nn.sigmoid(acc)
    o_ref[...] = acc.astype(o_ref.dtype)


def _norm_proj(x, g, w, *, act=None, out_dtype=BF16):
    s, d = x.shape
    n = w.shape[1]
    tm, tn = _proj_tiles(s, d, n, 2 * d * 4 + d * 2 + n * 4 + 2 * n * jnp.dtype(out_dtype).itemsize)
    return pl.pallas_call(
        functools.partial(_norm_proj_kernel, act=act),
        name="norm_proj",
        grid=(s // tm, n // tn),
        in_specs=[pl.BlockSpec((tm, d), lambda i, j: (i, 0)),
                  pl.BlockSpec((1, d), lambda i, j: (0, 0)),
                  pl.BlockSpec((d, tn), lambda i, j: (0, j))],
        out_specs=pl.BlockSpec((tm, tn), lambda i, j: (i, j)),
        out_shape=jax.ShapeDtypeStruct((s, n), out_dtype),
        scratch_shapes=[pltpu.VMEM((tm, d), BF16)],
        compiler_params=_params("parallel", "arbitrary"),
    )(x, g.reshape(1, d), w)


def _norm_proj_rope_kernel(x_ref, g_ref, w_ref, c_ref, sa_ref, sb_ref, *rest, shift, want_raw):
    if want_raw:
        raw_ref, rot_ref, h_scr = rest
    else:
        rot_ref, h_scr = rest

    @pl.when(pl.program_id(1) == 0)
    def _():
        h_scr[...] = _rms(x_ref[...], g_ref[...]).astype(BF16)

    acc = _dot(h_scr[...], w_ref[...])
    if want_raw:
        raw_ref[...] = acc.astype(BF16)
    c, sa, sb = c_ref[...], sa_ref[...], sb_ref[...]
    for blk in range(acc.shape[1] // LANES):
        a = acc[:, blk * LANES:(blk + 1) * LANES]
        r = a * c + pltpu.roll(a, LANES - shift, 1) * sa + pltpu.roll(a, shift, 1) * sb
        rot_ref[:, blk * LANES:(blk + 1) * LANES] = r.astype(BF16)


def _norm_proj_rope(x, g, w, tables, *, shift, want_raw):
    s, d = x.shape
    n = w.shape[1]
    n_out = 2 if want_raw else 1
    tm, tn = _proj_tiles(s, d, n, 2 * d * 4 + d * 2 + n * 4 + n_out * 2 * n * 2 + 3 * 2 * LANES * 4)
    tab_spec =pl.BlockSpec((tm, LANES), lambda i, j: (i, 0))
    out_spec = pl.BlockSpec((tm, tn), lambda i, j: (i, j))
    out_sd = jax.ShapeDtypeStruct((s, n), BF16)
    return pl.pallas_call(
        functools.partial(_norm_proj_rope_kernel, shift=shift, want_raw=want_raw),
        name="norm_proj_rope",
        grid=(s // tm, n // tn),
        in_specs=[pl.BlockSpec((tm, d), lambda i, j: (i, 0)),
                  pl.BlockSpec((1, d), lambda i, j: (0, 0)),
                  pl.BlockSpec((d, tn), lambda i, j: (0, j)),
                  tab_spec, tab_spec, tab_spec],
        out_specs=[out_spec, out_spec] if want_raw else out_spec,
        out_shape=[out_sd, out_sd] if want_raw else out_sd,
        scratch_shapes=[pltpu.VMEM((tm, d), BF16)],
        compiler_params=_params("parallel", "arbitrary"),
    )(x, g.reshape(1, d), w, *tables)


def _matmul_res_kernel(a_ref, w_ref, r_ref, o_ref):
    o_ref[...] = r_ref[...] + _dot(a_ref[...], w_ref[...])


def _matmul_res(a, w, res):
    s, k = a.shape
    n = w.shape[1]
    tm, tn = _proj_tiles(s, k, n, 2 * k * 2 + 2 * 2 * n * 4 + n * 4)
    return pl.pallas_call(
        _matmul_res_kernel,
        name="matmul_res",
        grid=(s // tm, n // tn),
        in_specs=[pl.BlockSpec((tm, k), lambda i, j: (i, 0)),
                  pl.BlockSpec((k, tn), lambda i, j: (0, j)),
                  pl.BlockSpec((tm, tn), lambda i, j: (i, j))],
        out_specs=pl.BlockSpec((tm, tn), lambda i, j: (i, j)),
        out_shape=jax.ShapeDtypeStruct((s, n), F32),
        compiler_params=_params("parallel", "arbitrary"),
    )(a, w, res)


def _mlp_kernel(x_ref, g_ref, wu_ref, wd_ref, fg_ref, o_ref, h_scr, acc_scr, *, final_norm):
    j = pl.program_id(1)

    @pl.when(j == 0)
    def _():
        h_scr[...] = _rms(x_ref[...], g_ref[...]).astype(BF16)
        acc_scr[...] = jnp.zeros_like(acc_scr)

    u = _dot(h_scr[...], wu_ref[...])
    u = jnp.square(jnp.maximum(u, 0.0)).astype(BF16)
    acc_scr[...] += _dot(u, wd_ref[...])

    @pl.when(j == pl.num_programs(1) - 1)
    def _():
        y = x_ref[...] + acc_scr[...]
        if final_norm:
            y = _rms(y, fg_ref[...])
        o_ref[...] = y


def _mlp(x, g, w_up, w_down, final_g, *, final_norm):
    s, d = x.shape
    ff = w_up.shape[1]
    tm, tf = min(ROW_TILE, s), min(FF_TILE, ff)
    return pl.pallas_call(
        functools.partial(_mlp_kernel, final_norm=final_norm),
        name="sq_relu_mlp",
        grid=(s // tm, ff // tf),
        in_specs=[pl.BlockSpec((tm, d), lambda i, j: (i, 0)),
                  pl.BlockSpec((1, d), lambda i, j: (0, 0)),
                  pl.BlockSpec((d, tf), lambda i, j: (0, j)),
                  pl.BlockSpec((tf, d), lambda i, j: (j, 0)),
                  pl.BlockSpec((1, d), lambda i, j: (0, 0))],
        out_specs=pl.BlockSpec((tm, d), lambda i, j: (i, 0)),
        out_shape=jax.ShapeDtypeStruct((s, d), F32),
        scratch_shapes=[pltpu.VMEM((tm, d), BF16), pltpu.VMEM((tm, d), F32)],
        compiler_params=_params("parallel", "arbitrary"),
    )(x, g.reshape(1, d), w_up, w_down, final_g.reshape(1, d))


def _compress_kernel(t_ref, pos_ref, w1_ref, w2_ref, o_ref):
    t = t_ref[0, 0].astype(F32)
    pos = pos_ref[0]
    half = t.shape[1]
    a = _dot((t + pos[0:1]).astype(BF16), w1_ref[0, :half, :])
    b = _dot((t + pos[1:2]).astype(BF16), w1_ref[0, half:, :])
    nc = t.shape[0]
    hid = jax.nn.gelu(a + pltpu.roll(b, nc - 1, 0))
    o_ref[0, 0] = _dot(hid.astype(BF16), w2_ref[0]).astype(BF16)


def _compress(t2, pos2, w1, w2):
    _, hk, nc, width = t2.shape
    dk = w2.shape[-1]
    return pl.pallas_call(
        _compress_kernel,
        name="nsa_compress",
        grid=(2, hk),
        in_specs=[pl.BlockSpec((1, 1, nc, width), lambda j, h: (j, h, 0, 0)),
                  pl.BlockSpec((1, 2, width), lambda j, h: (j, 0, 0)),
                  pl.BlockSpec((1, 2 * width, CMP_HIDDEN), lambda j, h: (j, 0, 0)),
                  pl.BlockSpec((1, CMP_HIDDEN, dk), lambda j, h: (j, 0, 0))],
        out_specs=pl.BlockSpec((1, 1, nc, dk), lambda j, h: (j, h, 0, 0)),
        out_shape=jax.ShapeDtypeStruct((2, hk, nc, dk), BF16),
        compiler_params=_params("parallel", "arbitrary"),
    )(t2, pos2, w1, w2)


def _nsa_attn_kernel(qr_ref, qo_ref, kc_ref, vc_ref, ks_ref, vs_ref, kw_ref, vw_ref, gt_ref,
                     o_ref, rhs_scr, m_scr, l_scr, acc_scr, *, seq):
    g_, tq, dk = NSA_GROUP, NSA_TQ, NSA_HEAD_DIM
    cw = NSA_CHAIN_LANES
    n_chain = g_ * tq // cw
    nc = seq // CMP_STRIDE
    nsel = seq // SEL_BLOCK
    ch = min(NSA_SEL_CHUNK, seq)
    wlen = WINDOW + tq
    c2 = dk ** -0.5 * LOG2E
    qb = pl.program_id(1)
    start = qb * tq
    t_row = start + lax.broadcasted_iota(jnp.int32, (1, tq), 1)

    def heads_on_lanes(ref, lo, hi):
        return jnp.concatenate([ref[g * dk:(g + 1) * dk, :] for g in range(lo, hi)], axis=1)

    def tile_heads(x, n):
        return jnp.concatenate([x] * n, axis=1)

    def softmax_keys(s2):
        m = jnp.max(s2, axis=0, keepdims=True)
        m = jnp.where(m == -jnp.inf, 0.0, m)
        e = jnp.exp2(s2 - m)
        return e, 1.0 / jnp.maximum(jnp.sum(e, axis=0, keepdims=True), 1e-30)

    qr_t = heads_on_lanes(qr_ref, 0, g_)
    qo_t = heads_on_lanes(qo_ref, 0, g_)

    n_end = lax.broadcasted_iota(jnp.int32, (nc, tq), 0) * CMP_STRIDE + (CMP_BLOCK - 1)
    bias_c = jnp.where(n_end <= t_row, 0.0, -jnp.inf)
    e_c, r_c = softmax_keys(_dot(kc_ref[0, 0], qr_t) * c2 + tile_heads(bias_c, g_))
    p_c = e_c * r_c
    o_c = _dot(vc_ref[0], p_c.astype(BF16))

    w0 = pl.multiple_of(jnp.maximum(start - WINDOW, 0), tq)
    kpos = w0 + lax.broadcasted_iota(jnp.int32, (wlen, tq), 0)
    bias_w = jnp.where((kpos <= t_row) & (kpos > t_row - WINDOW), 0.0, -jnp.inf)
    e_w, r_w = softmax_keys(_dot(kw_ref[pl.ds(w0, wlen), :], qo_t) * c2 + tile_heads(bias_w, g_))
    o_w = _dot(vw_ref[:, pl.ds(w0, wlen)], e_w.astype(BF16)) * r_w

    p_sum = p_c[:, :tq]
    for g in range(1, g_):
        p_sum = p_sum + p_c[:, g * tq:(g + 1) * tq]
    n_i = lax.broadcasted_iota(jnp.int32, (nsel, nc), 1) * CMP_STRIDE
    s_i = lax.broadcasted_iota(jnp.int32, (nsel, nc), 0) * SEL_BLOCK
    overlap = jnp.where((n_i < s_i + SEL_BLOCK) & (n_i + CMP_BLOCK > s_i), 1.0, 0.0).astype(BF16)
    p_hi = p_sum.astype(BF16)
    p_r1 = p_sum - p_hi.astype(F32)
    p_mid = p_r1.astype(BF16)
    p_lo = (p_r1 - p_mid.astype(F32)).astype(BF16)
    imp = _dot(overlap, p_hi) + _dot(overlap, p_mid) + _dot(overlap, p_lo)

    ids = lax.broadcasted_iota(jnp.int32, (nsel, tq), 0)
    cur = t_row >> SEL_SHIFT
    forced = (ids == 0) | (ids == cur) | (ids == cur - 1)
    imp = jnp.where(forced, FORCE_SCORE, imp)
    work = jnp.where(ids * SEL_BLOCK <= t_row, imp, -1.0)

    sel = jnp.zeros((nsel, tq), F32)
    for _ in range(min(SEL_TOPK, nsel)):
        mx = jnp.max(work, axis=0, keepdims=True)
        first = jnp.min(jnp.where(work == mx, ids, nsel), axis=0, keepdims=True)
        pick = ids == first
        sel = jnp.where(pick, 1.0, sel)
        work = jnp.where(pick, -3.0e38, work)

    neg_sel = tile_heads(jnp.where(sel > 0.5, 0.0, NEG_BIG).astype(BF16), cw // tq)
    for j in range(n_chain):
        rhs_scr[j] = jnp.concatenate([qo_t[:, j * cw:(j + 1) * cw], neg_sel], axis=0)
    m_scr[...] = jnp.full(m_scr.shape, NEG_BIG, F32)
    l_scr[...] = jnp.zeros(l_scr.shape, F32)
    acc_scr[...] = jnp.zeros(acc_scr.shape, F32)

    def sel_chunk(off, width, diagonal):
        k_aug = ks_ref[pl.ds(off, width), :]
        v_t = vs_ref[:, pl.ds(off, width)]
        scores = [_dot(k_aug, rhs_scr[j]) for j in range(n_chain)]
        state = [(m_scr[j], l_scr[j], acc_scr[j]) for j in range(n_chain)]
        if diagonal:
            key = off + lax.broadcasted_iota(jnp.int32, (width, tq), 0)
            causal = tile_heads(jnp.where(key <= t_row, 0.0, NEG_BIG), cw // tq)
        new_state = []
        for j in range(n_chain):
            s = scores[j]
            if diagonal:
                s = s + causal
            s = s * c2
            m_old, l_old, acc_old = state[j]
            m_new = jnp.maximum(m_old, jnp.max(s, axis=0, keepdims=True))
            alpha = jnp.exp2(m_old - m_new)
            e = jnp.exp2(s - m_new)
            l_new = alpha * l_old + jnp.sum(e, axis=0, keepdims=True)
            acc_new = alpha * acc_old + _dot(v_t, e.astype(BF16))
            new_state.append((m_new, l_new, acc_new))
        for j in range(n_chain):
            m_scr[j], l_scr[j], acc_scr[j] = new_state[j]

    def pair_step(c, carry):
        sel_chunk(pl.multiple_of(c * 2 * ch, 2 * ch), 2 * ch, False)
        return carry

    n_full = (start + tq - 1) // ch
    lax.fori_loop(0, n_full // 2, pair_step, 0)

    @pl.when(n_full % 2 == 1)
    def _():
        sel_chunk(pl.multiple_of((n_full - 1) * ch, ch), ch, False)

    sel_chunk(pl.multiple_of(n_full * ch, ch), ch, True)

    gt = gt_ref[0]
    for g in range(g_):
        j = g * tq // cw
        lanes = slice(g * tq - j * cw, g * tq - j * cw + tq)
        o_s = jnp.where(m_scr[j][:, lanes] > SCORE_FLOOR,
                        acc_scr[j][:, lanes] * (1.0 / jnp.maximum(l_scr[j][:, lanes], 1e-30)), 0.0)
        cols = slice(g * tq, (g + 1) * tq)
        o_t = (gt[3 * g:3 * g + 1] * o_c[:, cols] + gt[3 * g + 1:3 * g + 2] * o_s
               + gt[3 * g + 2:3 * g + 3] * o_w[:, cols])
        o_ref[:, g * dk:(g + 1) * dk] = o_t.T.astype(BF16)


def _nsa_attention(raw, rot, plain, cmp_kv, gates):
    s = raw.shape[0]
    hk, g_, dk, tq = NSA_KV_HEADS, NSA_GROUP, NSA_HEAD_DIM, NSA_TQ
    nc = s // CMP_STRIDE
    qw = NSA_HEADS * dk
    qr_t, qo_t = raw[:, :qw].T, rot[:, :qw].T
    v_t = plain[:, 2 * hk * dk:].T
    vc_t = cmp_kv[1].transpose(0, 2, 1)
    gates_t = gates[:, :3 * NSA_HEADS].T.reshape(hk, 3 * g_, s)
    nsel = s // SEL_BLOCK
    one_hot = (jnp.arange(s)[:, None] >> SEL_SHIFT == jnp.arange(nsel)[None, :]).astype(BF16)
    k_sel = rot[:, qw:qw + hk * dk].reshape(s, hk, dk)
    k_aug = jnp.concatenate([k_sel, jnp.broadcast_to(one_hot[:, None, :], (s, hk, nsel))], axis=-1)
    k_aug = k_aug.reshape(s, hk * (dk + nsel))
    qspec = pl.BlockSpec((g_ * dk, tq), lambda h, i: (h, i))
    k_col = lambda base: pl.BlockSpec((s, dk), lambda h, i: (0, base + h))
    v_row = lambda base: pl.BlockSpec((dk, s), lambda h, i: (base + h, 0))
    cw = NSA_CHAIN_LANES
    assert cw % tq == 0 and (g_ * tq) % cw == 0 and NSA_SEL_CHUNK % tq == 0
    n_chain = g_ * tq // cw
    return pl.pallas_call(
        functools.partial(_nsa_attn_kernel, seq=s),
        name="nsa_attention",
        grid=(hk, s // tq),
        in_specs=[qspec, qspec,
                  pl.BlockSpec((1, 1, nc, dk), lambda h, i: (0, h, 0, 0)),
                  pl.BlockSpec((1, dk, nc), lambda h, i: (h, 0, 0)),
                  pl.BlockSpec((s, dk + nsel), lambda h, i: (0, h)), v_row(0),
                  k_col(NSA_HEADS + hk), v_row(hk),
                  pl.BlockSpec((1, 3 * g_, tq), lambda h, i: (h, 0, i))],
        out_specs=pl.BlockSpec((tq, g_ * dk), lambda h, i: (i, h)),
        out_shape=jax.ShapeDtypeStruct((s, NSA_HEADS * dk), BF16),
        scratch_shapes=[pltpu.VMEM((n_chain, dk + nsel, cw), BF16),
                        pltpu.VMEM((n_chain, 1, cw), F32),
                        pltpu.VMEM((n_chain, 1, cw), F32),
                        pltpu.VMEM((n_chain, dk, cw), F32)],
        compiler_params=_params("parallel", "arbitrary"),
    )(qr_t, qo_t, cmp_kv, vc_t, k_aug, v_t, rot, v_t, gates_t)


def _diff_attn_kernel(q_ref, k_ref, v_ref, lam_ref, sg_ref, o_ref, qq_scr, m_scr, l_scr, acc_scr,
                      *, lambda_init):
    g_, tq, d = DIFF_GROUP, DIFF_TQ, DIFF_HEAD_DIM
    n_chain = 2 * g_
    scale = d ** -0.5
    qb = pl.program_id(1)

    first_half = lax.broadcasted_iota(jnp.int32, (2 * d, 1), 0) < d
    for g in range(g_):
        q_t = q_ref[g * 2 * d:(g + 1) * 2 * d, :] * scale
        zero = jnp.zeros_like(q_t)
        qq_scr[2 * g] = jnp.where(first_half, q_t, zero)
        qq_scr[2 * g + 1] = jnp.where(first_half, zero, q_t)
    m_scr[...] = jnp.full(m_scr.shape, NEG_BIG, F32)
    l_scr[...] = jnp.zeros(l_scr.shape, F32)
    acc_scr[...] = jnp.zeros(acc_scr.shape, F32)

    def chunk(off, width, diagonal):
        k = k_ref[pl.ds(off, width), :]
        v_t = v_ref[:, pl.ds(off, width)]
        if diagonal:
            key = lax.broadcasted_iota(jnp.int32, (width, tq), 0) + (off - qb * tq)
            causal = key <= lax.broadcasted_iota(jnp.int32, (width, tq), 1)
        state = [(m_scr[j], l_scr[j], acc_scr[j]) for j in range(n_chain)]
        new_state = []
        ahead = 3
        scores = [_dot(k, qq_scr[j]) for j in range(ahead)]
        for j in range(n_chain):
            if j + ahead < n_chain:
                scores.append(_dot(k, qq_scr[j + ahead]))
            s = scores[j]
            if diagonal:
                s = jnp.where(causal, s, NEG_BIG)
            m_old, l_old, acc_old = state[j]
            m_new = jnp.maximum(m_old, jnp.max(s, axis=0, keepdims=True))
            alpha = jnp.exp(m_old - m_new)
            p = jnp.exp(s - m_new)
            l_new = alpha * l_old + jnp.sum(p, axis=0, keepdims=True)
            acc_new = alpha * acc_old + _dot(v_t, p.astype(BF16))
            new_state.append((m_new, l_new, acc_new))
        for j in range(n_chain):
            m_scr[j], l_scr[j], acc_scr[j] = new_state[j]

    def pair_step(c, carry):
        chunk(pl.multiple_of(c * 2 * tq, 2 * tq), 2 * tq, False)
        return carry

    lax.fori_loop(0, qb // 2, pair_step, 0)

    @pl.when(qb % 2 == 1)
    def _():
        chunk(pl.multiple_of((qb - 1) * tq, tq), 2 * tq, True)

    @pl.when(qb % 2 == 0)
    def _():
        chunk(pl.multiple_of(qb * tq, tq), tq, True)

    lv = lam_ref[...]
    lam = (jnp.exp(jnp.sum(lv[0:1] * lv[1:2], axis=-1, keepdims=True))
           - jnp.exp(jnp.sum(lv[2:3] * lv[3:4], axis=-1, keepdims=True)) + lambda_init)
    sub_g = sg_ref[...]
    for g in range(g_):
        o1 = acc_scr[2 * g] / jnp.maximum(l_scr[2 * g], 1e-30)
        o2 = acc_scr[2 * g + 1] / jnp.maximum(l_scr[2 * g + 1], 1e-30)
        a = o1 - lam * o2
        ms = jnp.mean(a * a, axis=0, keepdims=True)
        a = a * lax.rsqrt(ms + NORM_EPS) * sub_g * (1.0 - lambda_init)
        o_ref[:, g * 2 * d:(g + 1) * 2 * d] = a.T.astype(BF16)


def _diff_attention(q_rot, k_rot, v, lam_vecs, subln_g, lambda_init):
    s = q_rot.shape[0]
    hk, g_, d, tq = DIFF_KV_HEADS, DIFF_GROUP, DIFF_HEAD_DIM, DIFF_TQ
    return pl.pallas_call(
        functools.partial(_diff_attn_kernel, lambda_init=lambda_init),
        name="diff_attention",
        grid=(hk, s // tq),
        in_specs=[pl.BlockSpec((g_ * 2 * d, tq), lambda h, i: (h, i)),
                  pl.BlockSpec((s, 2 * d), lambda h, i: (0, h)),
                  pl.BlockSpec((2 * d, s), lambda h, i: (h, 0)),
                  pl.BlockSpec((4, d), lambda h, i: (0, 0)),
                  pl.BlockSpec((2 * d, 1), lambda h, i: (0, 0))],
        out_specs=pl.BlockSpec((tq, g_ * 2 * d), lambda h, i: (i, h)),
        out_shape=jax.ShapeDtypeStruct((s, DIFF_HEADS * 2 * d), BF16),
        scratch_shapes=[pltpu.VMEM((2 * g_, 2 * d, tq), BF16),
                        pltpu.VMEM((2 * g_, 1, tq), F32),
                        pltpu.VMEM((2 * g_, 1, tq), F32),
                        pltpu.VMEM((2 * g_, 2 * d, tq), F32)],
        compiler_params=_params("parallel", "arbitrary"),
    )(q_rot.T, k_rot, v.T, lam_vecs, subln_g.reshape(2 * d, 1))


def _rope_tables(seq, head_dim, reps):
    rot = head_dim // ROPE_FRACTION
    half = rot // 2
    inv = 1.0 / (ROPE_THETA ** (jnp.arange(0, rot, 2, dtype=F32) / rot))
    ang = jnp.arange(seq, dtype=F32)[:, None] * inv[None, :]
    cos, sin = jnp.cos(ang), jnp.sin(ang)
    rest = head_dim - rot
    zeros_h = jnp.zeros((seq, half), F32)
    c = jnp.concatenate([cos, cos, jnp.ones((seq, rest), F32)], axis=-1)
    sa = jnp.concatenate([-sin, zeros_h, jnp.zeros((seq, rest), F32)], axis=-1)
    sb = jnp.concatenate([zeros_h, sin, jnp.zeros((seq, rest), F32)], axis=-1)
    return tuple(jnp.tile(t, (1, reps)) for t in (c, sa, sb)), half


def _nsa_layer(x, norm_g, w_in, cmp_pos, cmp_w1, cmp_w2, w_out, tables, shift):
    s = x.shape[0]
    hk, g_, dk = NSA_KV_HEADS, NSA_GROUP, NSA_HEAD_DIM
    qw, kvw = NSA_HEADS * dk, hk * dk
    part = lambda i: w_in[:, qw + i * kvw:qw + (i + 1) * kvw]
    w_rope = jnp.concatenate([w_in[:, :qw], part(2), part(4)], axis=1).astype(BF16)
    w_plain = jnp.concatenate([part(0), part(1), part(3), part(5)], axis=1).astype(BF16)
    n_gate = 3 * NSA_HEADS
    w_gate = jnp.pad(w_in[:, qw + 6 * kvw:], ((0, 0), (0, LANES - n_gate))).astype(BF16)

    raw, rot = _norm_proj_rope(x, norm_g, w_rope, tables, shift=shift, want_raw=True)
    plain = _norm_proj(x, norm_g, w_plain)
    gates = _norm_proj(x, norm_g, w_gate, act="sigmoid", out_dtype=F32)

    nc = s // CMP_STRIDE
    t2 = plain[:, :2 * kvw].reshape(nc, CMP_STRIDE, 2, hk, dk).transpose(2, 3, 0, 1, 4)
    t2 = t2.reshape(2, hk, nc, CMP_STRIDE * dk)
    cmp_kv = _compress(t2, cmp_pos.reshape(2, 2, CMP_STRIDE * dk), cmp_w1.astype(BF16),
                       cmp_w2.astype(BF16))
    o = _nsa_attention(raw, rot, plain, cmp_kv, gates)
    return _matmul_res(o, w_out.astype(BF16), x)


def kernel(x, attn_norm_g, mlp_norm_g, final_norm_g, nsa_w_in, nsa_cmp_pos, nsa_cmp_w1, nsa_cmp_w2, nsa_w_out, kv_norm_g, kv_w_shared, diff_w_q, diff_lambda, diff_subln_g, diff_w_out, mlp_w_up, mlp_w_down):
    b, s, d = x.shape
    tables_a, shift_a = _rope_tables(s, NSA_HEAD_DIM, 1)
    tables_b, shift_b = _rope_tables(s, DIFF_HEAD_DIM, 2)
    kcols = DIFF_KV_HEADS * 2 * DIFF_HEAD_DIM
    outs = []
    for bi in range(b):
        xs = x[bi]
        k_sh = v_sh = None
        for layer in range(DEPTH):
            if layer < N_A_LAYERS:
                xs = _nsa_layer(xs, attn_norm_g[layer], nsa_w_in[layer], nsa_cmp_pos[layer],
                                nsa_cmp_w1[layer], nsa_cmp_w2[layer], nsa_w_out[layer],
                                tables_a, shift_a)
            else:
                j = layer - N_A_LAYERS
                if j == 0:
                    k_sh = _norm_proj_rope(xs, kv_norm_g, kv_w_shared[:, :kcols].astype(BF16),
                                           tables_b, shift=shift_b, want_raw=False)
                    v_sh = _norm_proj(xs, kv_norm_g, kv_w_shared[:, kcols:].astype(BF16))
                lambda_init = 0.8 - 0.6 * math.exp(-0.3 * layer)
                q = _norm_proj_rope(xs, attn_norm_g[layer], diff_w_q[j].astype(BF16), tables_b,
                                    shift=shift_b, want_raw=False)
                o = _diff_attention(q, k_sh, v_sh, diff_lambda[j], diff_subln_g[j], lambda_init)
                xs = _matmul_res(o, diff_w_out[j].astype(BF16), xs)
            xs = _mlp(xs, mlp_norm_g[layer], mlp_w_up[layer].astype(BF16),
                      mlp_w_down[layer].astype(BF16), final_norm_g,
                      final_norm=(layer == DEPTH - 1))
        outs.append(xs)
    return jnp.stack(outs, axis=0)
```

```python
import functools
import math

import jax
import jax.numpy as jnp
from jax import lax
from jax.experimental import pallas as pl
from jax.experimental.pallas import tpu as pltpu

F32 = jnp.float32
BF16 = jnp.bfloat16

D_MODEL = 2048
DEPTH = 4
N_A_LAYERS = DEPTH // 2

NSA_HEADS = 16
NSA_KV_HEADS = 4
NSA_GROUP = NSA_HEADS // NSA_KV_HEADS
NSA_HEAD_DIM = D_MODEL // NSA_HEADS
CMP_BLOCK = 32
CMP_STRIDE = 16
CMP_HIDDEN = 4 * NSA_HEAD_DIM
SEL_BLOCK = 64
SEL_SHIFT = SEL_BLOCK.bit_length() - 1
SEL_TOPK = 16
WINDOW = 512
FORCE_SCORE = 1.0e4

DIFF_HEADS = 16
DIFF_KV_HEADS = 4
DIFF_GROUP = DIFF_HEADS // DIFF_KV_HEADS
DIFF_HEAD_DIM = D_MODEL // (2 * DIFF_HEADS)

D_FF = 4 * D_MODEL
ROPE_THETA = 500000.0
ROPE_FRACTION = 4
NORM_EPS = 1e-6

LANES = 128
VMEM_LIMIT = 48 * 1024 * 1024
VMEM_BUDGET = 40 * 1024 * 1024
NEG_BIG = -1.0e30
SCORE_FLOOR = -1.0e28
LOG2E = 1.4426950408889634

NSA_TQ = 256
NSA_SEL_CHUNK = 512
NSA_CHAIN_LANES = 256
DIFF_TQ = 256
ROW_TILE = 512
COL_TILE = 512
FF_TILE = 1024


def _params(*sem):
    return pltpu.CompilerParams(dimension_semantics=sem, vmem_limit_bytes=VMEM_LIMIT)


def _dot(a, b):
    return jnp.dot(a, b, preferred_element_type=F32)


def _dot_nt(a, b):
    return lax.dot_general(a, b, (((1,), (1,)), ((), ())), preferred_element_type=F32)


def _rms(x, g):
    ms = jnp.mean(x * x, axis=-1, keepdims=True)
    return x * lax.rsqrt(ms + NORM_EPS) * g


def _proj_tiles(s, k, n, row_bytes):
    weight = 2 * k * n * 2
    for tm in (ROW_TILE, ROW_TILE // 2):
        if weight + tm * row_bytes <= VMEM_BUDGET:
            return min(tm, s), n
    return min(ROW_TILE, s), min(COL_TILE, n)


def _norm_proj_kernel(x_ref, g_ref, w_ref, o_ref, h_scr, *, act):
    @pl.when(pl.program_id(1) == 0)
    def _():
        h_scr[...] = _rms(x_ref[...], g_ref[...]).astype(BF16)

    acc = _dot(h_scr[...], w_ref[...])
    if act == "sigmoid":
        acc = jax.nn.sigmoid(acc)
    o_ref[...] = acc.astype(o_ref.dtype)


def _norm_proj(x, g, w, *, act=None, out_dtype=BF16):
    s, d = x.shape
    n = w.shape[1]
    tm, tn = _proj_tiles(s, d, n, 2 * d * 4 + d * 2 + n * 4 + 2 * n * jnp.dtype(out_dtype).itemsize)
    return pl.pallas_call(
        functools.partial(_norm_proj_kernel, act=act),
        name="norm_proj",
        grid=(s // tm, n // tn),
        in_specs=[pl.BlockSpec((tm, d), lambda i, j: (i, 0)),
                  pl.BlockSpec((1, d), lambda i, j: (0, 0)),
                  pl.BlockSpec((d, tn), lambda i, j: (0, j))],
        out_specs=pl.BlockSpec((tm, tn), lambda i, j: (i, j)),
        out_shape=jax.ShapeDtypeStruct((s, n), out_dtype),
        scratch_shapes=[pltpu.VMEM((tm, d), BF16)],
        compiler_params=_params("parallel", "arbitrary"),
    )(x, g.reshape(1, d), w)


def _norm_proj_rope_kernel(x_ref, g_ref, w_ref, c_ref, sa_ref, sb_ref, *rest, shift, want_raw):
    if want_raw:
        raw_ref, rot_ref, h_scr = rest
    else:
        rot_ref, h_scr = rest

    @pl.when(pl.program_id(1) == 0)
    def _():
        h_scr[...] = _rms(x_ref[...], g_ref[...]).astype(BF16)

    acc = _dot(h_scr[...], w_ref[...])
    if want_raw:
        raw_ref[...] = acc.astype(BF16)
    c, sa, sb = c_ref[...], sa_ref[...], sb_ref[...]
    for blk in range(acc.shape[1] // LANES):
        a = acc[:, blk * LANES:(blk + 1) * LANES]
        r = a * c + pltpu.roll(a, LANES - shift, 1) * sa + pltpu.roll(a, shift, 1) * sb
        rot_ref[:, blk * LANES:(blk + 1) * LANES] = r.astype(BF16)


def _norm_proj_rope(x, g, w, tables, *, shift, want_raw):
    s, d = x.shape
    n = w.shape[1]
    n_out = 2 if want_raw else 1
    tm, tn = _proj_tiles(s, d, n, 2 * d * 4 + d * 2 + n * 4 + n_out * 2 * n * 2 + 3 * 2 * LANES * 4)
    tab_spec =pl.BlockSpec((tm, LANES), lambda i, j: (i, 0))
    out_spec = pl.BlockSpec((tm, tn), lambda i, j: (i, j))
    out_sd = jax.ShapeDtypeStruct((s, n), BF16)
    return pl.pallas_call(
        functools.partial(_norm_proj_rope_kernel, shift=shift, want_raw=want_raw),
        name="norm_proj_rope",
        grid=(s // tm, n // tn),
        in_specs=[pl.BlockSpec((tm, d), lambda i, j: (i, 0)),
                  pl.BlockSpec((1, d), lambda i, j: (0, 0)),
                  pl.BlockSpec((d, tn), lambda i, j: (0, j)),
                  tab_spec, tab_spec, tab_spec],
        out_specs=[out_spec, out_spec] if want_raw else out_spec,
        out_shape=[out_sd, out_sd] if want_raw else out_sd,
        scratch_shapes=[pltpu.VMEM((tm, d), BF16)],
        compiler_params=_params("parallel", "arbitrary"),
    )(x, g.reshape(1, d), w, *tables)


def _matmul_res_kernel(a_ref, w_ref, r_ref, o_ref):
    o_ref[...] = r_ref[...] + _dot(a_ref[...], w_ref[...])


def _matmul_res(a, w, res):
    s, k = a.shape
    n = w.shape[1]
    tm, tn = _proj_tiles(s, k, n, 2 * k * 2 + 2 * 2 * n * 4 + n * 4)
    return pl.pallas_call(
        _matmul_res_kernel,
        name="matmul_res",
        grid=(s // tm, n // tn),
        in_specs=[pl.BlockSpec((tm, k), lambda i, j: (i, 0)),
                  pl.BlockSpec((k, tn), lambda i, j: (0, j)),
                  pl.BlockSpec((tm, tn), lambda i, j: (i, j))],
        out_specs=pl.BlockSpec((tm, tn), lambda i, j: (i, j)),
        out_shape=jax.ShapeDtypeStruct((s, n), F32),
        compiler_params=_params("parallel", "arbitrary"),
    )(a, w, res)


def _mlp_kernel(x_ref, g_ref, wu_ref, wd_ref, fg_ref, o_ref, h_scr, acc_scr, *, final_norm):
    j = pl.program_id(1)

    @pl.when(j == 0)
    def _():
        h_scr[...] = _rms(x_ref[...], g_ref[...]).astype(BF16)
        acc_scr[...] = jnp.zeros_like(acc_scr)

    u = _dot(h_scr[...], wu_ref[...])
    u = jnp.square(jnp.maximum(u, 0.0)).astype(BF16)
    acc_scr[...] += _dot(u, wd_ref[...])

    @pl.when(j == pl.num_programs(1) - 1)
    def _():
        y = x_ref[...] + acc_scr[...]
        if final_norm:
            y = _rms(y, fg_ref[...])
        o_ref[...] = y


def _mlp(x, g, w_up, w_down, final_g, *, final_norm):
    s, d = x.shape
    ff = w_up.shape[1]
    tm, tf = min(ROW_TILE, s), min(FF_TILE, ff)
    return pl.pallas_call(
        functools.partial(_mlp_kernel, final_norm=final_norm),
        name="sq_relu_mlp",
        grid=(s // tm, ff // tf),
        in_specs=[pl.BlockSpec((tm, d), lambda i, j: (i, 0)),
                  pl.BlockSpec((1, d), lambda i, j: (0, 0)),
                  pl.BlockSpec((d, tf), lambda i, j: (0, j)),
                  pl.BlockSpec((tf, d), lambda i, j: (j, 0)),
                  pl.BlockSpec((1, d), lambda i, j: (0, 0))],
        out_specs=pl.BlockSpec((tm, d), lambda i, j: (i, 0)),
        out_shape=jax.ShapeDtypeStruct((s, d), F32),
        scratch_shapes=[pltpu.VMEM((tm, d), BF16), pltpu.VMEM((tm, d), F32)],
        compiler_params=_params("parallel", "arbitrary"),
    )(x, g.reshape(1, d), w_up, w_down, final_g.reshape(1, d))


def _compress_kernel(t_ref, pos_ref, w1_ref, w2_ref, o_ref):
    t = t_ref[0, 0].astype(F32)
    pos = pos_ref[0]
    half = t.shape[1]
    a = _dot((t + pos[0:1]).astype(BF16), w1_ref[0, :half, :])
    b = _dot((t + pos[1:2]).astype(BF16), w1_ref[0, half:, :])
    nc = t.shape[0]
    hid = jax.nn.gelu(a + pltpu.roll(b, nc - 1, 0))
    o_ref[0, 0] = _dot(hid.astype(BF16), w2_ref[0]).astype(BF16)


def _compress(t2, pos2, w1, w2):
    _, hk, nc, width = t2.shape
    dk = w2.shape[-1]
    return pl.pallas_call(
        _compress_kernel,
        name="nsa_compress",
        grid=(2, hk),
        in_specs=[pl.BlockSpec((1, 1, nc, width), lambda j, h: (j, h, 0, 0)),
                  pl.BlockSpec((1, 2, width), lambda j, h: (j, 0, 0)),
                  pl.BlockSpec((1, 2 * width, CMP_HIDDEN), lambda j, h: (j, 0, 0)),
                  pl.BlockSpec((1, CMP_HIDDEN, dk), lambda j, h: (j, 0, 0))],
        out_specs=pl.BlockSpec((1, 1, nc, dk), lambda j, h: (j, h, 0, 0)),
        out_shape=jax.ShapeDtypeStruct((2, hk, nc, dk), BF16),
        compiler_params=_params("parallel", "arbitrary"),
    )(t2, pos2, w1, w2)


def _nsa_attn_kernel(qr_ref, qo_ref, kc_ref, vc_ref, ks_ref, vs_ref, kw_ref, vw_ref, gt_ref,
                     o_ref, rhs_scr, s_scr, m_scr, l_scr, acc_scr, *, seq):
    g_, tq, dk = NSA_GROUP, NSA_TQ, NSA_HEAD_DIM
    cw = NSA_CHAIN_LANES
    n_chain = g_ * tq // cw
    nc = seq // CMP_STRIDE
    nsel = seq // SEL_BLOCK
    ch = min(NSA_SEL_CHUNK, seq)
    wlen = WINDOW + tq
    qb = pl.program_id(1)
    start = qb * tq
    t_row = start + lax.broadcasted_iota(jnp.int32, (1, tq), 1)

    def heads_on_lanes(ref, lo, hi):
        return jnp.concatenate([ref[g * dk:(g + 1) * dk, :] for g in range(lo, hi)], axis=1)

    def tile_heads(x, n):
        return jnp.concatenate([x] * n, axis=1)

    def softmax_keys(s2):
        m = jnp.max(s2, axis=0, keepdims=True)
        m = jnp.where(m == -jnp.inf, 0.0, m)
        e = jnp.exp2(s2 - m)
        return e, 1.0 / jnp.maximum(jnp.sum(e, axis=0, keepdims=True), 1e-30)

    qr_t = heads_on_lanes(qr_ref, 0, g_)
    qo_t = heads_on_lanes(qo_ref, 0, g_)

    n_end = lax.broadcasted_iota(jnp.int32, (nc, tq), 0) * CMP_STRIDE + (CMP_BLOCK - 1)
    bias_c = jnp.where(n_end <= t_row, 0.0, -jnp.inf)
    e_c, r_c = softmax_keys(_dot(kc_ref[0, 0], qr_t) + tile_heads(bias_c, g_))
    p_c = e_c * r_c
    o_c = _dot(vc_ref[0], p_c.astype(BF16))

    w0 = pl.multiple_of(jnp.maximum(start - WINDOW, 0), tq)
    kpos = w0 + lax.broadcasted_iota(jnp.int32, (wlen, tq), 0)
    bias_w = jnp.where((kpos <= t_row) & (kpos > t_row - WINDOW), 0.0, -jnp.inf)
    e_w, r_w = softmax_keys(_dot(kw_ref[pl.ds(w0, wlen), :], qo_t) + tile_heads(bias_w, g_))
    o_w = _dot(vw_ref[:, pl.ds(w0, wlen)], e_w.astype(BF16)) * r_w

    p_sum = p_c[:, :tq]
    for g in range(1, g_):
        p_sum = p_sum + p_c[:, g * tq:(g + 1) * tq]
    n_i = lax.broadcasted_iota(jnp.int32, (nsel, nc), 1) * CMP_STRIDE
    s_i = lax.broadcasted_iota(jnp.int32, (nsel, nc), 0) * SEL_BLOCK
    overlap = jnp.where((n_i < s_i + SEL_BLOCK) & (n_i + CMP_BLOCK > s_i), 1.0, 0.0).astype(BF16)
    p_hi = p_sum.astype(BF16)
    p_r1 = p_sum - p_hi.astype(F32)
    p_mid = p_r1.astype(BF16)
    p_lo = (p_r1 - p_mid.astype(F32)).astype(BF16)
    imp = _dot(overlap, p_hi) + _dot(overlap, p_mid) + _dot(overlap, p_lo)

    ids = lax.broadcasted_iota(jnp.int32, (nsel, tq), 0)
    cur = t_row >> SEL_SHIFT
    forced = (ids == 0) | (ids == cur) | (ids == cur - 1)
    imp = jnp.where(forced, FORCE_SCORE, imp)
    work = jnp.where(ids * SEL_BLOCK <= t_row, imp, -1.0)

    sel = jnp.zeros((nsel, tq), F32)
    for _ in range(min(SEL_TOPK, nsel)):
        mx = jnp.max(work, axis=0, keepdims=True)
        first = jnp.min(jnp.where(work == mx, ids, nsel), axis=0, keepdims=True)
        pick = ids == first
        sel = jnp.where(pick, 1.0, sel)
        work = jnp.where(pick, -3.0e38, work)

    neg_sel = tile_heads(jnp.where(sel > 0.5, 0.0, NEG_BIG).astype(BF16), cw // tq)
    for j in range(n_chain):
        rhs_scr[j] = jnp.concatenate([qo_t[:, j * cw:(j + 1) * cw], neg_sel], axis=0)
    m_scr[...] = jnp.full(m_scr.shape, NEG_BIG, F32)
    l_scr[...] = jnp.zeros(l_scr.shape, F32)
    acc_scr[...] = jnp.zeros(acc_scr.shape, F32)

    def fold(off, buf, diagonal):
        off = pl.multiple_of(off, ch)
        v_t = vs_ref[:, pl.ds(off, ch)]
        state = [(m_scr[j], l_scr[j], acc_scr[j]) for j in range(n_chain)]
        if diagonal:
            key = off + lax.broadcasted_iota(jnp.int32, (ch, tq), 0)
            causal = tile_heads(jnp.where(key <= t_row, 0.0, NEG_BIG), cw // tq)
            todo = []
        else:
            k_next = ks_ref[pl.ds(pl.multiple_of(off + ch, ch), ch), :]
            todo = list(range(n_chain))
        ahead = 2
        for j in todo[:ahead]:
            s_scr[1 - buf, j] = _dot(k_next, rhs_scr[j])
        new_state = []
        for j in range(n_chain):
            s = s_scr[buf, j]
            if diagonal:
                s = s + causal
            m_old, l_old, acc_old = state[j]
            m_new = jnp.maximum(m_old, jnp.max(s, axis=0, keepdims=True))
            alpha = jnp.exp2(m_old - m_new)
            e = jnp.exp2(s - m_new)
            l_new = alpha * l_old + jnp.sum(e, axis=0, keepdims=True)
            acc_new = alpha * acc_old + _dot(v_t, e.astype(BF16))
            new_state.append((m_new, l_new, acc_new))
            for jn in todo[j + ahead:j + ahead + 1]:
                s_scr[1 - buf, jn] = _dot(k_next, rhs_scr[jn])
        for j in range(n_chain):
            m_scr[j], l_scr[j], acc_scr[j] = new_state[j]

    k_first = ks_ref[:ch, :]
    for j in range(n_chain):
        s_scr[0, j] = _dot(k_first, rhs_scr[j])

    def pair_step(c, carry):
        off = pl.multiple_of(c * 2 * ch, 2 * ch)
        fold(off, 0, False)
        fold(off + ch, 1, False)
        return carry

    n_full = (start + tq - 1) // ch
    lax.fori_loop(0, n_full // 2, pair_step, 0)
    rest = pl.multiple_of((n_full // 2) * 2 * ch, 2 * ch)

    @pl.when(n_full % 2 == 1)
    def _():
        fold(rest, 0, False)
        fold(rest + ch, 1, True)

    @pl.when(n_full % 2 == 0)
    def _():
        fold(rest, 0, True)

    gt = gt_ref[0]
    for g in range(g_):
        j = g * tq // cw
        lanes = slice(g * tq - j * cw, g * tq - j * cw + tq)
        o_s = jnp.where(m_scr[j][:, lanes] > SCORE_FLOOR,
                        acc_scr[j][:, lanes] * (1.0 / jnp.maximum(l_scr[j][:, lanes], 1e-30)), 0.0)
        cols = slice(g * tq, (g + 1) * tq)
        o_t = (gt[3 * g:3 * g + 1] * o_c[:, cols] + gt[3 * g + 1:3 * g + 2] * o_s
               + gt[3 * g + 2:3 * g + 3] * o_w[:, cols])
        o_ref[:, g * dk:(g + 1) * dk] = o_t.T.astype(BF16)


def _nsa_attention(raw, rot, plain, cmp_kv, gates):
    s = raw.shape[0]
    hk, g_, dk, tq = NSA_KV_HEADS, NSA_GROUP, NSA_HEAD_DIM, NSA_TQ
    nc = s // CMP_STRIDE
    qw = NSA_HEADS * dk
    qr_t, qo_t = raw[:, :qw].T, rot[:, :qw].T
    v_t = plain[:, 2 * hk * dk:].T
    vc_t = cmp_kv[1].transpose(0, 2, 1)
    gates_t = gates[:, :3 * NSA_HEADS].T.reshape(hk, 3 * g_, s)
    nsel = s // SEL_BLOCK
    one_hot = (jnp.arange(s)[:, None] >> SEL_SHIFT == jnp.arange(nsel)[None, :]).astype(BF16)
    k_sel = rot[:, qw:qw + hk * dk].reshape(s, hk, dk)
    k_aug = jnp.concatenate([k_sel, jnp.broadcast_to(one_hot[:, None, :], (s, hk, nsel))], axis=-1)
    k_aug = k_aug.reshape(s, hk * (dk + nsel))
    qspec = pl.BlockSpec((g_ * dk, tq), lambda h, i: (h, i))
    k_col = lambda base: pl.BlockSpec((s, dk), lambda h, i: (0, base + h))
    v_row = lambda base: pl.BlockSpec((dk, s), lambda h, i: (base + h, 0))
    cw = NSA_CHAIN_LANES
    assert cw % tq == 0 and (g_ * tq) % cw == 0 and NSA_SEL_CHUNK % tq == 0
    n_chain = g_ * tq // cw
    return pl.pallas_call(
        functools.partial(_nsa_attn_kernel, seq=s),
        name="nsa_attention",
        grid=(hk, s // tq),
        in_specs=[qspec, qspec,
                  pl.BlockSpec((1, 1, nc, dk), lambda h, i: (0, h, 0, 0)),
                  pl.BlockSpec((1, dk, nc), lambda h, i: (h, 0, 0)),
                  pl.BlockSpec((s, dk + nsel), lambda h, i: (0, h)), v_row(0),
                  k_col(NSA_HEADS + hk), v_row(hk),
                  pl.BlockSpec((1, 3 * g_, tq), lambda h, i: (h, 0, i))],
        out_specs=pl.BlockSpec((tq, g_ * dk), lambda h, i: (i, h)),
        out_shape=jax.ShapeDtypeStruct((s, NSA_HEADS * dk), BF16),
        scratch_shapes=[pltpu.VMEM((n_chain, dk + nsel, cw), BF16),
                        pltpu.VMEM((2, n_chain, min(NSA_SEL_CHUNK, s), cw), F32),
                        pltpu.VMEM((n_chain, 1, cw), F32),
                        pltpu.VMEM((n_chain, 1, cw), F32),
                        pltpu.VMEM((n_chain, dk, cw), F32)],
        compiler_params=_params("parallel", "arbitrary"),
    )(qr_t, qo_t, cmp_kv, vc_t, k_aug, v_t, rot, v_t, gates_t)


def _diff_attn_kernel(q_ref, k_ref, v_ref, lam_ref, sg_ref, o_ref, qq_scr, s_scr, m_scr, l_scr,
                      acc_scr, *, lambda_init):
    g_, tq, d = DIFF_GROUP, DIFF_TQ, DIFF_HEAD_DIM
    n_chain = 2 * g_
    qb = pl.program_id(1)

    first_half = lax.broadcasted_iota(jnp.int32, (2 * d, 1), 0) < d
    for g in range(g_):
        q_t = q_ref[g * 2 * d:(g + 1) * 2 * d, :]
        zero = jnp.zeros_like(q_t)
        qq_scr[2 * g] = jnp.where(first_half, q_t, zero)
        qq_scr[2 * g + 1] = jnp.where(first_half, zero, q_t)
    m_scr[...] = jnp.full(m_scr.shape, NEG_BIG, F32)
    l_scr[...] = jnp.zeros(l_scr.shape, F32)
    acc_scr[...] = jnp.zeros(acc_scr.shape, F32)

    ch = 2 * tq
    n_full = qb // 2

    def fold(off, buf, diagonal):
        off = pl.multiple_of(off, ch)
        v_t = v_ref[:, pl.ds(off, ch)]
        state = [(m_scr[j], l_scr[j], acc_scr[j]) for j in range(n_chain)]
        if diagonal:
            key = lax.broadcasted_iota(jnp.int32, (ch, tq), 0) + (off - qb * tq)
            causal = key <= lax.broadcasted_iota(jnp.int32, (ch, tq), 1)
            todo = []
        else:
            k_next = k_ref[pl.ds(pl.multiple_of(off + ch, ch), ch), :]
            todo = list(range(n_chain))
        ahead = 2
        for j in todo[:ahead]:
            s_scr[1 - buf, j] = _dot(k_next, qq_scr[j])
        new_state = []
        for j in range(n_chain):
            s = s_scr[buf, j]
            if diagonal:
                s = jnp.where(causal, s, NEG_BIG)
            m_old, l_old, acc_old = state[j]
            m_new = jnp.maximum(m_old, jnp.max(s, axis=0, keepdims=True))
            alpha = jnp.exp2(m_old - m_new)
            p = jnp.exp2(s - m_new)
            l_new = alpha * l_old + jnp.sum(p, axis=0, keepdims=True)
            acc_new = alpha * acc_old + _dot(v_t, p.astype(BF16))
            new_state.append((m_new, l_new, acc_new))
            for jn in todo[j + ahead:j + ahead + 1]:
                s_scr[1 - buf, jn] = _dot(k_next, qq_scr[jn])
        for j in range(n_chain):
            m_scr[j], l_scr[j], acc_scr[j] = new_state[j]

    k_first = k_ref[:ch, :]
    for j in range(n_chain):
        s_scr[0, j] = _dot(k_first, qq_scr[j])

    def pair_step(c, carry):
        off = pl.multiple_of(c * 2 * ch, 2 * ch)
        fold(off, 0, False)
        fold(off + ch, 1, False)
        return carry

    lax.fori_loop(0, n_full // 2, pair_step, 0)
    rest = pl.multiple_of((n_full // 2) * 2 * ch, 2 * ch)

    @pl.when(n_full % 2 == 1)
    def _():
        fold(rest, 0, False)
        fold(rest + ch, 1, True)

    @pl.when(n_full % 2 == 0)
    def _():
        fold(rest, 0, True)

    lv = lam_ref[...]
    lam = (jnp.exp(jnp.sum(lv[0:1] * lv[1:2], axis=-1, keepdims=True))
           - jnp.exp(jnp.sum(lv[2:3] * lv[3:4], axis=-1, keepdims=True)) + lambda_init)
    sub_g = sg_ref[...]
    for g in range(g_):
        o1 = acc_scr[2 * g] / jnp.maximum(l_scr[2 * g], 1e-30)
        o2 = acc_scr[2 * g + 1] / jnp.maximum(l_scr[2 * g + 1], 1e-30)
        a = o1 - lam * o2
        ms = jnp.mean(a * a, axis=0, keepdims=True)
        a = a * lax.rsqrt(ms + NORM_EPS) * sub_g * (1.0 - lambda_init)
        o_ref[:, g * 2 * d:(g + 1) * 2 * d] = a.T.astype(BF16)


def _diff_attention(q_rot, k_rot, v, lam_vecs, subln_g, lambda_init):
    s = q_rot.shape[0]
    hk, g_, d, tq = DIFF_KV_HEADS, DIFF_GROUP, DIFF_HEAD_DIM, DIFF_TQ
    return pl.pallas_call(
        functools.partial(_diff_attn_kernel, lambda_init=lambda_init),
        name="diff_attention",
        grid=(hk, s // tq),
        in_specs=[pl.BlockSpec((g_ * 2 * d, tq), lambda h, i: (h, i)),
                  pl.BlockSpec((s, 2 * d), lambda h, i: (0, h)),
                  pl.BlockSpec((2 * d, s), lambda h, i: (h, 0)),
                  pl.BlockSpec((4, d), lambda h, i: (0, 0)),
                  pl.BlockSpec((2 * d, 1), lambda h, i: (0, 0))],
        out_specs=pl.BlockSpec((tq, g_ * 2 * d), lambda h, i: (i, h)),
        out_shape=jax.ShapeDtypeStruct((s, DIFF_HEADS * 2 * d), BF16),
        scratch_shapes=[pltpu.VMEM((2 * g_, 2 * d, tq), BF16),
                        pltpu.VMEM((2, 2 * g_, 2 * tq, tq), F32),
                        pltpu.VMEM((2 * g_, 1, tq), F32),
                        pltpu.VMEM((2 * g_, 1, tq), F32),
                        pltpu.VMEM((2 * g_, 2 * d, tq), F32)],
        compiler_params=_params("parallel", "arbitrary"),
    )(q_rot.T, k_rot, v.T, lam_vecs, subln_g.reshape(2 * d, 1))


def _rope_tables(seq, head_dim, reps):
    rot = head_dim // ROPE_FRACTION
    half = rot // 2
    inv = 1.0 / (ROPE_THETA ** (jnp.arange(0, rot, 2, dtype=F32) / rot))
    ang = jnp.arange(seq, dtype=F32)[:, None] * inv[None, :]
    cos, sin = jnp.cos(ang), jnp.sin(ang)
    rest = head_dim - rot
    zeros_h = jnp.zeros((seq, half), F32)
    c = jnp.concatenate([cos, cos, jnp.ones((seq, rest), F32)], axis=-1)
    sa = jnp.concatenate([-sin, zeros_h, jnp.zeros((seq, rest), F32)], axis=-1)
    sb = jnp.concatenate([zeros_h, sin, jnp.zeros((seq, rest), F32)], axis=-1)
    return tuple(jnp.tile(t, (1, reps)) for t in (c, sa, sb)), half


def _nsa_layer(x, norm_g, w_in, cmp_pos, cmp_w1, cmp_w2, w_out, tables, shift):
    s = x.shape[0]
    hk, g_, dk = NSA_KV_HEADS, NSA_GROUP, NSA_HEAD_DIM
    qw, kvw = NSA_HEADS * dk, hk * dk
    part = lambda i: w_in[:, qw + i * kvw:qw + (i + 1) * kvw]
    w_q = w_in[:, :qw] * (dk ** -0.5 * LOG2E)
    w_rope = jnp.concatenate([w_q, part(2), part(4)], axis=1).astype(BF16)
    w_plain = jnp.concatenate([part(0), part(1), part(3), part(5)], axis=1).astype(BF16)
    n_gate = 3 * NSA_HEADS
    w_gate = jnp.pad(w_in[:, qw + 6 * kvw:], ((0, 0), (0, LANES - n_gate))).astype(BF16)

    raw, rot = _norm_proj_rope(x, norm_g, w_rope, tables, shift=shift, want_raw=True)
    plain = _norm_proj(x, norm_g, w_plain)
    gates = _norm_proj(x, norm_g, w_gate, act="sigmoid", out_dtype=F32)

    nc = s // CMP_STRIDE
    t2 = plain[:, :2 * kvw].reshape(nc, CMP_STRIDE, 2, hk, dk).transpose(2, 3, 0, 1, 4)
    t2 = t2.reshape(2, hk, nc, CMP_STRIDE * dk)
    cmp_kv = _compress(t2, cmp_pos.reshape(2, 2, CMP_STRIDE * dk), cmp_w1.astype(BF16),
                       cmp_w2.astype(BF16))
    o = _nsa_attention(raw, rot, plain, cmp_kv, gates)
    return _matmul_res(o, w_out.astype(BF16), x)


def kernel(x, attn_norm_g, mlp_norm_g, final_norm_g, nsa_w_in, nsa_cmp_pos, nsa_cmp_w1, nsa_cmp_w2, nsa_w_out, kv_norm_g, kv_w_shared, diff_w_q, diff_lambda, diff_subln_g, diff_w_out, mlp_w_up, mlp_w_down):
    b, s, d = x.shape
    tables_a, shift_a = _rope_tables(s, NSA_HEAD_DIM, 1)
    tables_b, shift_b = _rope_tables(s, DIFF_HEAD_DIM, 2)
    kcols = DIFF_KV_HEADS * 2 * DIFF_HEAD_DIM
    outs = []
    for bi in range(b):
        xs = x[bi]
        k_sh = v_sh = None
        for layer in range(DEPTH):
            if layer < N_A_LAYERS:
                xs = _nsa_layer(xs, attn_norm_g[layer], nsa_w_in[layer], nsa_cmp_pos[layer],
                                nsa_cmp_w1[layer], nsa_cmp_w2[layer], nsa_w_out[layer],
                                tables_a, shift_a)
            else:
                j = layer - N_A_LAYERS
                if j == 0:
                    k_sh = _norm_proj_rope(xs, kv_norm_g, kv_w_shared[:, :kcols].astype(BF16),
                                           tables_b, shift=shift_b, want_raw=False)
                    v_sh = _norm_proj(xs, kv_norm_g, kv_w_shared[:, kcols:].astype(BF16))
                lambda_init = 0.8 - 0.6 * math.exp(-0.3 * layer)
                w_q = (diff_w_q[j] * (DIFF_HEAD_DIM ** -0.5 * LOG2E)).astype(BF16)
                q = _norm_proj_rope(xs, attn_norm_g[layer], w_q, tables_b,
                                    shift=shift_b, want_raw=False)
                o = _diff_attention(q, k_sh, v_sh, diff_lambda[j], diff_subln_g[j], lambda_init)
                xs = _matmul_res(o, diff_w_out[j].astype(BF16), xs)
            xs = _mlp(xs, mlp_norm_g[layer], mlp_w_up[layer].astype(BF16),
                      mlp_w_down[layer].astype(BF16), final_norm_g,
                      final_norm=(layer == DEPTH - 1))
        outs.append(xs)
    return jnp.stack(outs, axis=0)
```

```python
import functools
import math

import jax
import jax.numpy as jnp
from jax import lax
from jax.experimental import pallas as pl
from jax.experimental.pallas import tpu as pltpu

F32 = jnp.float32
BF16 = jnp.bfloat16

D_MODEL = 2048
DEPTH = 4
N_A_LAYERS = DEPTH // 2

NSA_HEADS = 16
NSA_KV_HEADS = 4
NSA_GROUP = NSA_HEADS // NSA_KV_HEADS
NSA_HEAD_DIM = D_MODEL // NSA_HEADS
CMP_BLOCK = 32
CMP_STRIDE = 16
CMP_HIDDEN = 4 * NSA_HEAD_DIM
SEL_BLOCK = 64
SEL_SHIFT = SEL_BLOCK.bit_length() - 1
SEL_TOPK = 16
WINDOW = 512
FORCE_SCORE = 1.0e4

DIFF_HEADS = 16
DIFF_KV_HEADS = 4
DIFF_GROUP = DIFF_HEADS // DIFF_KV_HEADS
DIFF_HEAD_DIM = D_MODEL // (2 * DIFF_HEADS)

D_FF = 4 * D_MODEL
ROPE_THETA = 500000.0
ROPE_FRACTION = 4
NORM_EPS = 1e-6

LANES = 128
VMEM_LIMIT = 48 * 1024 * 1024
VMEM_BUDGET = 40 * 1024 * 1024
NEG_BIG = -1.0e30
SCORE_FLOOR = -1.0e28
LOG2E = 1.4426950408889634

NSA_TQ = 256
NSA_SEL_CHUNK = 512
NSA_CHAIN_LANES = 256
DIFF_TQ = 256
ROW_TILE = 512
COL_TILE = 512
FF_TILE = 1024


def _params(*sem):
    return pltpu.CompilerParams(dimension_semantics=sem, vmem_limit_bytes=VMEM_LIMIT)


def _dot(a, b):
    return jnp.dot(a, b, preferred_element_type=F32)


def _dot_nt(a, b):
    return lax.dot_general(a, b, (((1,), (1,)), ((), ())), preferred_element_type=F32)


def _rms(x, g):
    ms = jnp.mean(x * x, axis=-1, keepdims=True)
    return x * lax.rsqrt(ms + NORM_EPS) * g


def _proj_tiles(s, k, n, row_bytes):
    weight = 2 * k * n * 2
    for tm in (ROW_TILE, ROW_TILE // 2):
        if weight + tm * row_bytes <= VMEM_BUDGET:
            return min(tm, s), n
    return min(ROW_TILE, s), min(COL_TILE, n)


def _norm_proj_kernel(x_ref, g_ref, w_ref, o_ref, h_scr, *, act):
    @pl.when(pl.program_id(1) == 0)
    def _():
        h_scr[...] = _rms(x_ref[...], g_ref[...]).astype(BF16)

    acc = _dot(h_scr[...], w_ref[...])
    if act == "sigmoid":
        acc = jax.nn.sigmoid(acc)
    o_ref[...] = acc.astype(o_ref.dtype)


def _norm_proj(x, g, w, *, act=None, out_dtype=BF16):
    s, d = x.shape
    n = w.shape[1]
    tm, tn = _proj_tiles(s, d, n, 2 * d * 4 + d * 2 + n * 4 + 2 * n * jnp.dtype(out_dtype).itemsize)
    return pl.pallas_call(
        functools.partial(_norm_proj_kernel, act=act),
        name="norm_proj",
        grid=(s // tm, n // tn),
        in_specs=[pl.BlockSpec((tm, d), lambda i, j: (i, 0)),
                  pl.BlockSpec((1, d), lambda i, j: (0, 0)),
                  pl.BlockSpec((d, tn), lambda i, j: (0, j))],
        out_specs=pl.BlockSpec((tm, tn), lambda i, j: (i, j)),
        out_shape=jax.ShapeDtypeStruct((s, n), out_dtype),
        scratch_shapes=[pltpu.VMEM((tm, d), BF16)],
        compiler_params=_params("parallel", "arbitrary"),
    )(x, g.reshape(1, d), w)


def _norm_proj_rope_kernel(x_ref, g_ref, w_ref, c_ref, sa_ref, sb_ref, *rest, shift, want_raw):
    if want_raw:
        raw_ref, rot_ref, h_scr = rest
    else:
        rot_ref, h_scr = rest

    @pl.when(pl.program_id(1) == 0)
    def _():
        h_scr[...] = _rms(x_ref[...], g_ref[...]).astype(BF16)

    acc = _dot(h_scr[...], w_ref[...])
    if want_raw:
        raw_ref[...] = acc.astype(BF16)
    c, sa, sb = c_ref[...], sa_ref[...], sb_ref[...]
    for blk in range(acc.shape[1] // LANES):
        a = acc[:, blk * LANES:(blk + 1) * LANES]
        r = a * c + pltpu.roll(a, LANES - shift, 1) * sa + pltpu.roll(a, shift, 1) * sb
        rot_ref[:, blk * LANES:(blk + 1) * LANES] = r.astype(BF16)


def _norm_proj_rope(x, g, w, tables, *, shift, want_raw):
    s, d = x.shape
    n = w.shape[1]
    n_out = 2 if want_raw else 1
    tm, tn = _proj_tiles(s, d, n, 2 * d * 4 + d * 2 + n * 4 + n_out * 2 * n * 2 + 3 * 2 * LANES * 4)
    tab_spec =pl.BlockSpec((tm, LANES), lambda i, j: (i, 0))
    out_spec = pl.BlockSpec((tm, tn), lambda i, j: (i, j))
    out_sd = jax.ShapeDtypeStruct((s, n), BF16)
    return pl.pallas_call(
        functools.partial(_norm_proj_rope_kernel, shift=shift, want_raw=want_raw),
        name="norm_proj_rope",
        grid=(s // tm, n // tn),
        in_specs=[pl.BlockSpec((tm, d), lambda i, j: (i, 0)),
                  pl.BlockSpec((1, d), lambda i, j: (0, 0)),
                  pl.BlockSpec((d, tn), lambda i, j: (0, j)),
                  tab_spec, tab_spec, tab_spec],
        out_specs=[out_spec, out_spec] if want_raw else out_spec,
        out_shape=[out_sd, out_sd] if want_raw else out_sd,
        scratch_shapes=[pltpu.VMEM((tm, d), BF16)],
        compiler_params=_params("parallel", "arbitrary"),
    )(x, g.reshape(1, d), w, *tables)


def _matmul_res_kernel(a_ref, w_ref, r_ref, o_ref):
    o_ref[...] = r_ref[...] + _dot(a_ref[...], w_ref[...])


def _matmul_res(a, w, res):
    s, k = a.shape
    n = w.shape[1]
    tm, tn = _proj_tiles(s, k, n, 2 * k * 2 + 2 * 2 * n * 4 + n * 4)
    return pl.pallas_call(
        _matmul_res_kernel,
        name="matmul_res",
        grid=(s // tm, n // tn),
        in_specs=[pl.BlockSpec((tm, k), lambda i, j: (i, 0)),
                  pl.BlockSpec((k, tn), lambda i, j: (0, j)),
                  pl.BlockSpec((tm, tn), lambda i, j: (i, j))],
        out_specs=pl.BlockSpec((tm, tn), lambda i, j: (i, j)),
        out_shape=jax.ShapeDtypeStruct((s, n), F32),
        compiler_params=_params("parallel", "arbitrary"),
    )(a, w, res)


def _mlp_kernel(x_ref, g_ref, wu_ref, wd_ref, fg_ref, o_ref, h_scr, acc_scr, *, final_norm):
    j = pl.program_id(1)

    @pl.when(j == 0)
    def _():
        h_scr[...] = _rms(x_ref[...], g_ref[...]).astype(BF16)
        acc_scr[...] = jnp.zeros_like(acc_scr)

    u = _dot(h_scr[...], wu_ref[...])
    u = jnp.square(jnp.maximum(u, 0.0)).astype(BF16)
    acc_scr[...] += _dot(u, wd_ref[...])

    @pl.when(j == pl.num_programs(1) - 1)
    def _():
        y = x_ref[...] + acc_scr[...]
        if final_norm:
            y = _rms(y, fg_ref[...])
        o_ref[...] = y


def _mlp(x, g, w_up, w_down, final_g, *, final_norm):
    s, d = x.shape
    ff = w_up.shape[1]
    tm, tf = min(ROW_TILE, s), min(FF_TILE, ff)
    return pl.pallas_call(
        functools.partial(_mlp_kernel, final_norm=final_norm),
        name="sq_relu_mlp",
        grid=(s // tm, ff // tf),
        in_specs=[pl.BlockSpec((tm, d), lambda i, j: (i, 0)),
                  pl.BlockSpec((1, d), lambda i, j: (0, 0)),
                  pl.BlockSpec((d, tf), lambda i, j: (0, j)),
                  pl.BlockSpec((tf, d), lambda i, j: (j, 0)),
                  pl.BlockSpec((1, d), lambda i, j: (0, 0))],
        out_specs=pl.BlockSpec((tm, d), lambda i, j: (i, 0)),
        out_shape=jax.ShapeDtypeStruct((s, d), F32),
        scratch_shapes=[pltpu.VMEM((tm, d), BF16), pltpu.VMEM((tm, d), F32)],
        compiler_params=_params("parallel", "arbitrary"),
    )(x, g.reshape(1, d), w_up, w_down, final_g.reshape(1, d))


def _compress_kernel(t_ref, pos_ref, w1_ref, w2_ref, o_ref):
    t = t_ref[0, 0].astype(F32)
    pos = pos_ref[0]
    half = t.shape[1]
    a = _dot((t + pos[0:1]).astype(BF16), w1_ref[0, :half, :])
    b = _dot((t + pos[1:2]).astype(BF16), w1_ref[0, half:, :])
    nc = t.shape[0]
    hid = jax.nn.gelu(a + pltpu.roll(b, nc - 1, 0))
    o_ref[0, 0] = _dot(hid.astype(BF16), w2_ref[0]).astype(BF16)


def _compress(t2, pos2, w1, w2):
    _, hk, nc, width = t2.shape
    dk = w2.shape[-1]
    return pl.pallas_call(
        _compress_kernel,
        name="nsa_compress",
        grid=(2, hk),
        in_specs=[pl.BlockSpec((1, 1, nc, width), lambda j, h: (j, h, 0, 0)),
                  pl.BlockSpec((1, 2, width), lambda j, h: (j, 0, 0)),
                  pl.BlockSpec((1, 2 * width, CMP_HIDDEN), lambda j, h: (j, 0, 0)),
                  pl.BlockSpec((1, CMP_HIDDEN, dk), lambda j, h: (j, 0, 0))],
        out_specs=pl.BlockSpec((1, 1, nc, dk), lambda j, h: (j, h, 0, 0)),
        out_shape=jax.ShapeDtypeStruct((2, hk, nc, dk), BF16),
        compiler_params=_params("parallel", "arbitrary"),
    )(t2, pos2, w1, w2)


def _nsa_attn_kernel(qr_ref, qo_ref, kc_ref, vc_ref, ks_ref, oh_ref, vs_ref, kw_ref, vw_ref, gt_ref,
                     o_ref, vst_scr, vwt_scr, rhs_scr, s_scr, m_scr, l_scr, acc_scr, *, seq):
    g_, tq, dk = NSA_GROUP, NSA_TQ, NSA_HEAD_DIM
    cw = NSA_CHAIN_LANES
    n_chain = g_ * tq // cw
    nc = seq // CMP_STRIDE
    nsel = seq // SEL_BLOCK
    ch = min(NSA_SEL_CHUNK, seq)
    wlen = WINDOW + tq
    qb = pl.program_id(1)
    start = qb * tq
    t_row = start + lax.broadcasted_iota(jnp.int32, (1, tq), 1)

    @pl.when(qb == 0)
    def _():
        for r in range(0, seq, ch):
            vst_scr[:, r:r + ch] = vs_ref[r:r + ch, :].T
            vwt_scr[:, r:r + ch] = vw_ref[r:r + ch, :].T

    def heads_on_lanes(ref):
        return jnp.concatenate([ref[:, g * dk:(g + 1) * dk].T for g in range(g_)], axis=1)

    def tile_heads(x, n):
        return jnp.concatenate([x] * n, axis=1)

    def softmax_keys(s2):
        m = jnp.max(s2, axis=0, keepdims=True)
        m = jnp.where(m == -jnp.inf, 0.0, m)
        e = jnp.exp2(s2 - m)
        return e, 1.0 / jnp.maximum(jnp.sum(e, axis=0, keepdims=True), 1e-30)

    qr_t = heads_on_lanes(qr_ref)
    qo_t = heads_on_lanes(qo_ref)

    n_end = lax.broadcasted_iota(jnp.int32, (nc, tq), 0) * CMP_STRIDE + (CMP_BLOCK - 1)
    bias_c = jnp.where(n_end <= t_row, 0.0, -jnp.inf)
    e_c, r_c = softmax_keys(_dot(kc_ref[0, 0], qr_t) + tile_heads(bias_c, g_))
    p_c = e_c * r_c
    o_c = _dot(vc_ref[0, 0].T, p_c.astype(BF16))

    w0 = pl.multiple_of(jnp.maximum(start - WINDOW, 0), tq)
    kpos = w0 + lax.broadcasted_iota(jnp.int32, (wlen, tq), 0)
    bias_w = jnp.where((kpos <= t_row) & (kpos > t_row - WINDOW), 0.0, -jnp.inf)
    e_w, r_w = softmax_keys(_dot(kw_ref[pl.ds(w0, wlen), :], qo_t) + tile_heads(bias_w, g_))
    o_w = _dot(vwt_scr[:, pl.ds(w0, wlen)], e_w.astype(BF16)) * r_w

    p_sum = p_c[:, :tq]
    for g in range(1, g_):
        p_sum = p_sum + p_c[:, g * tq:(g + 1) * tq]
    n_i = lax.broadcasted_iota(jnp.int32, (nsel, nc), 1) * CMP_STRIDE
    s_i = lax.broadcasted_iota(jnp.int32, (nsel, nc), 0) * SEL_BLOCK
    overlap = jnp.where((n_i < s_i + SEL_BLOCK) & (n_i + CMP_BLOCK > s_i), 1.0, 0.0).astype(BF16)
    p_hi = p_sum.astype(BF16)
    p_r1 = p_sum - p_hi.astype(F32)
    p_mid = p_r1.astype(BF16)
    p_lo = (p_r1 - p_mid.astype(F32)).astype(BF16)
    imp = _dot(overlap, p_hi) + _dot(overlap, p_mid) + _dot(overlap, p_lo)

    ids = lax.broadcasted_iota(jnp.int32, (nsel, tq), 0)
    cur = t_row >> SEL_SHIFT
    forced = (ids == 0) | (ids == cur) | (ids == cur - 1)
    imp = jnp.where(forced, FORCE_SCORE, imp)
    work = jnp.where(ids * SEL_BLOCK <= t_row, imp, -1.0)

    sel = jnp.zeros((nsel, tq), F32)
    for _ in range(min(SEL_TOPK, nsel)):
        mx = jnp.max(work, axis=0, keepdims=True)
        first = jnp.min(jnp.where(work == mx, ids, nsel), axis=0, keepdims=True)
        pick = ids == first
        sel = jnp.where(pick, 1.0, sel)
        work = jnp.where(pick, -3.0e38, work)

    neg_sel = tile_heads(jnp.where(sel > 0.5, 0.0, NEG_BIG).astype(BF16), cw // tq)
    for j in range(n_chain):
        rhs_scr[j] = jnp.concatenate([qo_t[:, j * cw:(j + 1) * cw], neg_sel], axis=0)
    m_scr[...] = jnp.full(m_scr.shape, NEG_BIG, F32)
    l_scr[...] = jnp.zeros(l_scr.shape, F32)
    acc_scr[...] = jnp.zeros(acc_scr.shape, F32)

    def keys_with_block(off):
        return jnp.concatenate([ks_ref[pl.ds(off, ch), :], oh_ref[pl.ds(off, ch), :]], axis=1)

    def fold(off, buf, diagonal):
        off = pl.multiple_of(off, ch)
        v_t = vst_scr[:, pl.ds(off, ch)]
        state = [(m_scr[j], l_scr[j], acc_scr[j]) for j in range(n_chain)]
        if diagonal:
            key = off + lax.broadcasted_iota(jnp.int32, (ch, tq), 0)
            causal = tile_heads(jnp.where(key <= t_row, 0.0, NEG_BIG), cw // tq)
            todo = []
        else:
            k_next = keys_with_block(pl.multiple_of(off + ch, ch))
            todo = list(range(n_chain))
        ahead = 2
        for j in todo[:ahead]:
            s_scr[1 - buf, j] = _dot(k_next, rhs_scr[j])
        new_state = []
        for j in range(n_chain):
            s = s_scr[buf, j]
            if diagonal:
                s = s + causal
            m_old, l_old, acc_old = state[j]
            m_new = jnp.maximum(m_old, jnp.max(s, axis=0, keepdims=True))
            alpha = jnp.exp2(m_old - m_new)
            e = jnp.exp2(s - m_new)
            l_new = alpha * l_old + jnp.sum(e, axis=0, keepdims=True)
            acc_new = alpha * acc_old + _dot(v_t, e.astype(BF16))
            new_state.append((m_new, l_new, acc_new))
            for jn in todo[j + ahead:j + ahead + 1]:
                s_scr[1 - buf, jn] = _dot(k_next, rhs_scr[jn])
        for j in range(n_chain):
            m_scr[j], l_scr[j], acc_scr[j] = new_state[j]

    k_first = keys_with_block(0)
    for j in range(n_chain):
        s_scr[0, j] = _dot(k_first, rhs_scr[j])

    def pair_step(c, carry):
        off = pl.multiple_of(c * 2 * ch, 2 * ch)
        fold(off, 0, False)
        fold(off + ch, 1, False)
        return carry

    n_full = (start + tq - 1) // ch
    lax.fori_loop(0, n_full // 2, pair_step, 0)
    rest = pl.multiple_of((n_full // 2) * 2 * ch, 2 * ch)

    @pl.when(n_full % 2 == 1)
    def _():
        fold(rest, 0, False)
        fold(rest + ch, 1, True)

    @pl.when(n_full % 2 == 0)
    def _():
        fold(rest, 0, True)

    gt = gt_ref[0]
    for g in range(g_):
        j = g * tq // cw
        lanes = slice(g * tq - j * cw, g * tq - j * cw + tq)
        o_s = jnp.where(m_scr[j][:, lanes] > SCORE_FLOOR,
                        acc_scr[j][:, lanes] * (1.0 / jnp.maximum(l_scr[j][:, lanes], 1e-30)), 0.0)
        cols = slice(g * tq, (g + 1) * tq)
        o_t = (gt[3 * g:3 * g + 1] * o_c[:, cols] + gt[3 * g + 1:3 * g + 2] * o_s
               + gt[3 * g + 2:3 * g + 3] * o_w[:, cols])
        o_ref[:, g * dk:(g + 1) * dk] = o_t.T.astype(BF16)


def _nsa_attention(raw, rot, plain, cmp_kv, gates):
    s = raw.shape[0]
    hk, g_, dk, tq = NSA_KV_HEADS, NSA_GROUP, NSA_HEAD_DIM, NSA_TQ
    nc = s // CMP_STRIDE
    nsel = s // SEL_BLOCK
    one_hot = (jnp.arange(s)[:, None] >> SEL_SHIFT == jnp.arange(nsel)[None, :]).astype(BF16)
    gates_t = gates[:, :3 * NSA_HEADS].T.reshape(hk, 3 * g_, s)
    qspec = pl.BlockSpec((tq, g_ * dk), lambda h, i: (i, h))
    col = lambda base: pl.BlockSpec((s, dk), lambda h, i: (0, base + h))
    cmp_spec = lambda j: pl.BlockSpec((1, 1, nc, dk), lambda h, i: (j, h, 0, 0))
    cw = NSA_CHAIN_LANES
    assert cw % tq == 0 and (g_ * tq) % cw == 0 and NSA_SEL_CHUNK % tq == 0
    n_chain = g_ * tq // cw
    return pl.pallas_call(
        functools.partial(_nsa_attn_kernel, seq=s),
        name="nsa_attention",
        grid=(hk, s // tq),
        in_specs=[qspec, qspec, cmp_spec(0), cmp_spec(1),
                  col(NSA_HEADS),
                  pl.BlockSpec((s, nsel), lambda h, i: (0, 0)),
                  col(2 * hk),
                  col(NSA_HEADS + hk), col(3 * hk),
                  pl.BlockSpec((1, 3 * g_, tq), lambda h, i: (h, 0, i))],
        out_specs=pl.BlockSpec((tq, g_ * dk), lambda h, i: (i, h)),
        out_shape=jax.ShapeDtypeStruct((s, NSA_HEADS * dk), BF16),
        scratch_shapes=[pltpu.VMEM((dk, s), BF16), pltpu.VMEM((dk, s), BF16),
                        pltpu.VMEM((n_chain, dk + nsel, cw), BF16),
                        pltpu.VMEM((2, n_chain, min(NSA_SEL_CHUNK, s), cw), F32),
                        pltpu.VMEM((n_chain, 1, cw), F32),
                        pltpu.VMEM((n_chain, 1, cw), F32),
                        pltpu.VMEM((n_chain, dk, cw), F32)],
        compiler_params=_params("parallel", "arbitrary"),
    )(raw, rot, cmp_kv, cmp_kv, rot, one_hot, plain, rot, plain, gates_t)


def _diff_attn_kernel(q_ref, k_ref, v_ref, lam_ref, sg_ref, o_ref, qq_scr, vt_scr, s_scr, m_scr,
                      l_scr, acc_scr, *, lambda_init):
    g_, tq, d = DIFF_GROUP, DIFF_TQ, DIFF_HEAD_DIM
    n_chain = 2 * g_
    qb = pl.program_id(1)

    @pl.when(qb == 0)
    def _():
        rows = 2 * tq
        for r in range(0, v_ref.shape[0], rows):
            vt_scr[:, r:r + rows] = v_ref[r:r + rows, :].T

    first_half = lax.broadcasted_iota(jnp.int32, (2 * d, 1), 0) < d
    for g in range(g_):
        q_t = q_ref[:, g * 2 * d:(g + 1) * 2 * d].T
        zero = jnp.zeros_like(q_t)
        qq_scr[2 * g] = jnp.where(first_half, q_t, zero)
        qq_scr[2 * g + 1] = jnp.where(first_half, zero, q_t)
    m_scr[...] = jnp.full(m_scr.shape, NEG_BIG, F32)
    l_scr[...] = jnp.zeros(l_scr.shape, F32)
    acc_scr[...] = jnp.zeros(acc_scr.shape, F32)

    ch = 2 * tq
    n_full = qb // 2

    def fold(off, buf, diagonal):
        off = pl.multiple_of(off, ch)
        v_t = vt_scr[:, pl.ds(off, ch)]
        state = [(m_scr[j], l_scr[j], acc_scr[j]) for j in range(n_chain)]
        if diagonal:
            key = lax.broadcasted_iota(jnp.int32, (ch, tq), 0) + (off - qb * tq)
            causal = key <= lax.broadcasted_iota(jnp.int32, (ch, tq), 1)
            todo = []
        else:
            k_next = k_ref[pl.ds(pl.multiple_of(off + ch, ch), ch), :]
            todo = list(range(n_chain))
        ahead = 2
        for j in todo[:ahead]:
            s_scr[1 - buf, j] = _dot(k_next, qq_scr[j])
        new_state = []
        for j in range(n_chain):
            s = s_scr[buf, j]
            if diagonal:
                s = jnp.where(causal, s, NEG_BIG)
            m_old, l_old, acc_old = state[j]
            m_new = jnp.maximum(m_old, jnp.max(s, axis=0, keepdims=True))
            alpha = jnp.exp2(m_old - m_new)
            p = jnp.exp2(s - m_new)
            l_new = alpha * l_old + jnp.sum(p, axis=0, keepdims=True)
            acc_new = alpha * acc_old + _dot(v_t, p.astype(BF16))
            new_state.append((m_new, l_new, acc_new))
            for jn in todo[j + ahead:j + ahead + 1]:
                s_scr[1 - buf, jn] = _dot(k_next, qq_scr[jn])
        for j in range(n_chain):
            m_scr[j], l_scr[j], acc_scr[j] = new_state[j]

    k_first = k_ref[:ch, :]
    for j in range(n_chain):
        s_scr[0, j] = _dot(k_first, qq_scr[j])

    def pair_step(c, carry):
        off = pl.multiple_of(c * 2 * ch, 2 * ch)
        fold(off, 0, False)
        fold(off + ch, 1, False)
        return carry

    lax.fori_loop(0, n_full // 2, pair_step, 0)
    rest = pl.multiple_of((n_full // 2) * 2 * ch, 2 * ch)

    @pl.when(n_full % 2 == 1)
    def _():
        fold(rest, 0, False)
        fold(rest + ch, 1, True)

    @pl.when(n_full % 2 == 0)
    def _():
        fold(rest, 0, True)

    lv = lam_ref[...]
    lam = (jnp.exp(jnp.sum(lv[0:1] * lv[1:2], axis=-1, keepdims=True))
           - jnp.exp(jnp.sum(lv[2:3] * lv[3:4], axis=-1, keepdims=True)) + lambda_init)
    sub_g = sg_ref[...]
    for g in range(g_):
        o1 = acc_scr[2 * g] / jnp.maximum(l_scr[2 * g], 1e-30)
        o2 = acc_scr[2 * g + 1] / jnp.maximum(l_scr[2 * g + 1], 1e-30)
        a = o1 - lam * o2
        ms = jnp.mean(a * a, axis=0, keepdims=True)
        a = a * lax.rsqrt(ms + NORM_EPS) * sub_g * (1.0 - lambda_init)
        o_ref[:, g * 2 * d:(g + 1) * 2 * d] = a.T.astype(BF16)


def _diff_attention(q_rot, k_rot, v, lam_vecs, subln_g, lambda_init):
    s = q_rot.shape[0]
    hk, g_, d, tq = DIFF_KV_HEADS, DIFF_GROUP, DIFF_HEAD_DIM, DIFF_TQ
    return pl.pallas_call(
        functools.partial(_diff_attn_kernel, lambda_init=lambda_init),
        name="diff_attention",
        grid=(hk, s // tq),
        in_specs=[pl.BlockSpec((tq, g_ * 2 * d), lambda h, i: (i, h)),
                  pl.BlockSpec((s, 2 * d), lambda h, i: (0, h)),
                  pl.BlockSpec((s, 2 * d), lambda h, i: (0, h)),
                  pl.BlockSpec((4, d), lambda h, i: (0, 0)),
                  pl.BlockSpec((2 * d, 1), lambda h, i: (0, 0))],
        out_specs=pl.BlockSpec((tq, g_ * 2 * d), lambda h, i: (i, h)),
        out_shape=jax.ShapeDtypeStruct((s, DIFF_HEADS * 2 * d), BF16),
        scratch_shapes=[pltpu.VMEM((2 * g_, 2 * d, tq), BF16),
                        pltpu.VMEM((2 * d, s), BF16),
                        pltpu.VMEM((2, 2 * g_, 2 * tq, tq), F32),
                        pltpu.VMEM((2 * g_, 1, tq), F32),
                        pltpu.VMEM((2 * g_, 1, tq), F32),
                        pltpu.VMEM((2 * g_, 2 * d, tq), F32)],
        compiler_params=_params("parallel", "arbitrary"),
    )(q_rot, k_rot, v, lam_vecs, subln_g.reshape(2 * d, 1))


def _rope_tables(seq, head_dim, reps):
    rot = head_dim // ROPE_FRACTION
    half = rot // 2
    inv = 1.0 / (ROPE_THETA ** (jnp.arange(0, rot, 2, dtype=F32) / rot))
    ang = jnp.arange(seq, dtype=F32)[:, None] * inv[None, :]
    cos, sin = jnp.cos(ang), jnp.sin(ang)
    rest = head_dim - rot
    zeros_h = jnp.zeros((seq, half), F32)
    c = jnp.concatenate([cos, cos, jnp.ones((seq, rest), F32)], axis=-1)
    sa = jnp.concatenate([-sin, zeros_h, jnp.zeros((seq, rest), F32)], axis=-1)
    sb = jnp.concatenate([zeros_h, sin, jnp.zeros((seq, rest), F32)], axis=-1)
    return tuple(jnp.tile(t, (1, reps)) for t in (c, sa, sb)), half


def _nsa_layer(x, norm_g, w_in, cmp_pos, cmp_w1, cmp_w2, w_out, tables, shift):
    s = x.shape[0]
    hk, g_, dk = NSA_KV_HEADS, NSA_GROUP, NSA_HEAD_DIM
    qw, kvw = NSA_HEADS * dk, hk * dk
    col_scale = jnp.where(jnp.arange(w_in.shape[1]) < qw, dk ** -0.5 * LOG2E, 1.0).astype(F32)
    w_all = (w_in * col_scale).astype(BF16)
    part = lambda i: w_all[:, qw + i * kvw:qw + (i + 1) * kvw]
    w_rope = jnp.concatenate([w_all[:, :qw], part(2), part(4)], axis=1)
    w_plain = jnp.concatenate([part(0), part(1), part(3), part(5)], axis=1)
    n_gate = 3 * NSA_HEADS
    w_gate = jnp.pad(w_all[:, qw + 6 * kvw:], ((0, 0), (0, LANES - n_gate)))

    raw, rot = _norm_proj_rope(x, norm_g, w_rope, tables, shift=shift, want_raw=True)
    plain = _norm_proj(x, norm_g, w_plain)
    gates = _norm_proj(x, norm_g, w_gate, act="sigmoid", out_dtype=F32)

    nc = s // CMP_STRIDE
    t2 = plain[:, :2 * kvw].reshape(nc, CMP_STRIDE, 2, hk, dk).transpose(2, 3, 0, 1, 4)
    t2 = t2.reshape(2, hk, nc, CMP_STRIDE * dk)
    cmp_kv = _compress(t2, cmp_pos.reshape(2, 2, CMP_STRIDE * dk), cmp_w1.astype(BF16),
                       cmp_w2.astype(BF16))
    o = _nsa_attention(raw, rot, plain, cmp_kv, gates)
    return _matmul_res(o, w_out.astype(BF16), x)


def kernel(x, attn_norm_g, mlp_norm_g, final_norm_g, nsa_w_in, nsa_cmp_pos, nsa_cmp_w1, nsa_cmp_w2, nsa_w_out, kv_norm_g, kv_w_shared, diff_w_q, diff_lambda, diff_subln_g, diff_w_out, mlp_w_up, mlp_w_down):
    b, s, d = x.shape
    tables_a, shift_a = _rope_tables(s, NSA_HEAD_DIM, 1)
    tables_b, shift_b = _rope_tables(s, DIFF_HEAD_DIM, 2)
    kcols = DIFF_KV_HEADS * 2 * DIFF_HEAD_DIM
    outs = []
    for bi in range(b):
        xs = x[bi]
        k_sh = v_sh = None
        for layer in range(DEPTH):
            if layer < N_A_LAYERS:
                xs = _nsa_layer(xs, attn_norm_g[layer], nsa_w_in[layer], nsa_cmp_pos[layer],
                                nsa_cmp_w1[layer], nsa_cmp_w2[layer], nsa_w_out[layer],
                                tables_a, shift_a)
            else:
                j = layer - N_A_LAYERS
                if j == 0:
                    k_sh = _norm_proj_rope(xs, kv_norm_g, kv_w_shared[:, :kcols].astype(BF16),
                                           tables_b, shift=shift_b, want_raw=False)
                    v_sh = _norm_proj(xs, kv_norm_g, kv_w_shared[:, kcols:].astype(BF16))
                lambda_init = 0.8 - 0.6 * math.exp(-0.3 * layer)
                w_q = (diff_w_q[j] * (DIFF_HEAD_DIM ** -0.5 * LOG2E)).astype(BF16)
                q = _norm_proj_rope(xs, attn_norm_g[layer], w_q, tables_b,
                                    shift=shift_b, want_raw=False)
                o = _diff_attention(q, k_sh, v_sh, diff_lambda[j], diff_subln_g[j], lambda_init)
                xs = _matmul_res(o, diff_w_out[j].astype(BF16), xs)
            xs = _mlp(xs, mlp_norm_g[layer], mlp_w_up[layer].astype(BF16),
                      mlp_w_down[layer].astype(BF16), final_norm_g,
                      final_norm=(layer == DEPTH - 1))
        outs.append(xs)
    return jnp.stack(outs, axis=0)
```

```python
import functools
import math

import jax
import jax.numpy as jnp
from jax import lax
from jax.experimental import pallas as pl
from jax.experimental.pallas import tpu as pltpu

F32 = jnp.float32
BF16 = jnp.bfloat16

D_MODEL = 2048
DEPTH = 4
N_A_LAYERS = DEPTH // 2

NSA_HEADS = 16
NSA_KV_HEADS = 4
NSA_GROUP = NSA_HEADS // NSA_KV_HEADS
NSA_HEAD_DIM = D_MODEL // NSA_HEADS
CMP_BLOCK = 32
CMP_STRIDE = 16
CMP_HIDDEN = 4 * NSA_HEAD_DIM
SEL_BLOCK = 64
SEL_SHIFT = SEL_BLOCK.bit_length() - 1
SEL_TOPK = 16
WINDOW = 512
FORCE_SCORE = 1.0e4

DIFF_HEADS = 16
DIFF_KV_HEADS = 4
DIFF_GROUP = DIFF_HEADS // DIFF_KV_HEADS
DIFF_HEAD_DIM = D_MODEL // (2 * DIFF_HEADS)

D_FF = 4 * D_MODEL
ROPE_THETA = 500000.0
ROPE_FRACTION = 4
NORM_EPS = 1e-6

LANES = 128
VMEM_LIMIT = 48 * 1024 * 1024
VMEM_BUDGET = 40 * 1024 * 1024
NEG_BIG = -1.0e30
SCORE_FLOOR = -1.0e28
LOG2E = 1.4426950408889634

NSA_TQ = 256
NSA_SEL_CHUNK = 512
NSA_CHAIN_LANES = 256
DIFF_TQ = 256
ROW_TILE = 512
COL_TILE = 512
FF_TILE = 1024


def _params(*sem):
    return pltpu.CompilerParams(dimension_semantics=sem, vmem_limit_bytes=VMEM_LIMIT)


def _dot(a, b):
    return jnp.dot(a, b, preferred_element_type=F32)


def _rms(x, g):
    ms = jnp.mean(x * x, axis=-1, keepdims=True)
    return x * lax.rsqrt(ms + NORM_EPS) * g


def _proj_tiles(s, k, n, row_bytes):
    weight = 2 * k * n * 2
    for tm in (ROW_TILE, ROW_TILE // 2):
        if weight + tm * row_bytes <= VMEM_BUDGET:
            return min(tm, s), n
    return min(ROW_TILE, s), min(COL_TILE, n)


def _weight_spec(w, widx, k, tn):
    if w.ndim == 2:
        return pl.BlockSpec((k, tn), lambda i, j: (0, j))
    return pl.BlockSpec((None, k, tn), lambda i, j: (widx, 0, j))


def _store_col_blocks(o_ref, acc):
    for blk in range(acc.shape[1] // LANES):
        o_ref[blk] = acc[:, blk * LANES:(blk + 1) * LANES].astype(o_ref.dtype)


def _blocked_out(s, n, tm, tn, dtype):
    spec = pl.BlockSpec((tn // LANES, tm, LANES), lambda i, j: (j, i, 0))
    return spec, jax.ShapeDtypeStruct((n // LANES, s, LANES), dtype)


def _norm_proj_kernel(x_ref, g_ref, w_ref, o_ref, h_scr, *, act, blocked):
    @pl.when(pl.program_id(1) == 0)
    def _():
        h_scr[...] = _rms(x_ref[...], g_ref[...]).astype(BF16)

    acc = _dot(h_scr[...], w_ref[...])
    if act == "sigmoid":
        acc = jax.nn.sigmoid(acc)
    if blocked:
        _store_col_blocks(o_ref, acc)
    else:
        o_ref[...] = acc.astype(o_ref.dtype)


def _norm_proj(x, g, w, *, widx=None, act=None, out_dtype=BF16, blocked=False):
    s, d = x.shape
    n = w.shape[-1]
    tm, tn = _proj_tiles(s, d, n, 2 * d * 4 + d * 2 + n * 4 + 2 * n * jnp.dtype(out_dtype).itemsize)
    if blocked:
        out_spec, out_shape = _blocked_out(s, n, tm, tn, out_dtype)
    else:
        out_spec = pl.BlockSpec((tm, tn), lambda i, j: (i, j))
        out_shape = jax.ShapeDtypeStruct((s, n), out_dtype)
    return pl.pallas_call(
        functools.partial(_norm_proj_kernel, act=act, blocked=blocked),
        name="norm_proj",
        grid=(s // tm, n // tn),
        in_specs=[pl.BlockSpec((tm, d), lambda i, j: (i, 0)),
                  pl.BlockSpec((1, d), lambda i, j: (0, 0)),
                  _weight_spec(w, widx, d, tn)],
        out_specs=out_spec,
        out_shape=out_shape,
        scratch_shapes=[pltpu.VMEM((tm, d), BF16)],
        compiler_params=_params("parallel", "arbitrary"),
    )(x, g.reshape(1, d), w)


def _norm_proj_rope_kernel(x_ref, g_ref, w_ref, c_ref, sa_ref, sb_ref, *rest, shift, want_raw):
    if want_raw:
        raw_ref, rot_ref, h_scr = rest
    else:
        rot_ref, h_scr = rest

    @pl.when(pl.program_id(1) == 0)
    def _():
        h_scr[...] = _rms(x_ref[...], g_ref[...]).astype(BF16)

    acc = _dot(h_scr[...], w_ref[...])
    if want_raw:
        _store_col_blocks(raw_ref, acc)
    c, sa, sb = c_ref[...], sa_ref[...], sb_ref[...]
    for blk in range(acc.shape[1] // LANES):
        a = acc[:, blk * LANES:(blk + 1) * LANES]
        r = a * c + pltpu.roll(a, LANES - shift, 1) * sa + pltpu.roll(a, shift, 1) * sb
        rot_ref[blk] = r.astype(BF16)


def _norm_proj_rope(x, g, w, tables, *, shift, want_raw, widx=None):
    s, d = x.shape
    n = w.shape[-1]
    n_out = 2 if want_raw else 1
    tm, tn = _proj_tiles(s, d, n, 2 * d * 4 + d * 2 + n * 4 + n_out * 2 * n * 2 + 3 * 2 * LANES * 4)
    tab_spec = pl.BlockSpec((tm, LANES), lambda i, j: (i, 0))
    out_spec, out_sd = _blocked_out(s, n, tm, tn, BF16)
    return pl.pallas_call(
        functools.partial(_norm_proj_rope_kernel, shift=shift, want_raw=want_raw),
        name="norm_proj_rope",
        grid=(s // tm, n // tn),
        in_specs=[pl.BlockSpec((tm, d), lambda i, j: (i, 0)),
                  pl.BlockSpec((1, d), lambda i, j: (0, 0)),
                  _weight_spec(w, widx, d, tn),
                  tab_spec, tab_spec, tab_spec],
        out_specs=[out_spec, out_spec] if want_raw else out_spec,
        out_shape=[out_sd, out_sd] if want_raw else out_sd,
        scratch_shapes=[pltpu.VMEM((tm, d), BF16)],
        compiler_params=_params("parallel", "arbitrary"),
    )(x, g.reshape(1, d), w, *tables)


def _matmul_res_kernel(a_ref, w_ref, r_ref, o_ref):
    o_ref[...] = r_ref[...] + _dot(a_ref[...], w_ref[...])


def _matmul_res(a, w, res, *, widx=None):
    s, k = a.shape
    n = w.shape[-1]
    tm, tn = _proj_tiles(s, k, n, 2 * k * 2 + 2 * 2 * n * 4 + n * 4)
    return pl.pallas_call(
        _matmul_res_kernel,
        name="matmul_res",
        grid=(s // tm, n // tn),
        in_specs=[pl.BlockSpec((tm, k), lambda i, j: (i, 0)),
                  _weight_spec(w, widx, k, tn),
                  pl.BlockSpec((tm, tn), lambda i, j: (i, j))],
        out_specs=pl.BlockSpec((tm, tn), lambda i, j: (i, j)),
        out_shape=jax.ShapeDtypeStruct((s, n), F32),
        compiler_params=_params("parallel", "arbitrary"),
    )(a, w, res)


def _mlp_kernel(x_ref, g_ref, wu_ref, wd_ref, fg_ref, o_ref, h_scr, acc_scr, *, final_norm):
    j = pl.program_id(1)

    @pl.when(j == 0)
    def _():
        h_scr[...] = _rms(x_ref[...], g_ref[...]).astype(BF16)
        acc_scr[...] = jnp.zeros_like(acc_scr)

    u = _dot(h_scr[...], wu_ref[...])
    u = jnp.square(jnp.maximum(u, 0.0)).astype(BF16)
    acc_scr[...] += _dot(u, wd_ref[...])

    @pl.when(j == pl.num_programs(1) - 1)
    def _():
        y = x_ref[...] + acc_scr[...]
        if final_norm:
            y = _rms(y, fg_ref[...])
        o_ref[...] = y


def _mlp(x, g, w_up, w_down, layer, final_g, *, final_norm):
    s, d = x.shape
    ff = w_up.shape[-1]
    tm, tf = min(ROW_TILE, s), min(FF_TILE, ff)
    return pl.pallas_call(
        functools.partial(_mlp_kernel, final_norm=final_norm),
        name="sq_relu_mlp",
        grid=(s // tm, ff // tf),
        in_specs=[pl.BlockSpec((tm, d), lambda i, j: (i, 0)),
                  pl.BlockSpec((1, d), lambda i, j: (0, 0)),
                  pl.BlockSpec((None, d, tf), lambda i, j: (layer, 0, j)),
                  pl.BlockSpec((None, tf, d), lambda i, j: (layer, j, 0)),
                  pl.BlockSpec((1, d), lambda i, j: (0, 0))],
        out_specs=pl.BlockSpec((tm, d), lambda i, j: (i, 0)),
        out_shape=jax.ShapeDtypeStruct((s, d), F32),
        scratch_shapes=[pltpu.VMEM((tm, d), BF16), pltpu.VMEM((tm, d), F32)],
        compiler_params=_params("parallel", "arbitrary"),
    )(x, g.reshape(1, d), w_up, w_down, final_g.reshape(1, d))


def _compress_kernel(t_ref, pos_ref, w1_ref, w2_ref, o_ref):
    t = t_ref[0, 0].astype(F32)
    pos = pos_ref[0]
    half = t.shape[1]
    a = _dot((t + pos[0:1]).astype(BF16), w1_ref[0, :half, :])
    b = _dot((t + pos[1:2]).astype(BF16), w1_ref[0, half:, :])
    nc = t.shape[0]
    hid = jax.nn.gelu(a + pltpu.roll(b, nc - 1, 0))
    o_ref[0, 0] = _dot(hid.astype(BF16), w2_ref[0]).astype(BF16)


def _compress(t2, pos2, w1, w2):
    _, hk, nc, width = t2.shape
    dk = w2.shape[-1]
    return pl.pallas_call(
        _compress_kernel,
        name="nsa_compress",
        grid=(2, hk),
        in_specs=[pl.BlockSpec((1, 1, nc, width), lambda j, h: (j, h, 0, 0)),
                  pl.BlockSpec((1, 2, width), lambda j, h: (j, 0, 0)),
                  pl.BlockSpec((1, 2 * width, CMP_HIDDEN), lambda j, h: (j, 0, 0)),
                  pl.BlockSpec((1, CMP_HIDDEN, dk), lambda j, h: (j, 0, 0))],
        out_specs=pl.BlockSpec((1, 1, nc, dk), lambda j, h: (j, h, 0, 0)),
        out_shape=jax.ShapeDtypeStruct((2, hk, nc, dk), BF16),
        compiler_params=_params("parallel", "arbitrary"),
    )(t2, pos2, w1, w2)


def _nsa_attn_kernel(qr_ref, qo_ref, kc_ref, vc_ref, ks_ref, oh_ref, vs_ref, kw_ref, vw_ref, gt_ref,
                     o_ref, vst_scr, vwt_scr, rhs_scr, s_scr, m_scr, l_scr, acc_scr, *, seq):
    g_, tq, dk = NSA_GROUP, NSA_TQ, NSA_HEAD_DIM
    cw = NSA_CHAIN_LANES
    n_chain = g_ * tq // cw
    nc = seq // CMP_STRIDE
    nsel = seq // SEL_BLOCK
    ch = min(NSA_SEL_CHUNK, seq)
    wlen = WINDOW + tq
    qb = pl.program_id(1)
    start = qb * tq
    t_row = start + lax.broadcasted_iota(jnp.int32, (1, tq), 1)

    @pl.when(qb == 0)
    def _():
        for r in range(0, seq, ch):
            vst_scr[:, r:r + ch] = vs_ref[r:r + ch, :].T
            vwt_scr[:, r:r + ch] = vw_ref[r:r + ch, :].T

    def heads_on_lanes(ref):
        return jnp.concatenate([ref[g].T for g in range(g_)], axis=1)

    def tile_heads(x, n):
        return jnp.concatenate([x] * n, axis=1)

    def softmax_keys(s2):
        m = jnp.max(s2, axis=0, keepdims=True)
        m = jnp.where(m == -jnp.inf, 0.0, m)
        e = jnp.exp2(s2 - m)
        return e, 1.0 / jnp.maximum(jnp.sum(e, axis=0, keepdims=True), 1e-30)

    qr_t = heads_on_lanes(qr_ref)
    qo_t = heads_on_lanes(qo_ref)

    n_end = lax.broadcasted_iota(jnp.int32, (nc, tq), 0) * CMP_STRIDE + (CMP_BLOCK - 1)
    bias_c = jnp.where(n_end <= t_row, 0.0, -jnp.inf)
    e_c, r_c = softmax_keys(_dot(kc_ref[0, 0], qr_t) + tile_heads(bias_c, g_))
    p_c = e_c * r_c
    o_c = _dot(vc_ref[0, 0].T, p_c.astype(BF16))

    w0 = pl.multiple_of(jnp.maximum(start - WINDOW, 0), tq)
    kpos = w0 + lax.broadcasted_iota(jnp.int32, (wlen, tq), 0)
    bias_w = jnp.where((kpos <= t_row) & (kpos > t_row - WINDOW), 0.0, -jnp.inf)
    e_w, r_w = softmax_keys(_dot(kw_ref[pl.ds(w0, wlen), :], qo_t) + tile_heads(bias_w, g_))
    o_w = _dot(vwt_scr[:, pl.ds(w0, wlen)], e_w.astype(BF16)) * r_w

    p_sum = p_c[:, :tq]
    for g in range(1, g_):
        p_sum = p_sum + p_c[:, g * tq:(g + 1) * tq]
    n_i = lax.broadcasted_iota(jnp.int32, (nsel, nc), 1) * CMP_STRIDE
    s_i = lax.broadcasted_iota(jnp.int32, (nsel, nc), 0) * SEL_BLOCK
    overlap = jnp.where((n_i < s_i + SEL_BLOCK) & (n_i + CMP_BLOCK > s_i), 1.0, 0.0).astype(BF16)
    p_hi = p_sum.astype(BF16)
    p_r1 = p_sum - p_hi.astype(F32)
    p_mid = p_r1.astype(BF16)
    p_lo = (p_r1 - p_mid.astype(F32)).astype(BF16)
    imp = _dot(overlap, p_hi) + _dot(overlap, p_mid) + _dot(overlap, p_lo)

    ids = lax.broadcasted_iota(jnp.int32, (nsel, tq), 0)
    cur = t_row >> SEL_SHIFT
    forced = (ids == 0) | (ids == cur) | (ids == cur - 1)
    imp = jnp.where(forced, FORCE_SCORE, imp)
    work = jnp.where(ids * SEL_BLOCK <= t_row, imp, -1.0)

    sel = jnp.zeros((nsel, tq), F32)
    for _ in range(min(SEL_TOPK, nsel)):
        mx = jnp.max(work, axis=0, keepdims=True)
        first = jnp.min(jnp.where(work == mx, ids, nsel), axis=0, keepdims=True)
        pick = ids == first
        sel = jnp.where(pick, 1.0, sel)
        work = jnp.where(pick, -3.0e38, work)

    neg_sel = tile_heads(jnp.where(sel > 0.5, 0.0, NEG_BIG).astype(BF16), cw // tq)
    for j in range(n_chain):
        rhs_scr[j] = jnp.concatenate([qo_t[:, j * cw:(j + 1) * cw], neg_sel], axis=0)
    m_scr[...] = jnp.full(m_scr.shape, NEG_BIG, F32)
    l_scr[...] = jnp.zeros(l_scr.shape, F32)
    acc_scr[...] = jnp.zeros(acc_scr.shape, F32)

    def keys_with_block(off):
        return jnp.concatenate([ks_ref[pl.ds(off, ch), :], oh_ref[pl.ds(off, ch), :]], axis=1)

    def fold(off, buf, diagonal):
        off = pl.multiple_of(off, ch)
        v_t = vst_scr[:, pl.ds(off, ch)]
        state = [(m_scr[j], l_scr[j], acc_scr[j]) for j in range(n_chain)]
        if diagonal:
            key = off + lax.broadcasted_iota(jnp.int32, (ch, tq), 0)
            causal = tile_heads(jnp.where(key <= t_row, 0.0, NEG_BIG), cw // tq)
            todo = []
        else:
            k_next = keys_with_block(pl.multiple_of(off + ch, ch))
            todo = list(range(n_chain))
        ahead = 2
        for j in todo[:ahead]:
            s_scr[1 - buf, j] = _dot(k_next, rhs_scr[j])
        new_state = []
        for j in range(n_chain):
            s = s_scr[buf, j]
            if diagonal:
                s = s + causal
            m_old, l_old, acc_old = state[j]
            m_new = jnp.maximum(m_old, jnp.max(s, axis=0, keepdims=True))
            alpha = jnp.exp2(m_old - m_new)
            e = jnp.exp2(s - m_new)
            l_new = alpha * l_old + jnp.sum(e, axis=0, keepdims=True)
            acc_new = alpha * acc_old + _dot(v_t, e.astype(BF16))
            new_state.append((m_new, l_new, acc_new))
            for jn in todo[j + ahead:j + ahead + 1]:
                s_scr[1 - buf, jn] = _dot(k_next, rhs_scr[jn])
        for j in range(n_chain):
            m_scr[j], l_scr[j], acc_scr[j] = new_state[j]

    k_first = keys_with_block(0)
    for j in range(n_chain):
        s_scr[0, j] = _dot(k_first, rhs_scr[j])

    def pair_step(c, carry):
        off = pl.multiple_of(c * 2 * ch, 2 * ch)
        fold(off, 0, False)
        fold(off + ch, 1, False)
        return carry

    n_full = (start + tq - 1) // ch
    lax.fori_loop(0, n_full // 2, pair_step, 0)
    rest = pl.multiple_of((n_full // 2) * 2 * ch, 2 * ch)

    @pl.when(n_full % 2 == 1)
    def _():
        fold(rest, 0, False)
        fold(rest + ch, 1, True)

    @pl.when(n_full % 2 == 0)
    def _():
        fold(rest, 0, True)

    gt = gt_ref[0]
    for g in range(g_):
        j = g * tq // cw
        lanes = slice(g * tq - j * cw, g * tq - j * cw + tq)
        o_s = jnp.where(m_scr[j][:, lanes] > SCORE_FLOOR,
                        acc_scr[j][:, lanes] * (1.0 / jnp.maximum(l_scr[j][:, lanes], 1e-30)), 0.0)
        cols = slice(g * tq, (g + 1) * tq)
        o_t = (gt[3 * g:3 * g + 1] * o_c[:, cols] + gt[3 * g + 1:3 * g + 2] * o_s
               + gt[3 * g + 2:3 * g + 3] * o_w[:, cols])
        o_ref[:, g * dk:(g + 1) * dk] = o_t.T.astype(BF16)


def _nsa_attention(raw, rot, plain, cmp_kv, gates):
    s = raw.shape[1]
    hk, g_, dk, tq = NSA_KV_HEADS, NSA_GROUP, NSA_HEAD_DIM, NSA_TQ
    nc = s // CMP_STRIDE
    nsel = s // SEL_BLOCK
    one_hot = (jnp.arange(s)[:, None] >> SEL_SHIFT == jnp.arange(nsel)[None, :]).astype(BF16)
    gates_t = gates[:, :3 * NSA_HEADS].T.reshape(hk, 3 * g_, s)
    qspec = pl.BlockSpec((g_, tq, dk), lambda h, i: (h, i, 0))
    col = lambda base: pl.BlockSpec((None, s, dk), lambda h, i: (base + h, 0, 0))
    cmp_spec = lambda j: pl.BlockSpec((1, 1, nc, dk), lambda h, i: (j, h, 0, 0))
    cw = NSA_CHAIN_LANES
    assert cw % tq == 0 and (g_ * tq) % cw == 0 and NSA_SEL_CHUNK % tq == 0
    n_chain = g_ * tq // cw
    return pl.pallas_call(
        functools.partial(_nsa_attn_kernel, seq=s),
        name="nsa_attention",
        grid=(hk, s // tq),
        in_specs=[qspec, qspec, cmp_spec(0), cmp_spec(1),
                  col(NSA_HEADS),
                  pl.BlockSpec((s, nsel), lambda h, i: (0, 0)),
                  col(2 * hk),
                  col(NSA_HEADS + hk), col(3 * hk),
                  pl.BlockSpec((1, 3 * g_, tq), lambda h, i: (h, 0, i))],
        out_specs=pl.BlockSpec((tq, g_ * dk), lambda h, i: (i, h)),
        out_shape=jax.ShapeDtypeStruct((s, NSA_HEADS * dk), BF16),
        scratch_shapes=[pltpu.VMEM((dk, s), BF16), pltpu.VMEM((dk, s), BF16),
                        pltpu.VMEM((n_chain, dk + nsel, cw), BF16),
                        pltpu.VMEM((2, n_chain, min(NSA_SEL_CHUNK, s), cw), F32),
                        pltpu.VMEM((n_chain, 1, cw), F32),
                        pltpu.VMEM((n_chain, 1, cw), F32),
                        pltpu.VMEM((n_chain, dk, cw), F32)],
        compiler_params=_params("parallel", "arbitrary"),
    )(raw, rot, cmp_kv, cmp_kv, rot, one_hot, plain, rot, plain, gates_t)


def _diff_attn_kernel(q_ref, k_ref, v_ref, lam_ref, sg_ref, o_ref, qq_scr, vt_scr, s_scr, m_scr,
                      l_scr, acc_scr, *, lambda_init):
    g_, tq, d = DIFF_GROUP, DIFF_TQ, DIFF_HEAD_DIM
    n_chain = 2 * g_
    qb = pl.program_id(1)

    @pl.when(qb == 0)
    def _():
        rows = 2 * tq
        for r in range(0, v_ref.shape[0], rows):
            vt_scr[:, r:r + rows] = v_ref[r:r + rows, :].T

    first_half = lax.broadcasted_iota(jnp.int32, (2 * d, 1), 0) < d
    for g in range(g_):
        q_t = q_ref[g].T
        zero = jnp.zeros_like(q_t)
        qq_scr[2 * g] = jnp.where(first_half, q_t, zero)
        qq_scr[2 * g + 1] = jnp.where(first_half, zero, q_t)
    m_scr[...] = jnp.full(m_scr.shape, NEG_BIG, F32)
    l_scr[...] = jnp.zeros(l_scr.shape, F32)
    acc_scr[...] = jnp.zeros(acc_scr.shape, F32)

    ch = 2 * tq
    n_full = qb // 2

    def fold(off, buf, diagonal):
        off = pl.multiple_of(off, ch)
        v_t = vt_scr[:, pl.ds(off, ch)]
        state = [(m_scr[j], l_scr[j], acc_scr[j]) for j in range(n_chain)]
        if diagonal:
            key = lax.broadcasted_iota(jnp.int32, (ch, tq), 0) + (off - qb * tq)
            causal = key <= lax.broadcasted_iota(jnp.int32, (ch, tq), 1)
            todo = []
        else:
            k_next = k_ref[pl.ds(pl.multiple_of(off + ch, ch), ch), :]
            todo = list(range(n_chain))
        ahead = 2
        for j in todo[:ahead]:
            s_scr[1 - buf, j] = _dot(k_next, qq_scr[j])
        new_state = []
        for j in range(n_chain):
            s = s_scr[buf, j]
            if diagonal:
                s = jnp.where(causal, s, NEG_BIG)
            m_old, l_old, acc_old = state[j]
            m_new = jnp.maximum(m_old, jnp.max(s, axis=0, keepdims=True))
            alpha = jnp.exp2(m_old - m_new)
            p = jnp.exp2(s - m_new)
            l_new = alpha * l_old + jnp.sum(p, axis=0, keepdims=True)
            acc_new = alpha * acc_old + _dot(v_t, p.astype(BF16))
            new_state.append((m_new, l_new, acc_new))
            for jn in todo[j + ahead:j + ahead + 1]:
                s_scr[1 - buf, jn] = _dot(k_next, qq_scr[jn])
        for j in range(n_chain):
            m_scr[j], l_scr[j], acc_scr[j] = new_state[j]

    k_first = k_ref[:ch, :]
    for j in range(n_chain):
        s_scr[0, j] = _dot(k_first, qq_scr[j])

    def pair_step(c, carry):
        off = pl.multiple_of(c * 2 * ch, 2 * ch)
        fold(off, 0, False)
        fold(off + ch, 1, False)
        return carry

    lax.fori_loop(0, n_full // 2, pair_step, 0)
    rest = pl.multiple_of((n_full // 2) * 2 * ch, 2 * ch)

    @pl.when(n_full % 2 == 1)
    def _():
        fold(rest, 0, False)
        fold(rest + ch, 1, True)

    @pl.when(n_full % 2 == 0)
    def _():
        fold(rest, 0, True)

    lv = lam_ref[...]
    lam = (jnp.exp(jnp.sum(lv[0:1] * lv[1:2], axis=-1, keepdims=True))
           - jnp.exp(jnp.sum(lv[2:3] * lv[3:4], axis=-1, keepdims=True)) + lambda_init)
    sub_g = sg_ref[...]
    for g in range(g_):
        o1 = acc_scr[2 * g] / jnp.maximum(l_scr[2 * g], 1e-30)
        o2 = acc_scr[2 * g + 1] / jnp.maximum(l_scr[2 * g + 1], 1e-30)
        a = o1 - lam * o2
        ms = jnp.mean(a * a, axis=0, keepdims=True)
        a = a * lax.rsqrt(ms + NORM_EPS) * sub_g * (1.0 - lambda_init)
        o_ref[:, g * 2 * d:(g + 1) * 2 * d] = a.T.astype(BF16)


def _diff_attention(q_rot, k_rot, v, lam_vecs, subln_g, lambda_init):
    s = q_rot.shape[1]
    hk, g_, d, tq = DIFF_KV_HEADS, DIFF_GROUP, DIFF_HEAD_DIM, DIFF_TQ
    return pl.pallas_call(
        functools.partial(_diff_attn_kernel, lambda_init=lambda_init),
        name="diff_attention",
        grid=(hk, s // tq),
        in_specs=[pl.BlockSpec((g_, tq, 2 * d), lambda h, i: (h, i, 0)),
                  pl.BlockSpec((None, s, 2 * d), lambda h, i: (h, 0, 0)),
                  pl.BlockSpec((None, s, 2 * d), lambda h, i: (h, 0, 0)),
                  pl.BlockSpec((4, d), lambda h, i: (0, 0)),
                  pl.BlockSpec((2 * d, 1), lambda h, i: (0, 0))],
        out_specs=pl.BlockSpec((tq, g_ * 2 * d), lambda h, i: (i, h)),
        out_shape=jax.ShapeDtypeStruct((s, DIFF_HEADS * 2 * d), BF16),
        scratch_shapes=[pltpu.VMEM((2 * g_, 2 * d, tq), BF16),
                        pltpu.VMEM((2 * d, s), BF16),
                        pltpu.VMEM((2, 2 * g_, 2 * tq, tq), F32),
                        pltpu.VMEM((2 * g_, 1, tq), F32),
                        pltpu.VMEM((2 * g_, 1, tq), F32),
                        pltpu.VMEM((2 * g_, 2 * d, tq), F32)],
        compiler_params=_params("parallel", "arbitrary"),
    )(q_rot, k_rot, v, lam_vecs, subln_g.reshape(2 * d, 1))


def _rope_tables(seq, head_dim, reps):
    rot = head_dim // ROPE_FRACTION
    half = rot // 2
    inv = 1.0 / (ROPE_THETA ** (jnp.arange(0, rot, 2, dtype=F32) / rot))
    ang = jnp.arange(seq, dtype=F32)[:, None] * inv[None, :]
    cos, sin = jnp.cos(ang), jnp.sin(ang)
    rest = head_dim - rot
    zeros_h = jnp.zeros((seq, half), F32)
    c = jnp.concatenate([cos, cos, jnp.ones((seq, rest), F32)], axis=-1)
    sa = jnp.concatenate([-sin, zeros_h, jnp.zeros((seq, rest), F32)], axis=-1)
    sb = jnp.concatenate([zeros_h, sin, jnp.zeros((seq, rest), F32)], axis=-1)
    return tuple(jnp.tile(t, (1, reps)) for t in (c, sa, sb)), half


def _nsa_layer(x, norm_g, w_in, cmp_pos, cmp_w1, cmp_w2, w_out, layer, tables, shift):
    s = x.shape[0]
    hk, dk = NSA_KV_HEADS, NSA_HEAD_DIM
    qw, kvw = NSA_HEADS * dk, hk * dk
    col_scale = jnp.where(jnp.arange(w_in.shape[1]) < qw, dk ** -0.5 * LOG2E, 1.0).astype(F32)
    w_all = (w_in * col_scale).astype(BF16)
    part = lambda i: w_all[:, qw + i * kvw:qw + (i + 1) * kvw]
    w_rope = jnp.concatenate([w_all[:, :qw], part(2), part(4)], axis=1)
    w_plain = jnp.concatenate([part(0), part(1), part(3), part(5)], axis=1)
    n_gate = 3 * NSA_HEADS
    w_gate = jnp.pad(w_all[:, qw + 6 * kvw:], ((0, 0), (0, LANES - n_gate)))

    raw, rot = _norm_proj_rope(x, norm_g, w_rope, tables, shift=shift, want_raw=True)
    plain = _norm_proj(x, norm_g, w_plain, blocked=True)
    gates = _norm_proj(x, norm_g, w_gate, act="sigmoid", out_dtype=F32)

    nc = s // CMP_STRIDE
    t2 = plain[:2 * hk].reshape(2, hk, nc, CMP_STRIDE * dk)
    cmp_kv = _compress(t2, cmp_pos.reshape(2, 2, CMP_STRIDE * dk), cmp_w1.astype(BF16),
                       cmp_w2.astype(BF16))
    o = _nsa_attention(raw, rot, plain, cmp_kv, gates)
    return _matmul_res(o, w_out, x, widx=layer)


def kernel(x, attn_norm_g, mlp_norm_g, final_norm_g, nsa_w_in, nsa_cmp_pos, nsa_cmp_w1, nsa_cmp_w2, nsa_w_out, kv_norm_g, kv_w_shared, diff_w_q, diff_lambda, diff_subln_g, diff_w_out, mlp_w_up, mlp_w_down):
    b, s, d = x.shape
    tables_a, shift_a = _rope_tables(s, NSA_HEAD_DIM, 1)
    tables_b, shift_b = _rope_tables(s, DIFF_HEAD_DIM, 2)
    kcols = DIFF_KV_HEADS * 2 * DIFF_HEAD_DIM
    w_up, w_down = mlp_w_up.astype(BF16), mlp_w_down.astype(BF16)
    nsa_out, diff_out = nsa_w_out.astype(BF16), diff_w_out.astype(BF16)
    diff_q = (diff_w_q * (DIFF_HEAD_DIM ** -0.5 * LOG2E)).astype(BF16)
    outs = []
    for bi in range(b):
        xs = x[bi]
        k_sh = v_sh = None
        for layer in range(DEPTH):
            if layer < N_A_LAYERS:
                xs = _nsa_layer(xs, attn_norm_g[layer], nsa_w_in[layer], nsa_cmp_pos[layer],
                                nsa_cmp_w1[layer], nsa_cmp_w2[layer], nsa_out, layer,
                                tables_a, shift_a)
            else:
                j = layer - N_A_LAYERS
                if j == 0:
                    k_sh = _norm_proj_rope(xs, kv_norm_g, kv_w_shared[:, :kcols].astype(BF16),
                                           tables_b, shift=shift_b, want_raw=False)
                    v_sh = _norm_proj(xs, kv_norm_g, kv_w_shared[:, kcols:].astype(BF16),
                                      blocked=True)
                lambda_init = 0.8 - 0.6 * math.exp(-0.3 * layer)
                q = _norm_proj_rope(xs, attn_norm_g[layer], diff_q, tables_b, widx=j,
                                    shift=shift_b, want_raw=False)
                o = _diff_attention(q, k_sh, v_sh, diff_lambda[j], diff_subln_g[j], lambda_init)
                xs = _matmul_res(o, diff_out, xs, widx=j)
            xs = _mlp(xs, mlp_norm_g[layer], w_up, w_down, layer, final_norm_g,
                      final_norm=(layer == DEPTH - 1))
        outs.append(xs)
    return jnp.stack(outs, axis=0)
```

```python
import functools
import math

import jax
import jax.numpy as jnp
from jax import lax
from jax.experimental import pallas as pl
from jax.experimental.pallas import tpu as pltpu

F32 = jnp.float32
BF16 = jnp.bfloat16

D_MODEL = 2048
DEPTH = 4
N_A_LAYERS = DEPTH // 2

NSA_HEADS = 16
NSA_KV_HEADS = 4
NSA_GROUP = NSA_HEADS // NSA_KV_HEADS
NSA_HEAD_DIM = D_MODEL // NSA_HEADS
CMP_BLOCK = 32
CMP_STRIDE = 16
CMP_HIDDEN = 4 * NSA_HEAD_DIM
SEL_BLOCK = 64
SEL_SHIFT = SEL_BLOCK.bit_length() - 1
SEL_TOPK = 16
WINDOW = 512
FORCE_SCORE = 1.0e4

DIFF_HEADS = 16
DIFF_KV_HEADS = 4
DIFF_GROUP = DIFF_HEADS // DIFF_KV_HEADS
DIFF_HEAD_DIM = D_MODEL // (2 * DIFF_HEADS)

D_FF = 4 * D_MODEL
ROPE_THETA = 500000.0
ROPE_FRACTION = 4
NORM_EPS = 1e-6

LANES = 128
VMEM_LIMIT = 48 * 1024 * 1024
VMEM_BUDGET = 40 * 1024 * 1024
NEG_BIG = -1.0e30
SCORE_FLOOR = -1.0e28
LOG2E = 1.4426950408889634

NSA_TQ = 256
NSA_SEL_CHUNK = 512
NSA_CHAIN_LANES = 256
DIFF_TQ = 256
ROW_TILE = 512
COL_TILE = 512
FF_TILE = 1024


def _params(*sem):
    return pltpu.CompilerParams(dimension_semantics=sem, vmem_limit_bytes=VMEM_LIMIT)


def _dot(a, b):
    return jnp.dot(a, b, preferred_element_type=F32)


def _rms(x, g):
    ms = jnp.mean(x * x, axis=-1, keepdims=True)
    return x * lax.rsqrt(ms + NORM_EPS) * g


def _proj_tiles(s, k, n, row_bytes):
    weight = 2 * k * n * 2
    for tm in (ROW_TILE, ROW_TILE // 2):
        if weight + tm * row_bytes <= VMEM_BUDGET:
            return min(tm, s), n
    return min(ROW_TILE, s), min(COL_TILE, n)


def _weight_spec(w, widx, k, tn):
    if w.ndim == 2:
        return pl.BlockSpec((k, tn), lambda i, j: (0, j))
    return pl.BlockSpec((None, k, tn), lambda i, j: (widx, 0, j))


def _store_col_blocks(o_ref, acc):
    for blk in range(acc.shape[1] // LANES):
        o_ref[blk] = acc[:, blk * LANES:(blk + 1) * LANES].astype(o_ref.dtype)


def _blocked_out(s, n, tm, tn, dtype):
    spec = pl.BlockSpec((tn // LANES, tm, LANES), lambda i, j: (j, i, 0))
    return spec, jax.ShapeDtypeStruct((n // LANES, s, LANES), dtype)


def _norm_proj_kernel(x_ref, g_ref, w_ref, o_ref, gate_ref, h_scr):
    @pl.when(pl.program_id(1) == 0)
    def _():
        h_scr[...] = _rms(x_ref[...], g_ref[...]).astype(BF16)

    acc = _dot(h_scr[...], w_ref[...])
    n_main = o_ref.shape[0] * LANES
    _store_col_blocks(o_ref, acc[:, :n_main])
    gate_ref[...] = jax.nn.sigmoid(acc[:, n_main:])


def _norm_proj_gated(x, g, w, *, widx=None, gate_cols=LANES):
    s, d = x.shape
    n_all = w.shape[-1]
    n = n_all - gate_cols
    tm, tn = _proj_tiles(s, d, n_all, 2 * d * 4 + d * 2 + n_all * 4 + 2 * n * 2 + 2 * gate_cols * 4)
    assert tn == n_all, "the gate split needs the whole weight in one column tile"
    out_spec, out_shape = _blocked_out(s, n, tm, n, BF16)
    return pl.pallas_call(
        _norm_proj_kernel,
        name="norm_proj",
        grid=(s // tm, 1),
        in_specs=[pl.BlockSpec((tm, d), lambda i, j: (i, 0)),
                  pl.BlockSpec((1, d), lambda i, j: (0, 0)),
                  _weight_spec(w, widx, d, tn)],
        out_specs=[out_spec, pl.BlockSpec((tm, gate_cols), lambda i, j: (i, 0))],
        out_shape=[out_shape, jax.ShapeDtypeStruct((s, gate_cols), F32)],
        scratch_shapes=[pltpu.VMEM((tm, d), BF16)],
        compiler_params=_params("parallel", "arbitrary"),
    )(x, g.reshape(1, d), w)


def _norm_proj_rope_kernel(x_ref, g_ref, w_ref, c_ref, sa_ref, sb_ref, *rest, shift, want_raw):
    if want_raw:
        raw_ref, rot_ref, h_scr = rest
    else:
        rot_ref, h_scr = rest

    @pl.when(pl.program_id(1) == 0)
    def _():
        h_scr[...] = _rms(x_ref[...], g_ref[...]).astype(BF16)

    acc = _dot(h_scr[...], w_ref[...])
    if want_raw:
        _store_col_blocks(raw_ref, acc)
    c, sa, sb = c_ref[...], sa_ref[...], sb_ref[...]
    for blk in range(acc.shape[1] // LANES):
        a = acc[:, blk * LANES:(blk + 1) * LANES]
        r = a * c + pltpu.roll(a, LANES - shift, 1) * sa + pltpu.roll(a, shift, 1) * sb
        rot_ref[blk] = r.astype(BF16)


def _norm_proj_rope(x, g, w, tables, *, shift, want_raw, widx=None):
    s, d = x.shape
    n = w.shape[-1]
    n_out = 2 if want_raw else 1
    tm, tn = _proj_tiles(s, d, n, 2 * d * 4 + d * 2 + n * 4 + n_out * 2 * n * 2 + 3 * 2 * LANES * 4)
    tab_spec = pl.BlockSpec((tm, LANES), lambda i, j: (i, 0))
    out_spec, out_sd = _blocked_out(s, n, tm, tn, BF16)
    return pl.pallas_call(
        functools.partial(_norm_proj_rope_kernel, shift=shift, want_raw=want_raw),
        name="norm_proj_rope",
        grid=(s // tm, n // tn),
        in_specs=[pl.BlockSpec((tm, d), lambda i, j: (i, 0)),
                  pl.BlockSpec((1, d), lambda i, j: (0, 0)),
                  _weight_spec(w, widx, d, tn),
                  tab_spec, tab_spec, tab_spec],
        out_specs=[out_spec, out_spec] if want_raw else out_spec,
        out_shape=[out_sd, out_sd] if want_raw else out_sd,
        scratch_shapes=[pltpu.VMEM((tm, d), BF16)],
        compiler_params=_params("parallel", "arbitrary"),
    )(x, g.reshape(1, d), w, *tables)


def _matmul_res_kernel(a_ref, w_ref, r_ref, o_ref):
    o_ref[...] = r_ref[...] + _dot(a_ref[...], w_ref[...])


def _matmul_res(a, w, res, *, widx=None):
    s, k = a.shape
    n = w.shape[-1]
    tm, tn = _proj_tiles(s, k, n, 2 * k * 2 + 2 * 2 * n * 4 + n * 4)
    return pl.pallas_call(
        _matmul_res_kernel,
        name="matmul_res",
        grid=(s // tm, n // tn),
        in_specs=[pl.BlockSpec((tm, k), lambda i, j: (i, 0)),
                  _weight_spec(w, widx, k, tn),
                  pl.BlockSpec((tm, tn), lambda i, j: (i, j))],
        out_specs=pl.BlockSpec((tm, tn), lambda i, j: (i, j)),
        out_shape=jax.ShapeDtypeStruct((s, n), F32),
        compiler_params=_params("parallel", "arbitrary"),
    )(a, w, res)


def _mlp_kernel(x_ref, g_ref, wu_ref, wd_ref, fg_ref, o_ref, h_scr, acc_scr, *, final_norm):
    j = pl.program_id(1)

    @pl.when(j == 0)
    def _():
        h_scr[...] = _rms(x_ref[...], g_ref[...]).astype(BF16)
        acc_scr[...] = jnp.zeros_like(acc_scr)

    u = _dot(h_scr[...], wu_ref[...])
    u = jnp.square(jnp.maximum(u, 0.0)).astype(BF16)
    acc_scr[...] += _dot(u, wd_ref[...])

    @pl.when(j == pl.num_programs(1) - 1)
    def _():
        y = x_ref[...] + acc_scr[...]
        if final_norm:
            y = _rms(y, fg_ref[...])
        o_ref[...] = y


def _mlp(x, g, w_up, w_down, layer, final_g, *, final_norm):
    s, d = x.shape
    ff = w_up.shape[-1]
    tm, tf = min(ROW_TILE, s), min(FF_TILE, ff)
    return pl.pallas_call(
        functools.partial(_mlp_kernel, final_norm=final_norm),
        name="sq_relu_mlp",
        grid=(s // tm, ff // tf),
        in_specs=[pl.BlockSpec((tm, d), lambda i, j: (i, 0)),
                  pl.BlockSpec((1, d), lambda i, j: (0, 0)),
                  pl.BlockSpec((None, d, tf), lambda i, j: (layer, 0, j)),
                  pl.BlockSpec((None, tf, d), lambda i, j: (layer, j, 0)),
                  pl.BlockSpec((1, d), lambda i, j: (0, 0))],
        out_specs=pl.BlockSpec((tm, d), lambda i, j: (i, 0)),
        out_shape=jax.ShapeDtypeStruct((s, d), F32),
        scratch_shapes=[pltpu.VMEM((tm, d), BF16), pltpu.VMEM((tm, d), F32)],
        compiler_params=_params("parallel", "arbitrary"),
    )(x, g.reshape(1, d), w_up, w_down, final_g.reshape(1, d))


def _compress_kernel(t_ref, pos_ref, w1_ref, w2_ref, o_ref):
    t = t_ref[0, 0].astype(F32)
    pos = pos_ref[0]
    half = t.shape[1]
    a = _dot((t + pos[0:1]).astype(BF16), w1_ref[0, :half, :])
    b = _dot((t + pos[1:2]).astype(BF16), w1_ref[0, half:, :])
    nc = t.shape[0]
    hid = jax.nn.gelu(a + pltpu.roll(b, nc - 1, 0))
    o_ref[0, 0] = _dot(hid.astype(BF16), w2_ref[0]).astype(BF16)


def _compress(t2, pos2, w1, w2):
    _, hk, nc, width = t2.shape
    dk = w2.shape[-1]
    return pl.pallas_call(
        _compress_kernel,
        name="nsa_compress",
        grid=(2, hk),
        in_specs=[pl.BlockSpec((1, 1, nc, width), lambda j, h: (j, h, 0, 0)),
                  pl.BlockSpec((1, 2, width), lambda j, h: (j, 0, 0)),
                  pl.BlockSpec((1, 2 * width, CMP_HIDDEN), lambda j, h: (j, 0, 0)),
                  pl.BlockSpec((1, CMP_HIDDEN, dk), lambda j, h: (j, 0, 0))],
        out_specs=pl.BlockSpec((1, 1, nc, dk), lambda j, h: (j, h, 0, 0)),
        out_shape=jax.ShapeDtypeStruct((2, hk, nc, dk), BF16),
        compiler_params=_params("parallel", "arbitrary"),
    )(t2, pos2, w1, w2)


def _nsa_attn_kernel(qr_ref, qo_ref, kc_ref, vc_ref, ks_ref, oh_ref, vs_ref, kw_ref, vw_ref, gt_ref,
                     o_ref, vst_scr, vwt_scr, rhs_scr, s_scr, m_scr, l_scr, acc_scr, *, seq):
    g_, tq, dk = NSA_GROUP, NSA_TQ, NSA_HEAD_DIM
    cw = NSA_CHAIN_LANES
    n_chain = g_ * tq // cw
    nc = seq // CMP_STRIDE
    nsel = seq // SEL_BLOCK
    ch = min(NSA_SEL_CHUNK, seq)
    wlen = WINDOW + tq
    qb = pl.program_id(1)
    start = qb * tq
    t_row = start + lax.broadcasted_iota(jnp.int32, (1, tq), 1)

    @pl.when(qb == 0)
    def _():
        for r in range(0, seq, ch):
            vst_scr[:, r:r + ch] = vs_ref[r:r + ch, :].T
            vwt_scr[:, r:r + ch] = vw_ref[r:r + ch, :].T

    def heads_on_lanes(ref):
        return jnp.concatenate([ref[g].T for g in range(g_)], axis=1)

    def tile_heads(x, n):
        return jnp.concatenate([x] * n, axis=1)

    def softmax_keys(s2):
        m = jnp.max(s2, axis=0, keepdims=True)
        m = jnp.where(m == -jnp.inf, 0.0, m)
        e = jnp.exp2(s2 - m)
        return e, 1.0 / jnp.maximum(jnp.sum(e, axis=0, keepdims=True), 1e-30)

    qr_t = heads_on_lanes(qr_ref)
    qo_t = heads_on_lanes(qo_ref)

    n_end = lax.broadcasted_iota(jnp.int32, (nc, tq), 0) * CMP_STRIDE + (CMP_BLOCK - 1)
    bias_c = jnp.where(n_end <= t_row, 0.0, -jnp.inf)
    e_c, r_c = softmax_keys(_dot(kc_ref[0, 0], qr_t) + tile_heads(bias_c, g_))
    p_c = e_c * r_c
    o_c = _dot(vc_ref[0, 0].T, p_c.astype(BF16))

    w0 = pl.multiple_of(jnp.maximum(start - WINDOW, 0), tq)
    kpos = w0 + lax.broadcasted_iota(jnp.int32, (wlen, tq), 0)
    bias_w = jnp.where((kpos <= t_row) & (kpos > t_row - WINDOW), 0.0, -jnp.inf)
    e_w, r_w = softmax_keys(_dot(kw_ref[pl.ds(w0, wlen), :], qo_t) + tile_heads(bias_w, g_))
    o_w = _dot(vwt_scr[:, pl.ds(w0, wlen)], e_w.astype(BF16)) * r_w

    p_sum = p_c[:, :tq]
    for g in range(1, g_):
        p_sum = p_sum + p_c[:, g * tq:(g + 1) * tq]
    n_i = lax.broadcasted_iota(jnp.int32, (nsel, nc), 1) * CMP_STRIDE
    s_i = lax.broadcasted_iota(jnp.int32, (nsel, nc), 0) * SEL_BLOCK
    overlap = jnp.where((n_i < s_i + SEL_BLOCK) & (n_i + CMP_BLOCK > s_i), 1.0, 0.0).astype(BF16)
    p_hi = p_sum.astype(BF16)
    p_r1 = p_sum - p_hi.astype(F32)
    p_mid = p_r1.astype(BF16)
    p_lo = (p_r1 - p_mid.astype(F32)).astype(BF16)
    imp = _dot(overlap, p_hi) + _dot(overlap, p_mid) + _dot(overlap, p_lo)

    ids = lax.broadcasted_iota(jnp.int32, (nsel, tq), 0)
    cur = t_row >> SEL_SHIFT
    forced = (ids == 0) | (ids == cur) | (ids == cur - 1)
    imp = jnp.where(forced, FORCE_SCORE, imp)
    work = jnp.where(ids * SEL_BLOCK <= t_row, imp, -1.0)

    sel = jnp.zeros((nsel, tq), F32)
    for _ in range(min(SEL_TOPK, nsel)):
        mx = jnp.max(work, axis=0, keepdims=True)
        first = jnp.min(jnp.where(work == mx, ids, nsel), axis=0, keepdims=True)
        pick = ids == first
        sel = jnp.where(pick, 1.0, sel)
        work = jnp.where(pick, -3.0e38, work)

    neg_sel = tile_heads(jnp.where(sel > 0.5, 0.0, NEG_BIG).astype(BF16), cw // tq)
    for j in range(n_chain):
        rhs_scr[j] = jnp.concatenate([qo_t[:, j * cw:(j + 1) * cw], neg_sel], axis=0)
    m_scr[...] = jnp.full(m_scr.shape, NEG_BIG, F32)
    l_scr[...] = jnp.zeros(l_scr.shape, F32)
    acc_scr[...] = jnp.zeros(acc_scr.shape, F32)

    def keys_with_block(off):
        return jnp.concatenate([ks_ref[pl.ds(off, ch), :], oh_ref[pl.ds(off, ch), :]], axis=1)

    def fold(off, buf, diagonal):
        off = pl.multiple_of(off, ch)
        v_t = vst_scr[:, pl.ds(off, ch)]
        state = [(m_scr[j], l_scr[j], acc_scr[j]) for j in range(n_chain)]
        if diagonal:
            key = off + lax.broadcasted_iota(jnp.int32, (ch, tq), 0)
            causal = tile_heads(jnp.where(key <= t_row, 0.0, NEG_BIG), cw // tq)
            todo = []
        else:
            k_next = keys_with_block(pl.multiple_of(off + ch, ch))
            todo = list(range(n_chain))
        ahead = 3
        for j in todo[:ahead]:
            s_scr[1 - buf, j] = _dot(k_next, rhs_scr[j])
        new_state = []
        for j in range(n_chain):
            s = s_scr[buf, j]
            if diagonal:
                s = s + causal
            m_old, l_old, acc_old = state[j]
            m_new = jnp.maximum(m_old, jnp.max(s, axis=0, keepdims=True))
            alpha = jnp.exp2(m_old - m_new)
            e = jnp.exp2(s - m_new)
            l_new = alpha * l_old + jnp.sum(e, axis=0, keepdims=True)
            acc_new = alpha * acc_old + _dot(v_t, e.astype(BF16))
            new_state.append((m_new, l_new, acc_new))
            for jn in todo[j + ahead:j + ahead + 1]:
                s_scr[1 - buf, jn] = _dot(k_next, rhs_scr[jn])
        for j in range(n_chain):
            m_scr[j], l_scr[j], acc_scr[j] = new_state[j]

    k_first = keys_with_block(0)
    for j in range(n_chain):
        s_scr[0, j] = _dot(k_first, rhs_scr[j])

    def pair_step(c, carry):
        off = pl.multiple_of(c * 2 * ch, 2 * ch)
        fold(off, 0, False)
        fold(off + ch, 1, False)
        return carry

    n_full = (start + tq - 1) // ch
    lax.fori_loop(0, n_full // 2, pair_step, 0)
    rest = pl.multiple_of((n_full // 2) * 2 * ch, 2 * ch)

    @pl.when(n_full % 2 == 1)
    def _():
        fold(rest, 0, False)
        fold(rest + ch, 1, True)

    @pl.when(n_full % 2 == 0)
    def _():
        fold(rest, 0, True)

    gt = gt_ref[0]
    for g in range(g_):
        j = g * tq // cw
        lanes = slice(g * tq - j * cw, g * tq - j * cw + tq)
        o_s = jnp.where(m_scr[j][:, lanes] > SCORE_FLOOR,
                        acc_scr[j][:, lanes] * (1.0 / jnp.maximum(l_scr[j][:, lanes], 1e-30)), 0.0)
        cols = slice(g * tq, (g + 1) * tq)
        o_t = (gt[3 * g:3 * g + 1] * o_c[:, cols] + gt[3 * g + 1:3 * g + 2] * o_s
               + gt[3 * g + 2:3 * g + 3] * o_w[:, cols])
        o_ref[:, g * dk:(g + 1) * dk] = o_t.T.astype(BF16)


def _nsa_attention(raw, rot, plain, cmp_kv, gates):
    s = raw.shape[1]
    hk, g_, dk, tq = NSA_KV_HEADS, NSA_GROUP, NSA_HEAD_DIM, NSA_TQ
    nc = s // CMP_STRIDE
    nsel = s // SEL_BLOCK
    one_hot = (jnp.arange(s)[:, None] >> SEL_SHIFT == jnp.arange(nsel)[None, :]).astype(BF16)
    gates_t = gates[:, :3 * NSA_HEADS].T.reshape(hk, 3 * g_, s)
    qspec = pl.BlockSpec((g_, tq, dk), lambda h, i: (h, i, 0))
    col = lambda base: pl.BlockSpec((None, s, dk), lambda h, i: (base + h, 0, 0))
    cmp_spec = lambda j: pl.BlockSpec((1, 1, nc, dk), lambda h, i: (j, h, 0, 0))
    cw = NSA_CHAIN_LANES
    assert cw % tq == 0 and (g_ * tq) % cw == 0 and NSA_SEL_CHUNK % tq == 0
    n_chain = g_ * tq // cw
    return pl.pallas_call(
        functools.partial(_nsa_attn_kernel, seq=s),
        name="nsa_attention",
        grid=(hk, s // tq),
        in_specs=[qspec, qspec, cmp_spec(0), cmp_spec(1),
                  col(NSA_HEADS),
                  pl.BlockSpec((s, nsel), lambda h, i: (0, 0)),
                  col(2 * hk),
                  col(NSA_HEADS + hk), col(3 * hk),
                  pl.BlockSpec((1, 3 * g_, tq), lambda h, i: (h, 0, i))],
        out_specs=pl.BlockSpec((tq, g_ * dk), lambda h, i: (i, h)),
        out_shape=jax.ShapeDtypeStruct((s, NSA_HEADS * dk), BF16),
        scratch_shapes=[pltpu.VMEM((dk, s), BF16), pltpu.VMEM((dk, s), BF16),
                        pltpu.VMEM((n_chain, dk + nsel, cw), BF16),
                        pltpu.VMEM((2, n_chain, min(NSA_SEL_CHUNK, s), cw), F32),
                        pltpu.VMEM((n_chain, 1, cw), F32),
                        pltpu.VMEM((n_chain, 1, cw), F32),
                        pltpu.VMEM((n_chain, dk, cw), F32)],
        compiler_params=_params("parallel", "arbitrary"),
    )(raw, rot, cmp_kv, cmp_kv, rot, one_hot, plain, rot, plain, gates_t)


def _diff_attn_kernel(q_ref, k_ref, v_ref, lam_ref, sg_ref, o_ref, qq_scr, vt_scr, s_scr, m_scr,
                      l_scr, acc_scr, *, lambda_init):
    g_, tq, d = DIFF_GROUP, DIFF_TQ, DIFF_HEAD_DIM
    n_chain = 2 * g_
    qb = pl.program_id(1)

    @pl.when(qb == 0)
    def _():
        rows = 2 * tq
        for r in range(0, v_ref.shape[0], rows):
            vt_scr[:, r:r + rows] = v_ref[r:r + rows, :].T

    first_half = lax.broadcasted_iota(jnp.int32, (2 * d, 1), 0) < d
    for g in range(g_):
        q_t = q_ref[g].T
        zero = jnp.zeros_like(q_t)
        qq_scr[2 * g] = jnp.where(first_half, q_t, zero)
        qq_scr[2 * g + 1] = jnp.where(first_half, zero, q_t)
    m_scr[...] = jnp.full(m_scr.shape, NEG_BIG, F32)
    l_scr[...] = jnp.zeros(l_scr.shape, F32)
    acc_scr[...] = jnp.zeros(acc_scr.shape, F32)

    ch = 2 * tq
    n_full = qb // 2

    def fold(off, buf, diagonal):
        off = pl.multiple_of(off, ch)
        v_t = vt_scr[:, pl.ds(off, ch)]
        state = [(m_scr[j], l_scr[j], acc_scr[j]) for j in range(n_chain)]
        if diagonal:
            key = lax.broadcasted_iota(jnp.int32, (ch, tq), 0) + (off - qb * tq)
            causal = key <= lax.broadcasted_iota(jnp.int32, (ch, tq), 1)
            todo = []
        else:
            k_next = k_ref[pl.ds(pl.multiple_of(off + ch, ch), ch), :]
            todo = list(range(n_chain))
        ahead = 3
        for j in todo[:ahead]:
            s_scr[1 - buf, j] = _dot(k_next, qq_scr[j])
        new_state = []
        for j in range(n_chain):
            s = s_scr[buf, j]
            if diagonal:
                s = jnp.where(causal, s, NEG_BIG)
            m_old, l_old, acc_old = state[j]
            m_new = jnp.maximum(m_old, jnp.max(s, axis=0, keepdims=True))
            alpha = jnp.exp2(m_old - m_new)
            p = jnp.exp2(s - m_new)
            l_new = alpha * l_old + jnp.sum(p, axis=0, keepdims=True)
            acc_new = alpha * acc_old + _dot(v_t, p.astype(BF16))
            new_state.append((m_new, l_new, acc_new))
            for jn in todo[j + ahead:j + ahead + 1]:
                s_scr[1 - buf, jn] = _dot(k_next, qq_scr[jn])
        for j in range(n_chain):
            m_scr[j], l_scr[j], acc_scr[j] = new_state[j]

    k_first = k_ref[:ch, :]
    for j in range(n_chain):
        s_scr[0, j] = _dot(k_first, qq_scr[j])

    def pair_step(c, carry):
        off = pl.multiple_of(c * 2 * ch, 2 * ch)
        fold(off, 0, False)
        fold(off + ch, 1, False)
        return carry

    lax.fori_loop(0, n_full // 2, pair_step, 0)
    rest = pl.multiple_of((n_full // 2) * 2 * ch, 2 * ch)

    @pl.when(n_full % 2 == 1)
    def _():
        fold(rest, 0, False)
        fold(rest + ch, 1, True)

    @pl.when(n_full % 2 == 0)
    def _():
        fold(rest, 0, True)

    lv = lam_ref[...]
    lam = (jnp.exp(jnp.sum(lv[0:1] * lv[1:2], axis=-1, keepdims=True))
           - jnp.exp(jnp.sum(lv[2:3] * lv[3:4], axis=-1, keepdims=True)) + lambda_init)
    sub_g = sg_ref[...]
    for g in range(g_):
        o1 = acc_scr[2 * g] / jnp.maximum(l_scr[2 * g], 1e-30)
        o2 = acc_scr[2 * g + 1] / jnp.maximum(l_scr[2 * g + 1], 1e-30)
        a = o1 - lam * o2
        ms = jnp.mean(a * a, axis=0, keepdims=True)
        a = a * lax.rsqrt(ms + NORM_EPS) * sub_g * (1.0 - lambda_init)
        o_ref[:, g * 2 * d:(g + 1) * 2 * d] = a.T.astype(BF16)


def _diff_attention(q_rot, k_rot, v, lam_vecs, subln_g, lambda_init):
    s = q_rot.shape[1]
    hk, g_, d, tq = DIFF_KV_HEADS, DIFF_GROUP, DIFF_HEAD_DIM, DIFF_TQ
    return pl.pallas_call(
        functools.partial(_diff_attn_kernel, lambda_init=lambda_init),
        name="diff_attention",
        grid=(hk, s // tq),
        in_specs=[pl.BlockSpec((g_, tq, 2 * d), lambda h, i: (h, i, 0)),
                  pl.BlockSpec((None, s, 2 * d), lambda h, i: (h, 0, 0)),
                  pl.BlockSpec((None, s, 2 * d), lambda h, i: (hk + h, 0, 0)),
                  pl.BlockSpec((4, d), lambda h, i: (0, 0)),
                  pl.BlockSpec((2 * d, 1), lambda h, i: (0, 0))],
        out_specs=pl.BlockSpec((tq, g_ * 2 * d), lambda h, i: (i, h)),
        out_shape=jax.ShapeDtypeStruct((s, DIFF_HEADS * 2 * d), BF16),
        scratch_shapes=[pltpu.VMEM((2 * g_, 2 * d, tq), BF16),
                        pltpu.VMEM((2 * d, s), BF16),
                        pltpu.VMEM((2, 2 * g_, 2 * tq, tq), F32),
                        pltpu.VMEM((2 * g_, 1, tq), F32),
                        pltpu.VMEM((2 * g_, 1, tq), F32),
                        pltpu.VMEM((2 * g_, 2 * d, tq), F32)],
        compiler_params=_params("parallel", "arbitrary"),
    )(q_rot, k_rot, v, lam_vecs, subln_g.reshape(2 * d, 1))


def _rope_tables(seq, head_dim, reps):
    rot = head_dim // ROPE_FRACTION
    half = rot // 2
    inv = 1.0 / (ROPE_THETA ** (jnp.arange(0, rot, 2, dtype=F32) / rot))
    ang = jnp.arange(seq, dtype=F32)[:, None] * inv[None, :]
    cos, sin = jnp.cos(ang), jnp.sin(ang)
    rest = head_dim - rot
    zeros_h = jnp.zeros((seq, half), F32)
    c = jnp.concatenate([cos, cos, jnp.ones((seq, rest), F32)], axis=-1)
    sa = jnp.concatenate([-sin, zeros_h, jnp.zeros((seq, rest), F32)], axis=-1)
    sb = jnp.concatenate([zeros_h, sin, jnp.zeros((seq, rest), F32)], axis=-1)
    return tuple(jnp.tile(t, (1, reps)) for t in (c, sa, sb)), half


def _nsa_weights(w_in):
    hk, dk = NSA_KV_HEADS, NSA_HEAD_DIM
    qw, kvw = NSA_HEADS * dk, hk * dk
    part = lambda i: w_in[:, :, qw + i * kvw:qw + (i + 1) * kvw]
    w_q = w_in[:, :, :qw] * (dk ** -0.5 * LOG2E)
    w_rope = jnp.concatenate([w_q, part(2), part(4)], axis=2).astype(BF16)
    n_gate = 3 * NSA_HEADS
    w_gate = jnp.pad(w_in[:, :, qw + 6 * kvw:], ((0, 0), (0, 0), (0, LANES - n_gate)))
    w_plain = jnp.concatenate([part(0), part(1), part(3), part(5), w_gate], axis=2).astype(BF16)
    return w_rope, w_plain


def _nsa_layer(x, norm_g, proj_w, cmp_pos, cmp_w1, cmp_w2, w_out, layer, tables, shift):
    s = x.shape[0]
    hk, dk = NSA_KV_HEADS, NSA_HEAD_DIM
    w_rope, w_plain = proj_w
    raw, rot = _norm_proj_rope(x, norm_g, w_rope, tables, widx=layer, shift=shift, want_raw=True)
    plain, gates = _norm_proj_gated(x, norm_g, w_plain, widx=layer)

    nc = s // CMP_STRIDE
    t2 = plain[:2 * hk].reshape(2, hk, nc, CMP_STRIDE * dk)
    cmp_kv = _compress(t2, cmp_pos.reshape(2, 2, CMP_STRIDE * dk), cmp_w1.astype(BF16),
                       cmp_w2.astype(BF16))
    o = _nsa_attention(raw, rot, plain, cmp_kv, gates)
    return _matmul_res(o, w_out, x, widx=layer)


def kernel(x, attn_norm_g, mlp_norm_g, final_norm_g, nsa_w_in, nsa_cmp_pos, nsa_cmp_w1, nsa_cmp_w2, nsa_w_out, kv_norm_g, kv_w_shared, diff_w_q, diff_lambda, diff_subln_g, diff_w_out, mlp_w_up, mlp_w_down):
    b, s, d = x.shape
    tables_a, shift_a = _rope_tables(s, NSA_HEAD_DIM, 1)
    tables_b, shift_b = _rope_tables(s, DIFF_HEAD_DIM, 2)
    w_up, w_down = mlp_w_up.astype(BF16), mlp_w_down.astype(BF16)
    nsa_out, diff_out = nsa_w_out.astype(BF16), diff_w_out.astype(BF16)
    diff_q = (diff_w_q * (DIFF_HEAD_DIM ** -0.5 * LOG2E)).astype(BF16)
    nsa_proj = _nsa_weights(nsa_w_in)
    outs = []
    for bi in range(b):
        xs = x[bi]
        k_sh = v_sh = None
        for layer in range(DEPTH):
            if layer < N_A_LAYERS:
                xs = _nsa_layer(xs, attn_norm_g[layer], nsa_proj, nsa_cmp_pos[layer],
                                nsa_cmp_w1[layer], nsa_cmp_w2[layer], nsa_out, layer,
                                tables_a, shift_a)
            else:
                j = layer - N_A_LAYERS
                if j == 0:
                    v_sh, k_sh = _norm_proj_rope(xs, kv_norm_g, kv_w_shared.astype(BF16), tables_b,
                                                 shift=shift_b, want_raw=True)
                lambda_init = 0.8 - 0.6 * math.exp(-0.3 * layer)
                q = _norm_proj_rope(xs, attn_norm_g[layer], diff_q, tables_b, widx=j,
                                    shift=shift_b, want_raw=False)
                o = _diff_attention(q, k_sh, v_sh, diff_lambda[j], diff_subln_g[j], lambda_init)
                xs = _matmul_res(o, diff_out, xs, widx=j)
            xs = _mlp(xs, mlp_norm_g[layer], w_up, w_down, layer, final_norm_g,
                      final_norm=(layer == DEPTH - 1))
        outs.append(xs[None])
    return outs[0] if b == 1 else jnp.concatenate(outs, axis=0)
```

```python
import functools
import math

import jax
import jax.numpy as jnp
from jax import lax
from jax.experimental import pallas as pl
from jax.experimental.pallas import tpu as pltpu

F32 = jnp.float32
BF16 = jnp.bfloat16

D_MODEL = 2048
DEPTH = 4
N_A_LAYERS = DEPTH // 2

NSA_HEADS = 16
NSA_KV_HEADS = 4
NSA_GROUP = NSA_HEADS // NSA_KV_HEADS
NSA_HEAD_DIM = D_MODEL // NSA_HEADS
CMP_BLOCK = 32
CMP_STRIDE = 16
CMP_HIDDEN = 4 * NSA_HEAD_DIM
SEL_BLOCK = 64
SEL_SHIFT = SEL_BLOCK.bit_length() - 1
SEL_TOPK = 16
WINDOW = 512
FORCE_SCORE = 1.0e4

DIFF_HEADS = 16
DIFF_KV_HEADS = 4
DIFF_GROUP = DIFF_HEADS // DIFF_KV_HEADS
DIFF_HEAD_DIM = D_MODEL // (2 * DIFF_HEADS)

D_FF = 4 * D_MODEL
ROPE_THETA = 500000.0
ROPE_FRACTION = 4
NORM_EPS = 1e-6

LANES = 128
VMEM_LIMIT = 48 * 1024 * 1024
VMEM_BUDGET = 40 * 1024 * 1024
NEG_BIG = -1.0e30
SCORE_FLOOR = -1.0e28
LOG2E = 1.4426950408889634

NSA_TQ = 256
NSA_SEL_CHUNK = 512
NSA_CHAIN_LANES = 256
DIFF_TQ = 256
ROW_TILE = 512
COL_TILE = 512
FF_TILE = 1024


def _params(*sem):
    return pltpu.CompilerParams(dimension_semantics=sem, vmem_limit_bytes=VMEM_LIMIT)


def _dot(a, b):
    return jnp.dot(a, b, preferred_element_type=F32)


def _rms(x, g):
    ms = jnp.mean(x * x, axis=-1, keepdims=True)
    return x * lax.rsqrt(ms + NORM_EPS) * g


def _proj_tiles(s, k, n, row_bytes):
    weight = 2 * k * n * 2
    for tm in (ROW_TILE, ROW_TILE // 2):
        if weight + tm * row_bytes <= VMEM_BUDGET:
            return min(tm, s), n
    return min(ROW_TILE, s), min(COL_TILE, n)


def _weight_spec(w, widx, k, tn):
    if w.ndim == 2:
        return pl.BlockSpec((k, tn), lambda i, j: (0, j))
    return pl.BlockSpec((None, k, tn), lambda i, j: (widx, 0, j))


def _store_col_blocks(o_ref, acc):
    for blk in range(acc.shape[1] // LANES):
        o_ref[blk] = acc[:, blk * LANES:(blk + 1) * LANES].astype(o_ref.dtype)


def _blocked_out(s, n, tm, tn, dtype):
    spec = pl.BlockSpec((tn // LANES, tm, LANES), lambda i, j: (j, i, 0))
    return spec, jax.ShapeDtypeStruct((n // LANES, s, LANES), dtype)


def _norm_proj_kernel(x_ref, g_ref, w_ref, o_ref, gate_ref, h_scr):
    @pl.when(pl.program_id(1) == 0)
    def _():
        h_scr[...] = _rms(x_ref[...], g_ref[...]).astype(BF16)

    acc = _dot(h_scr[...], w_ref[...])
    n_main = o_ref.shape[0] * LANES
    _store_col_blocks(o_ref, acc[:, :n_main])
    gate_ref[...] = jax.nn.sigmoid(acc[:, n_main:])


def _norm_proj_gated(x, g, w, *, widx=None, gate_cols=LANES):
    s, d = x.shape
    n_all = w.shape[-1]
    n = n_all - gate_cols
    tm, tn = _proj_tiles(s, d, n_all, 2 * d * 4 + d * 2 + n_all * 4 + 2 * n * 2 + 2 * gate_cols * 4)
    assert tn == n_all, "the gate split needs the whole weight in one column tile"
    out_spec, out_shape = _blocked_out(s, n, tm, n, BF16)
    return pl.pallas_call(
        _norm_proj_kernel,
        name="norm_proj",
        grid=(s // tm, 1),
        in_specs=[pl.BlockSpec((tm, d), lambda i, j: (i, 0)),
                  pl.BlockSpec((1, d), lambda i, j: (0, 0)),
                  _weight_spec(w, widx, d, tn)],
        out_specs=[out_spec, pl.BlockSpec((tm, gate_cols), lambda i, j: (i, 0))],
        out_shape=[out_shape, jax.ShapeDtypeStruct((s, gate_cols), F32)],
        scratch_shapes=[pltpu.VMEM((tm, d), BF16)],
        compiler_params=_params("parallel", "arbitrary"),
    )(x, g.reshape(1, d), w)


def _norm_proj_rope_kernel(x_ref, g_ref, w_ref, c_ref, sa_ref, sb_ref, *rest, shift, want_raw):
    if want_raw:
        raw_ref, rot_ref, h_scr = rest
    else:
        rot_ref, h_scr = rest

    @pl.when(pl.program_id(1) == 0)
    def _():
        h_scr[...] = _rms(x_ref[...], g_ref[...]).astype(BF16)

    acc = _dot(h_scr[...], w_ref[...])
    if want_raw:
        _store_col_blocks(raw_ref, acc)
    c, sa, sb = c_ref[...], sa_ref[...], sb_ref[...]
    for blk in range(acc.shape[1] // LANES):
        a = acc[:, blk * LANES:(blk + 1) * LANES]
        r = a * c + pltpu.roll(a, LANES - shift, 1) * sa + pltpu.roll(a, shift, 1) * sb
        rot_ref[blk] = r.astype(BF16)


def _norm_proj_rope(x, g, w, tables, *, shift, want_raw, widx=None):
    s, d = x.shape
    n = w.shape[-1]
    n_out = 2 if want_raw else 1
    tm, tn = _proj_tiles(s, d, n, 2 * d * 4 + d * 2 + n * 4 + n_out * 2 * n * 2 + 3 * 2 * LANES * 4)
    tab_spec = pl.BlockSpec((tm, LANES), lambda i, j: (i, 0))
    out_spec, out_sd = _blocked_out(s, n, tm, tn, BF16)
    return pl.pallas_call(
        functools.partial(_norm_proj_rope_kernel, shift=shift, want_raw=want_raw),
        name="norm_proj_rope",
        grid=(s // tm, n // tn),
        in_specs=[pl.BlockSpec((tm, d), lambda i, j: (i, 0)),
                  pl.BlockSpec((1, d), lambda i, j: (0, 0)),
                  _weight_spec(w, widx, d, tn),
                  tab_spec, tab_spec, tab_spec],
        out_specs=[out_spec, out_spec] if want_raw else out_spec,
        out_shape=[out_sd, out_sd] if want_raw else out_sd,
        scratch_shapes=[pltpu.VMEM((tm, d), BF16)],
        compiler_params=_params("parallel", "arbitrary"),
    )(x, g.reshape(1, d), w, *tables)


def _matmul_res_kernel(a_ref, w_ref, r_ref, o_ref):
    o_ref[...] = r_ref[...] + _dot(a_ref[...], w_ref[...])


def _matmul_res(a, w, res, *, widx=None):
    s, k = a.shape
    n = w.shape[-1]
    tm, tn = _proj_tiles(s, k, n, 2 * k * 2 + 2 * 2 * n * 4 + n * 4)
    return pl.pallas_call(
        _matmul_res_kernel,
        name="matmul_res",
        grid=(s // tm, n // tn),
        in_specs=[pl.BlockSpec((tm, k), lambda i, j: (i, 0)),
                  _weight_spec(w, widx, k, tn),
                  pl.BlockSpec((tm, tn), lambda i, j: (i, j))],
        out_specs=pl.BlockSpec((tm, tn), lambda i, j: (i, j)),
        out_shape=jax.ShapeDtypeStruct((s, n), F32),
        compiler_params=_params("parallel", "arbitrary"),
    )(a, w, res)


def _mlp_kernel(x_ref, g_ref, wu_ref, wd_ref, fg_ref, o_ref, h_scr, acc_scr, *, final_norm):
    j = pl.program_id(1)

    @pl.when(j == 0)
    def _():
        h_scr[...] = _rms(x_ref[...], g_ref[...]).astype(BF16)
        acc_scr[...] = jnp.zeros_like(acc_scr)

    u = _dot(h_scr[...], wu_ref[...])
    u = jnp.square(jnp.maximum(u, 0.0)).astype(BF16)
    acc_scr[...] += _dot(u, wd_ref[...])

    @pl.when(j == pl.num_programs(1) - 1)
    def _():
        y = x_ref[...] + acc_scr[...]
        if final_norm:
            y = _rms(y, fg_ref[...])
        o_ref[...] = y


def _mlp(x, g, w_up, w_down, layer, final_g, *, final_norm):
    s, d = x.shape
    ff = w_up.shape[-1]
    tm, tf = min(ROW_TILE, s), min(FF_TILE, ff)
    return pl.pallas_call(
        functools.partial(_mlp_kernel, final_norm=final_norm),
        name="sq_relu_mlp",
        grid=(s // tm, ff // tf),
        in_specs=[pl.BlockSpec((tm, d), lambda i, j: (i, 0)),
                  pl.BlockSpec((1, d), lambda i, j: (0, 0)),
                  pl.BlockSpec((None, d, tf), lambda i, j: (layer, 0, j)),
                  pl.BlockSpec((None, tf, d), lambda i, j: (layer, j, 0)),
                  pl.BlockSpec((1, d), lambda i, j: (0, 0))],
        out_specs=pl.BlockSpec((tm, d), lambda i, j: (i, 0)),
        out_shape=jax.ShapeDtypeStruct((s, d), F32),
        scratch_shapes=[pltpu.VMEM((tm, d), BF16), pltpu.VMEM((tm, d), F32)],
        compiler_params=_params("parallel", "arbitrary"),
    )(x, g.reshape(1, d), w_up, w_down, final_g.reshape(1, d))


def _compress_kernel(t_ref, pos_ref, w1_ref, w2_ref, o_ref):
    t = t_ref[0, 0].astype(F32)
    pos = pos_ref[0]
    half = t.shape[1]
    a = _dot((t + pos[0:1]).astype(BF16), w1_ref[0, :half, :])
    b = _dot((t + pos[1:2]).astype(BF16), w1_ref[0, half:, :])
    nc = t.shape[0]
    hid = jax.nn.gelu(a + pltpu.roll(b, nc - 1, 0))
    o_ref[0, 0] = _dot(hid.astype(BF16), w2_ref[0]).astype(BF16)


def _compress(t2, pos2, w1, w2):
    _, hk, nc, width = t2.shape
    dk = w2.shape[-1]
    return pl.pallas_call(
        _compress_kernel,
        name="nsa_compress",
        grid=(2, hk),
        in_specs=[pl.BlockSpec((1, 1, nc, width), lambda j, h: (j, h, 0, 0)),
                  pl.BlockSpec((1, 2, width), lambda j, h: (j, 0, 0)),
                  pl.BlockSpec((1, 2 * width, CMP_HIDDEN), lambda j, h: (j, 0, 0)),
                  pl.BlockSpec((1, CMP_HIDDEN, dk), lambda j, h: (j, 0, 0))],
        out_specs=pl.BlockSpec((1, 1, nc, dk), lambda j, h: (j, h, 0, 0)),
        out_shape=jax.ShapeDtypeStruct((2, hk, nc, dk), BF16),
        compiler_params=_params("parallel", "arbitrary"),
    )(t2, pos2, w1, w2)


def _nsa_attn_kernel(qr_ref, qo_ref, kc_ref, vc_ref, ks_ref, oh_ref, vs_ref, kw_ref, vw_ref, gt_ref,
                     wu_ref, wd_ref, o_ref, wu_out_ref, wd_out_ref,
                     vst_scr, vwt_scr, rhs_scr, s_scr, m_scr, l_scr, acc_scr, *, seq):
    g_, tq, dk = NSA_GROUP, NSA_TQ, NSA_HEAD_DIM
    cw = NSA_CHAIN_LANES
    n_chain = g_ * tq // cw
    nc = seq // CMP_STRIDE
    nsel = seq // SEL_BLOCK
    ch = min(NSA_SEL_CHUNK, seq)
    wlen = WINDOW + tq
    qb = pl.program_id(1)
    start = qb * tq
    t_row = start + lax.broadcasted_iota(jnp.int32, (1, tq), 1)

    wu_out_ref[...] = wu_ref[...].astype(BF16)
    wd_out_ref[...] = wd_ref[...].astype(BF16)

    @pl.when(qb == 0)
    def _():
        for r in range(0, seq, ch):
            vst_scr[:, r:r + ch] = vs_ref[r:r + ch, :].T
            vwt_scr[:, r:r + ch] = vw_ref[r:r + ch, :].T

    def heads_on_lanes(ref):
        return jnp.concatenate([ref[g].T for g in range(g_)], axis=1)

    def tile_heads(x, n):
        return jnp.concatenate([x] * n, axis=1)

    def softmax_keys(s2):
        m = jnp.max(s2, axis=0, keepdims=True)
        m = jnp.where(m == -jnp.inf, 0.0, m)
        e = jnp.exp2(s2 - m)
        return e, 1.0 / jnp.maximum(jnp.sum(e, axis=0, keepdims=True), 1e-30)

    qr_t = heads_on_lanes(qr_ref)
    qo_t = heads_on_lanes(qo_ref)

    n_end = lax.broadcasted_iota(jnp.int32, (nc, tq), 0) * CMP_STRIDE + (CMP_BLOCK - 1)
    bias_c = jnp.where(n_end <= t_row, 0.0, -jnp.inf)
    e_c, r_c = softmax_keys(_dot(kc_ref[0, 0], qr_t) + tile_heads(bias_c, g_))
    p_c = e_c * r_c
    o_c = _dot(vc_ref[0, 0].T, p_c.astype(BF16))

    w0 = pl.multiple_of(jnp.maximum(start - WINDOW, 0), tq)
    kpos = w0 + lax.broadcasted_iota(jnp.int32, (wlen, tq), 0)
    bias_w = jnp.where((kpos <= t_row) & (kpos > t_row - WINDOW), 0.0, -jnp.inf)
    e_w, r_w = softmax_keys(_dot(kw_ref[pl.ds(w0, wlen), :], qo_t) + tile_heads(bias_w, g_))
    o_w = _dot(vwt_scr[:, pl.ds(w0, wlen)], e_w.astype(BF16)) * r_w

    p_sum = p_c[:, :tq]
    for g in range(1, g_):
        p_sum = p_sum + p_c[:, g * tq:(g + 1) * tq]
    n_i = lax.broadcasted_iota(jnp.int32, (nsel, nc), 1) * CMP_STRIDE
    s_i = lax.broadcasted_iota(jnp.int32, (nsel, nc), 0) * SEL_BLOCK
    overlap = jnp.where((n_i < s_i + SEL_BLOCK) & (n_i + CMP_BLOCK > s_i), 1.0, 0.0).astype(BF16)
    p_hi = p_sum.astype(BF16)
    p_r1 = p_sum - p_hi.astype(F32)
    p_mid = p_r1.astype(BF16)
    p_lo = (p_r1 - p_mid.astype(F32)).astype(BF16)
    imp = _dot(overlap, p_hi) + _dot(overlap, p_mid) + _dot(overlap, p_lo)

    ids = lax.broadcasted_iota(jnp.int32, (nsel, tq), 0)
    cur = t_row >> SEL_SHIFT
    forced = (ids == 0) | (ids == cur) | (ids == cur - 1)
    imp = jnp.where(forced, FORCE_SCORE, imp)
    work = jnp.where(ids * SEL_BLOCK <= t_row, imp, -1.0)

    sel = jnp.zeros((nsel, tq), F32)
    for _ in range(min(SEL_TOPK, nsel)):
        mx = jnp.max(work, axis=0, keepdims=True)
        first = jnp.min(jnp.where(work == mx, ids, nsel), axis=0, keepdims=True)
        pick = ids == first
        sel = jnp.where(pick, 1.0, sel)
        work = jnp.where(pick, -3.0e38, work)

    neg_sel = tile_heads(jnp.where(sel > 0.5, 0.0, NEG_BIG).astype(BF16), cw // tq)
    for j in range(n_chain):
        rhs_scr[j] = jnp.concatenate([qo_t[:, j * cw:(j + 1) * cw], neg_sel], axis=0)
    m_scr[...] = jnp.full(m_scr.shape, NEG_BIG, F32)
    l_scr[...] = jnp.zeros(l_scr.shape, F32)
    acc_scr[...] = jnp.zeros(acc_scr.shape, F32)

    def keys_with_block(off):
        return jnp.concatenate([ks_ref[pl.ds(off, ch), :], oh_ref[pl.ds(off, ch), :]], axis=1)

    def fold(off, buf, diagonal):
        off = pl.multiple_of(off, ch)
        v_t = vst_scr[:, pl.ds(off, ch)]
        state = [(m_scr[j], l_scr[j], acc_scr[j]) for j in range(n_chain)]
        if diagonal:
            key = off + lax.broadcasted_iota(jnp.int32, (ch, tq), 0)
            causal = tile_heads(jnp.where(key <= t_row, 0.0, NEG_BIG), cw // tq)
            todo = []
        else:
            k_next = keys_with_block(pl.multiple_of(off + ch, ch))
            todo = list(range(n_chain))
        ahead = 3
        for j in todo[:ahead]:
            s_scr[1 - buf, j] = _dot(k_next, rhs_scr[j])
        new_state = []
        for j in range(n_chain):
            s = s_scr[buf, j]
            if diagonal:
                s = s + causal
            m_old, l_old, acc_old = state[j]
            m_new = jnp.maximum(m_old, jnp.max(s, axis=0, keepdims=True))
            alpha = jnp.exp2(m_old - m_new)
            e = jnp.exp2(s - m_new)
            l_new = alpha * l_old + jnp.sum(e, axis=0, keepdims=True)
            acc_new = alpha * acc_old + _dot(v_t, e.astype(BF16))
            new_state.append((m_new, l_new, acc_new))
            for jn in todo[j + ahead:j + ahead + 1]:
                s_scr[1 - buf, jn] = _dot(k_next, rhs_scr[jn])
        for j in range(n_chain):
            m_scr[j], l_scr[j], acc_scr[j] = new_state[j]

    k_first = keys_with_block(0)
    for j in range(n_chain):
        s_scr[0, j] = _dot(k_first, rhs_scr[j])

    def pair_step(c, carry):
        off = pl.multiple_of(c * 2 * ch, 2 * ch)
        fold(off, 0, False)
        fold(off + ch, 1, False)
        return carry

    n_full = (start + tq - 1) // ch
    lax.fori_loop(0, n_full // 2, pair_step, 0)
    rest = pl.multiple_of((n_full // 2) * 2 * ch, 2 * ch)

    @pl.when(n_full % 2 == 1)
    def _():
        fold(rest, 0, False)
        fold(rest + ch, 1, True)

    @pl.when(n_full % 2 == 0)
    def _():
        fold(rest, 0, True)

    gt = gt_ref[0]
    for g in range(g_):
        j = g * tq // cw
        lanes = slice(g * tq - j * cw, g * tq - j * cw + tq)
        o_s = jnp.where(m_scr[j][:, lanes] > SCORE_FLOOR,
                        acc_scr[j][:, lanes] * (1.0 / jnp.maximum(l_scr[j][:, lanes], 1e-30)), 0.0)
        cols = slice(g * tq, (g + 1) * tq)
        o_t = (gt[3 * g:3 * g + 1] * o_c[:, cols] + gt[3 * g + 1:3 * g + 2] * o_s
               + gt[3 * g + 2:3 * g + 3] * o_w[:, cols])
        o_ref[:, g * dk:(g + 1) * dk] = o_t.T.astype(BF16)


def _nsa_attention(raw, rot, plain, cmp_kv, gates, w_up, w_down, layer):
    s = raw.shape[1]
    hk, g_, dk, tq = NSA_KV_HEADS, NSA_GROUP, NSA_HEAD_DIM, NSA_TQ
    nc = s // CMP_STRIDE
    nsel = s // SEL_BLOCK
    one_hot = (jnp.arange(s)[:, None] >> SEL_SHIFT == jnp.arange(nsel)[None, :]).astype(BF16)
    gates_t = gates[:, :3 * NSA_HEADS].T.reshape(hk, 3 * g_, s)
    qspec = pl.BlockSpec((g_, tq, dk), lambda h, i: (h, i, 0))
    col = lambda base: pl.BlockSpec((None, s, dk), lambda h, i: (base + h, 0, 0))
    cmp_spec = lambda j: pl.BlockSpec((1, 1, nc, dk), lambda h, i: (j, h, 0, 0))
    cw = NSA_CHAIN_LANES
    assert cw % tq == 0 and (g_ * tq) % cw == 0 and NSA_SEL_CHUNK % tq == 0
    n_chain = g_ * tq // cw
    n_q = s // tq
    n_prog = hk * n_q
    per_call = w_up.shape[0] // N_A_LAYERS
    wu2 = w_up.reshape(-1, w_up.shape[-1])
    wd2 = w_down.reshape(-1, w_down.shape[-1])
    ru, rd = per_call * w_up.shape[1] // n_prog, per_call * w_down.shape[1] // n_prog
    cast_in = lambda rows, cols: pl.BlockSpec((rows, cols), lambda h, i: (layer * n_prog + h * n_q + i, 0))
    cast_out = lambda rows, cols: pl.BlockSpec((rows, cols), lambda h, i: (h * n_q + i, 0))
    o, wu_b, wd_b = pl.pallas_call(
        functools.partial(_nsa_attn_kernel, seq=s),
        name="nsa_attention",
        grid=(hk, n_q),
        in_specs=[qspec, qspec, cmp_spec(0), cmp_spec(1),
                  col(NSA_HEADS),
                  pl.BlockSpec((s, nsel), lambda h, i: (0, 0)),
                  col(2 * hk),
                  col(NSA_HEADS + hk), col(3 * hk),
                  pl.BlockSpec((1, 3 * g_, tq), lambda h, i: (h, 0, i)),
                  cast_in(ru, wu2.shape[1]), cast_in(rd, wd2.shape[1])],
        out_specs=[pl.BlockSpec((tq, g_ * dk), lambda h, i: (i, h)),
                   cast_out(ru, wu2.shape[1]), cast_out(rd, wd2.shape[1])],
        out_shape=[jax.ShapeDtypeStruct((s, NSA_HEADS * dk), BF16),
                   jax.ShapeDtypeStruct((ru * n_prog, wu2.shape[1]), BF16),
                   jax.ShapeDtypeStruct((rd * n_prog, wd2.shape[1]), BF16)],
        scratch_shapes=[pltpu.VMEM((dk, s), BF16), pltpu.VMEM((dk, s), BF16),
                        pltpu.VMEM((n_chain, dk + nsel, cw), BF16),
                        pltpu.VMEM((2, n_chain, min(NSA_SEL_CHUNK, s), cw), F32),
                        pltpu.VMEM((n_chain, 1, cw), F32),
                        pltpu.VMEM((n_chain, 1, cw), F32),
                        pltpu.VMEM((n_chain, dk, cw), F32)],
        compiler_params=_params("parallel", "arbitrary"),
    )(raw, rot, cmp_kv, cmp_kv, rot, one_hot, plain, rot, plain, gates_t, wu2, wd2)
    return o, wu_b.reshape(per_call, *w_up.shape[1:]), wd_b.reshape(per_call, *w_down.shape[1:])


def _diff_attn_kernel(q_ref, k_ref, v_ref, lam_ref, sg_ref, o_ref, qq_scr, vt_scr, s_scr, m_scr,
                      l_scr, acc_scr, *, lambda_init):
    g_, tq, d = DIFF_GROUP, DIFF_TQ, DIFF_HEAD_DIM
    n_chain = 2 * g_
    qb = pl.program_id(1)

    @pl.when(qb == 0)
    def _():
        rows = 2 * tq
        for r in range(0, v_ref.shape[0], rows):
            vt_scr[:, r:r + rows] = v_ref[r:r + rows, :].T

    first_half = lax.broadcasted_iota(jnp.int32, (2 * d, 1), 0) < d
    for g in range(g_):
        q_t = q_ref[g].T
        zero = jnp.zeros_like(q_t)
        qq_scr[2 * g] = jnp.where(first_half, q_t, zero)
        qq_scr[2 * g + 1] = jnp.where(first_half, zero, q_t)
    m_scr[...] = jnp.full(m_scr.shape, NEG_BIG, F32)
    l_scr[...] = jnp.zeros(l_scr.shape, F32)
    acc_scr[...] = jnp.zeros(acc_scr.shape, F32)

    ch = 2 * tq
    n_full = qb // 2

    def fold(off, buf, diagonal):
        off = pl.multiple_of(off, ch)
        v_t = vt_scr[:, pl.ds(off, ch)]
        state = [(m_scr[j], l_scr[j], acc_scr[j]) for j in range(n_chain)]
        if diagonal:
            key = lax.broadcasted_iota(jnp.int32, (ch, tq), 0) + (off - qb * tq)
            causal = key <= lax.broadcasted_iota(jnp.int32, (ch, tq), 1)
            todo = []
        else:
            k_next = k_ref[pl.ds(pl.multiple_of(off + ch, ch), ch), :]
            todo = list(range(n_chain))
        ahead = 3
        for j in todo[:ahead]:
            s_scr[1 - buf, j] = _dot(k_next, qq_scr[j])
        new_state = []
        for j in range(n_chain):
            s = s_scr[buf, j]
            if diagonal:
                s = jnp.where(causal, s, NEG_BIG)
            m_old, l_old, acc_old = state[j]
            m_new = jnp.maximum(m_old, jnp.max(s, axis=0, keepdims=True))
            alpha = jnp.exp2(m_old - m_new)
            p = jnp.exp2(s - m_new)
            l_new = alpha * l_old + jnp.sum(p, axis=0, keepdims=True)
            acc_new = alpha * acc_old + _dot(v_t, p.astype(BF16))
            new_state.append((m_new, l_new, acc_new))
            for jn in todo[j + ahead:j + ahead + 1]:
                s_scr[1 - buf, jn] = _dot(k_next, qq_scr[jn])
        for j in range(n_chain):
            m_scr[j], l_scr[j], acc_scr[j] = new_state[j]

    k_first = k_ref[:ch, :]
    for j in range(n_chain):
        s_scr[0, j] = _dot(k_first, qq_scr[j])

    def pair_step(c, carry):
        off = pl.multiple_of(c * 2 * ch, 2 * ch)
        fold(off, 0, False)
        fold(off + ch, 1, False)
        return carry

    lax.fori_loop(0, n_full // 2, pair_step, 0)
    rest = pl.multiple_of((n_full // 2) * 2 * ch, 2 * ch)

    @pl.when(n_full % 2 == 1)
    def _():
        fold(rest, 0, False)
        fold(rest + ch, 1, True)

    @pl.when(n_full % 2 == 0)
    def _():
        fold(rest, 0, True)

    lv = lam_ref[...]
    lam = (jnp.exp(jnp.sum(lv[0:1] * lv[1:2], axis=-1, keepdims=True))
           - jnp.exp(jnp.sum(lv[2:3] * lv[3:4], axis=-1, keepdims=True)) + lambda_init)
    sub_g = sg_ref[...]
    for g in range(g_):
        o1 = acc_scr[2 * g] / jnp.maximum(l_scr[2 * g], 1e-30)
        o2 = acc_scr[2 * g + 1] / jnp.maximum(l_scr[2 * g + 1], 1e-30)
        a = o1 - lam * o2
        ms = jnp.mean(a * a, axis=0, keepdims=True)
        a = a * lax.rsqrt(ms + NORM_EPS) * sub_g * (1.0 - lambda_init)
        o_ref[:, g * 2 * d:(g + 1) * 2 * d] = a.T.astype(BF16)


def _diff_attention(q_rot, k_rot, v, lam_vecs, subln_g, lambda_init):
    s = q_rot.shape[1]
    hk, g_, d, tq = DIFF_KV_HEADS, DIFF_GROUP, DIFF_HEAD_DIM, DIFF_TQ
    return pl.pallas_call(
        functools.partial(_diff_attn_kernel, lambda_init=lambda_init),
        name="diff_attention",
        grid=(hk, s // tq),
        in_specs=[pl.BlockSpec((g_, tq, 2 * d), lambda h, i: (h, i, 0)),
                  pl.BlockSpec((None, s, 2 * d), lambda h, i: (h, 0, 0)),
                  pl.BlockSpec((None, s, 2 * d), lambda h, i: (hk + h, 0, 0)),
                  pl.BlockSpec((4, d), lambda h, i: (0, 0)),
                  pl.BlockSpec((2 * d, 1), lambda h, i: (0, 0))],
        out_specs=pl.BlockSpec((tq, g_ * 2 * d), lambda h, i: (i, h)),
        out_shape=jax.ShapeDtypeStruct((s, DIFF_HEADS * 2 * d), BF16),
        scratch_shapes=[pltpu.VMEM((2 * g_, 2 * d, tq), BF16),
                        pltpu.VMEM((2 * d, s), BF16),
                        pltpu.VMEM((2, 2 * g_, 2 * tq, tq), F32),
                        pltpu.VMEM((2 * g_, 1, tq), F32),
                        pltpu.VMEM((2 * g_, 1, tq), F32),
                        pltpu.VMEM((2 * g_, 2 * d, tq), F32)],
        compiler_params=_params("parallel", "arbitrary"),
    )(q_rot, k_rot, v, lam_vecs, subln_g.reshape(2 * d, 1))


def _rope_tables(seq, head_dim, reps):
    rot = head_dim // ROPE_FRACTION
    half = rot // 2
    inv = 1.0 / (ROPE_THETA ** (jnp.arange(0, rot, 2, dtype=F32) / rot))
    ang = jnp.arange(seq, dtype=F32)[:, None] * inv[None, :]
    cos, sin = jnp.cos(ang), jnp.sin(ang)
    rest = head_dim - rot
    zeros_h = jnp.zeros((seq, half), F32)
    c = jnp.concatenate([cos, cos, jnp.ones((seq, rest), F32)], axis=-1)
    sa = jnp.concatenate([-sin, zeros_h, jnp.zeros((seq, rest), F32)], axis=-1)
    sb = jnp.concatenate([zeros_h, sin, jnp.zeros((seq, rest), F32)], axis=-1)
    return tuple(jnp.tile(t, (1, reps)) for t in (c, sa, sb)), half


def _nsa_weights(w_in):
    hk, dk = NSA_KV_HEADS, NSA_HEAD_DIM
    qw, kvw = NSA_HEADS * dk, hk * dk
    part = lambda i: w_in[:, :, qw + i * kvw:qw + (i + 1) * kvw]
    w_q = w_in[:, :, :qw] * (dk ** -0.5 * LOG2E)
    w_rope = jnp.concatenate([w_q, part(2), part(4)], axis=2).astype(BF16)
    n_gate = 3 * NSA_HEADS
    w_gate = jnp.pad(w_in[:, :, qw + 6 * kvw:], ((0, 0), (0, 0), (0, LANES - n_gate)))
    w_plain = jnp.concatenate([part(0), part(1), part(3), part(5), w_gate], axis=2).astype(BF16)
    return w_rope, w_plain


def _nsa_layer(x, norm_g, proj_w, cmp_pos, cmp_w1, cmp_w2, w_out, mlp_w, layer, tables, shift):
    s = x.shape[0]
    hk, dk = NSA_KV_HEADS, NSA_HEAD_DIM
    w_rope, w_plain = proj_w
    raw, rot = _norm_proj_rope(x, norm_g, w_rope, tables, widx=layer, shift=shift, want_raw=True)
    plain, gates = _norm_proj_gated(x, norm_g, w_plain, widx=layer)

    nc = s // CMP_STRIDE
    t2 = plain[:2 * hk].reshape(2, hk, nc, CMP_STRIDE * dk)
    cmp_kv = _compress(t2, cmp_pos.reshape(2, 2, CMP_STRIDE * dk), cmp_w1.astype(BF16),
                       cmp_w2.astype(BF16))
    o, w_up, w_down = _nsa_attention(raw, rot, plain, cmp_kv, gates, *mlp_w, layer)
    return _matmul_res(o, w_out, x, widx=layer), (w_up, w_down)


def kernel(x, attn_norm_g, mlp_norm_g, final_norm_g, nsa_w_in, nsa_cmp_pos, nsa_cmp_w1, nsa_cmp_w2, nsa_w_out, kv_norm_g, kv_w_shared, diff_w_q, diff_lambda, diff_subln_g, diff_w_out, mlp_w_up, mlp_w_down):
    b, s, d = x.shape
    tables_a, shift_a = _rope_tables(s, NSA_HEAD_DIM, 1)
    tables_b, shift_b = _rope_tables(s, DIFF_HEAD_DIM, 2)
    assert DEPTH == 2 * N_A_LAYERS
    nsa_out, diff_out = nsa_w_out.astype(BF16), diff_w_out.astype(BF16)
    diff_q = (diff_w_q * (DIFF_HEAD_DIM ** -0.5 * LOG2E)).astype(BF16)
    nsa_proj = _nsa_weights(nsa_w_in)
    outs = []
    for bi in range(b):
        xs = x[bi]
        k_sh = v_sh = None
        mixer_w = {}
        for layer in range(DEPTH):
            if layer < N_A_LAYERS:
                xs, mixer_w[layer] = _nsa_layer(xs, attn_norm_g[layer], nsa_proj, nsa_cmp_pos[layer],
                                                nsa_cmp_w1[layer], nsa_cmp_w2[layer], nsa_out,
                                                (mlp_w_up, mlp_w_down), layer, tables_a, shift_a)
            else:
                j = layer - N_A_LAYERS
                if j == 0:
                    v_sh, k_sh = _norm_proj_rope(xs, kv_norm_g, kv_w_shared.astype(BF16), tables_b,
                                                 shift=shift_b, want_raw=True)
                lambda_init = 0.8 - 0.6 * math.exp(-0.3 * layer)
                q = _norm_proj_rope(xs, attn_norm_g[layer], diff_q, tables_b, widx=j,
                                    shift=shift_b, want_raw=False)
                o = _diff_attention(q, k_sh, v_sh, diff_lambda[j], diff_subln_g[j], lambda_init)
                xs = _matmul_res(o, diff_out, xs, widx=j)
            w_up, w_down = mixer_w[layer // 2]
            xs = _mlp(xs, mlp_norm_g[layer], w_up, w_down, layer % 2, final_norm_g,
                      final_norm=(layer == DEPTH - 1))
        outs.append(xs[None])
    return outs[0] if b == 1 else jnp.concatenate(outs, axis=0)
```

```python
import functools
import math

import jax
import jax.numpy as jnp
from jax import lax
from jax.experimental import pallas as pl
from jax.experimental.pallas import tpu as pltpu

F32 = jnp.float32
BF16 = jnp.bfloat16

D_MODEL = 2048
DEPTH = 4
N_A_LAYERS = DEPTH // 2

NSA_HEADS = 16
NSA_KV_HEADS = 4
NSA_GROUP = NSA_HEADS // NSA_KV_HEADS
NSA_HEAD_DIM = D_MODEL // NSA_HEADS
CMP_BLOCK = 32
CMP_STRIDE = 16
CMP_HIDDEN = 4 * NSA_HEAD_DIM
SEL_BLOCK = 64
SEL_SHIFT = SEL_BLOCK.bit_length() - 1
SEL_TOPK = 16
WINDOW = 512
FORCE_SCORE = 1.0e4

DIFF_HEADS = 16
DIFF_KV_HEADS = 4
DIFF_GROUP = DIFF_HEADS // DIFF_KV_HEADS
DIFF_HEAD_DIM = D_MODEL // (2 * DIFF_HEADS)

D_FF = 4 * D_MODEL
ROPE_THETA = 500000.0
ROPE_FRACTION = 4
NORM_EPS = 1e-6

LANES = 128
VMEM_LIMIT = 48 * 1024 * 1024
VMEM_BUDGET = 40 * 1024 * 1024
NEG_BIG = -1.0e30
SCORE_FLOOR = -1.0e28
LOG2E = 1.4426950408889634

NSA_TQ = 256
NSA_SEL_CHUNK = 512
NSA_CHAIN_LANES = 256
DIFF_TQ = 256
ROW_TILE = 512
COL_TILE = 512
FF_TILE = 1024


def _params(*sem):
    return pltpu.CompilerParams(dimension_semantics=sem, vmem_limit_bytes=VMEM_LIMIT)


def _dot(a, b):
    return jnp.dot(a, b, preferred_element_type=F32)


def _rms(x, g):
    ms = jnp.mean(x * x, axis=-1, keepdims=True)
    return x * lax.rsqrt(ms + NORM_EPS) * g


def _proj_tiles(s, k, n, row_bytes):
    weight = 2 * k * n * 2
    for tm in (ROW_TILE, ROW_TILE // 2):
        if weight + tm * row_bytes <= VMEM_BUDGET:
            return min(tm, s), n
    return min(ROW_TILE, s), min(COL_TILE, n)


def _weight_spec(w, widx, k, tn):
    if w.ndim == 2:
        return pl.BlockSpec((k, tn), lambda i, j: (0, j))
    return pl.BlockSpec((None, k, tn), lambda i, j: (widx, 0, j))


def _normed(x_ref, g_ref, h_scr, col_tiles):
    if col_tiles == 1:
        return _rms(x_ref[...], g_ref[...]).astype(BF16)

    @pl.when(pl.program_id(1) == 0)
    def _():
        h_scr[...] = _rms(x_ref[...], g_ref[...]).astype(BF16)

    return h_scr[...]


def _store_col_blocks(o_ref, acc):
    for blk in range(acc.shape[1] // LANES):
        o_ref[blk] = acc[:, blk * LANES:(blk + 1) * LANES].astype(o_ref.dtype)


def _blocked_out(s, n, tm, tn, dtype):
    spec = pl.BlockSpec((tn // LANES, tm, LANES), lambda i, j: (j, i, 0))
    return spec, jax.ShapeDtypeStruct((n // LANES, s, LANES), dtype)


def _norm_proj_kernel(x_ref, g_ref, w_ref, o_ref, gate_ref):
    acc = _dot(_rms(x_ref[...], g_ref[...]).astype(BF16), w_ref[...])
    n_main = o_ref.shape[0] * LANES
    _store_col_blocks(o_ref, acc[:, :n_main])
    gate_ref[...] = jax.nn.sigmoid(acc[:, n_main:])


def _norm_proj_gated(x, g, w, *, widx=None, gate_cols=LANES):
    s, d = x.shape
    n_all = w.shape[-1]
    n = n_all - gate_cols
    tm, tn = _proj_tiles(s, d, n_all, 2 * d * 4 + d * 2 + n_all * 4 + 2 * n * 2 + 2 * gate_cols * 4)
    assert tn == n_all, "the gate split needs the whole weight in one column tile"
    out_spec, out_shape = _blocked_out(s, n, tm, n, BF16)
    return pl.pallas_call(
        _norm_proj_kernel,
        name="norm_proj",
        grid=(s // tm, 1),
        in_specs=[pl.BlockSpec((tm, d), lambda i, j: (i, 0)),
                  pl.BlockSpec((1, d), lambda i, j: (0, 0)),
                  _weight_spec(w, widx, d, tn)],
        out_specs=[out_spec, pl.BlockSpec((tm, gate_cols), lambda i, j: (i, 0))],
        out_shape=[out_shape, jax.ShapeDtypeStruct((s, gate_cols), F32)],
        compiler_params=_params("parallel", "arbitrary"),
    )(x, g.reshape(1, d), w)


def _norm_proj_rope_kernel(x_ref, g_ref, w_ref, c_ref, sa_ref, sb_ref, *rest, shift, want_raw,
                           col_tiles):
    if want_raw:
        raw_ref, rot_ref, h_scr = rest
    else:
        rot_ref, h_scr = rest

    acc = _dot(_normed(x_ref, g_ref, h_scr, col_tiles), w_ref[...])
    if want_raw:
        _store_col_blocks(raw_ref, acc)
    c, sa, sb = c_ref[...], sa_ref[...], sb_ref[...]
    for blk in range(acc.shape[1] // LANES):
        a = acc[:, blk * LANES:(blk + 1) * LANES]
        r = a * c + pltpu.roll(a, LANES - shift, 1) * sa + pltpu.roll(a, shift, 1) * sb
        rot_ref[blk] = r.astype(BF16)


def _norm_proj_rope(x, g, w, tables, *, shift, want_raw, widx=None):
    s, d = x.shape
    n = w.shape[-1]
    n_out = 2 if want_raw else 1
    tm, tn = _proj_tiles(s, d, n, 2 * d * 4 + d * 2 + n * 4 + n_out * 2 * n * 2 + 3 * 2 * LANES * 4)
    tab_spec = pl.BlockSpec((tm, LANES), lambda i, j: (i, 0))
    out_spec, out_sd = _blocked_out(s, n, tm, tn, BF16)
    return pl.pallas_call(
        functools.partial(_norm_proj_rope_kernel, shift=shift, want_raw=want_raw,
                          col_tiles=n // tn),
        name="norm_proj_rope",
        grid=(s // tm, n // tn),
        in_specs=[pl.BlockSpec((tm, d), lambda i, j: (i, 0)),
                  pl.BlockSpec((1, d), lambda i, j: (0, 0)),
                  _weight_spec(w, widx, d, tn),
                  tab_spec, tab_spec, tab_spec],
        out_specs=[out_spec, out_spec] if want_raw else out_spec,
        out_shape=[out_sd, out_sd] if want_raw else out_sd,
        scratch_shapes=[pltpu.VMEM((tm, d), BF16)],
        compiler_params=_params("parallel", "arbitrary"),
    )(x, g.reshape(1, d), w, *tables)


def _matmul_res_kernel(a_ref, w_ref, r_ref, o_ref):
    o_ref[...] = r_ref[...] + _dot(a_ref[...], w_ref[...])


def _matmul_res(a, w, res, *, widx=None):
    s, k = a.shape
    n = w.shape[-1]
    tm, tn = _proj_tiles(s, k, n, 2 * k * 2 + 2 * 2 * n * 4 + n * 4)
    return pl.pallas_call(
        _matmul_res_kernel,
        name="matmul_res",
        grid=(s // tm, n // tn),
        in_specs=[pl.BlockSpec((tm, k), lambda i, j: (i, 0)),
                  _weight_spec(w, widx, k, tn),
                  pl.BlockSpec((tm, tn), lambda i, j: (i, j))],
        out_specs=pl.BlockSpec((tm, tn), lambda i, j: (i, j)),
        out_shape=jax.ShapeDtypeStruct((s, n), F32),
        compiler_params=_params("parallel", "arbitrary"),
    )(a, w, res)


def _mlp_kernel(x_ref, g_ref, wu_ref, wd_ref, fg_ref, o_ref, h_scr, acc_scr, *, final_norm):
    j = pl.program_id(1)

    @pl.when(j == 0)
    def _():
        h_scr[...] = _rms(x_ref[...], g_ref[...]).astype(BF16)
        acc_scr[...] = jnp.zeros_like(acc_scr)

    u = _dot(h_scr[...], wu_ref[...])
    u = jnp.square(jnp.maximum(u, 0.0)).astype(BF16)
    acc_scr[...] += _dot(u, wd_ref[...])

    @pl.when(j == pl.num_programs(1) - 1)
    def _():
        y = x_ref[...] + acc_scr[...]
        if final_norm:
            y = _rms(y, fg_ref[...])
        o_ref[...] = y


def _mlp(x, g, w_up, w_down, layer, final_g, *, final_norm):
    s, d = x.shape
    ff = w_up.shape[-1]
    tm, tf = min(ROW_TILE, s), min(FF_TILE, ff)
    return pl.pallas_call(
        functools.partial(_mlp_kernel, final_norm=final_norm),
        name="sq_relu_mlp",
        grid=(s // tm, ff // tf),
        in_specs=[pl.BlockSpec((tm, d), lambda i, j: (i, 0)),
                  pl.BlockSpec((1, d), lambda i, j: (0, 0)),
                  pl.BlockSpec((None, d, tf), lambda i, j: (layer, 0, j)),
                  pl.BlockSpec((None, tf, d), lambda i, j: (layer, j, 0)),
                  pl.BlockSpec((1, d), lambda i, j: (0, 0))],
        out_specs=pl.BlockSpec((tm, d), lambda i, j: (i, 0)),
        out_shape=jax.ShapeDtypeStruct((s, d), F32),
        scratch_shapes=[pltpu.VMEM((tm, d), BF16), pltpu.VMEM((tm, d), F32)],
        compiler_params=_params("parallel", "arbitrary"),
    )(x, g.reshape(1, d), w_up, w_down, final_g.reshape(1, d))


def _compress_kernel(t_ref, pos_ref, w1_ref, w2_ref, o_ref):
    t = t_ref[0, 0].astype(F32)
    pos = pos_ref[0]
    half = t.shape[1]
    a = _dot((t + pos[0:1]).astype(BF16), w1_ref[0, :half, :])
    b = _dot((t + pos[1:2]).astype(BF16), w1_ref[0, half:, :])
    nc = t.shape[0]
    hid = jax.nn.gelu(a + pltpu.roll(b, nc - 1, 0))
    o_ref[0, 0] = _dot(hid.astype(BF16), w2_ref[0]).astype(BF16)


def _compress(t2, pos2, w1, w2):
    _, hk, nc, width = t2.shape
    dk = w2.shape[-1]
    return pl.pallas_call(
        _compress_kernel,
        name="nsa_compress",
        grid=(2, hk),
        in_specs=[pl.BlockSpec((1, 1, nc, width), lambda j, h: (j, h, 0, 0)),
                  pl.BlockSpec((1, 2, width), lambda j, h: (j, 0, 0)),
                  pl.BlockSpec((1, 2 * width, CMP_HIDDEN), lambda j, h: (j, 0, 0)),
                  pl.BlockSpec((1, CMP_HIDDEN, dk), lambda j, h: (j, 0, 0))],
        out_specs=pl.BlockSpec((1, 1, nc, dk), lambda j, h: (j, h, 0, 0)),
        out_shape=jax.ShapeDtypeStruct((2, hk, nc, dk), BF16),
        compiler_params=_params("parallel", "arbitrary"),
    )(t2, pos2, w1, w2)


def _nsa_attn_kernel(qr_ref, qo_ref, kc_ref, vc_ref, ks_ref, oh_ref, vs_ref, kw_ref, vw_ref, gt_ref,
                     wu_ref, wd_ref, o_ref, wu_out_ref, wd_out_ref,
                     vst_scr, vwt_scr, rhs_scr, s_scr, m_scr, l_scr, acc_scr, *, seq):
    g_, tq, dk = NSA_GROUP, NSA_TQ, NSA_HEAD_DIM
    cw = NSA_CHAIN_LANES
    n_chain = g_ * tq // cw
    nc = seq // CMP_STRIDE
    nsel = seq // SEL_BLOCK
    ch = min(NSA_SEL_CHUNK, seq)
    wlen = WINDOW + tq
    qb = pl.program_id(1)
    start = qb * tq
    t_row = start + lax.broadcasted_iota(jnp.int32, (1, tq), 1)

    wu_out_ref[...] = wu_ref[...].astype(BF16)
    wd_out_ref[...] = wd_ref[...].astype(BF16)

    @pl.when(qb == 0)
    def _():
        for r in range(0, seq, ch):
            vst_scr[:, r:r + ch] = vs_ref[r:r + ch, :].T
            vwt_scr[:, r:r + ch] = vw_ref[r:r + ch, :].T

    def heads_on_lanes(ref):
        return jnp.concatenate([ref[g].T for g in range(g_)], axis=1)

    def tile_heads(x, n):
        return jnp.concatenate([x] * n, axis=1)

    def softmax_keys(s2):
        m = jnp.max(s2, axis=0, keepdims=True)
        m = jnp.where(m == -jnp.inf, 0.0, m)
        e = jnp.exp2(s2 - m)
        return e, 1.0 / jnp.maximum(jnp.sum(e, axis=0, keepdims=True), 1e-30)

    qr_t = heads_on_lanes(qr_ref)
    qo_t = heads_on_lanes(qo_ref)

    n_end = lax.broadcasted_iota(jnp.int32, (nc, tq), 0) * CMP_STRIDE + (CMP_BLOCK - 1)
    bias_c = jnp.where(n_end <= t_row, 0.0, -jnp.inf)
    e_c, r_c = softmax_keys(_dot(kc_ref[0, 0], qr_t) + tile_heads(bias_c, g_))
    p_c = e_c * r_c
    o_c = _dot(vc_ref[0, 0].T, p_c.astype(BF16))

    w0 = pl.multiple_of(jnp.maximum(start - WINDOW, 0), tq)
    kpos = w0 + lax.broadcasted_iota(jnp.int32, (wlen, tq), 0)
    bias_w = jnp.where((kpos <= t_row) & (kpos > t_row - WINDOW), 0.0, -jnp.inf)
    e_w, r_w = softmax_keys(_dot(kw_ref[pl.ds(w0, wlen), :], qo_t) + tile_heads(bias_w, g_))
    o_w = _dot(vwt_scr[:, pl.ds(w0, wlen)], e_w.astype(BF16)) * r_w

    p_sum = p_c[:, :tq]
    for g in range(1, g_):
        p_sum = p_sum + p_c[:, g * tq:(g + 1) * tq]
    n_i = lax.broadcasted_iota(jnp.int32, (nsel, nc), 1) * CMP_STRIDE
    s_i = lax.broadcasted_iota(jnp.int32, (nsel, nc), 0) * SEL_BLOCK
    overlap = jnp.where((n_i < s_i + SEL_BLOCK) & (n_i + CMP_BLOCK > s_i), 1.0, 0.0).astype(BF16)
    p_hi = p_sum.astype(BF16)
    p_r1 = p_sum - p_hi.astype(F32)
    p_mid = p_r1.astype(BF16)
    p_lo = (p_r1 - p_mid.astype(F32)).astype(BF16)
    imp = _dot(overlap, p_hi) + _dot(overlap, p_mid) + _dot(overlap, p_lo)

    ids = lax.broadcasted_iota(jnp.int32, (nsel, tq), 0)
    cur = t_row >> SEL_SHIFT
    forced = (ids == 0) | (ids == cur) | (ids == cur - 1)
    imp = jnp.where(forced, FORCE_SCORE, imp)
    work = jnp.where(ids * SEL_BLOCK <= t_row, imp, -1.0)

    sel = jnp.zeros((nsel, tq), F32)
    for _ in range(min(SEL_TOPK, nsel)):
        mx = jnp.max(work, axis=0, keepdims=True)
        first = jnp.min(jnp.where(work == mx, ids, nsel), axis=0, keepdims=True)
        pick = ids == first
        sel = jnp.where(pick, 1.0, sel)
        work = jnp.where(pick, -3.0e38, work)

    neg_sel = tile_heads(jnp.where(sel > 0.5, 0.0, NEG_BIG).astype(BF16), cw // tq)
    for j in range(n_chain):
        rhs_scr[j] = jnp.concatenate([qo_t[:, j * cw:(j + 1) * cw], neg_sel], axis=0)
    m_scr[...] = jnp.full(m_scr.shape, NEG_BIG, F32)
    l_scr[...] = jnp.zeros(l_scr.shape, F32)
    acc_scr[...] = jnp.zeros(acc_scr.shape, F32)

    def keys_with_block(off):
        return jnp.concatenate([ks_ref[pl.ds(off, ch), :], oh_ref[pl.ds(off, ch), :]], axis=1)

    def fold(off, buf, diagonal):
        off = pl.multiple_of(off, ch)
        v_t = vst_scr[:, pl.ds(off, ch)]
        state = [(m_scr[j], l_scr[j], acc_scr[j]) for j in range(n_chain)]
        if diagonal:
            key = off + lax.broadcasted_iota(jnp.int32, (ch, tq), 0)
            causal = tile_heads(jnp.where(key <= t_row, 0.0, NEG_BIG), cw // tq)
            todo = []
        else:
            k_next = keys_with_block(pl.multiple_of(off + ch, ch))
            todo = list(range(n_chain))
        ahead = 3
        for j in todo[:ahead]:
            s_scr[1 - buf, j] = _dot(k_next, rhs_scr[j])
        new_state = []
        for j in range(n_chain):
            s = s_scr[buf, j]
            if diagonal:
                s = s + causal
            m_old, l_old, acc_old = state[j]
            m_new = jnp.maximum(m_old, jnp.max(s, axis=0, keepdims=True))
            alpha = jnp.exp2(m_old - m_new)
            e = jnp.exp2(s - m_new)
            l_new = alpha * l_old + jnp.sum(e, axis=0, keepdims=True)
            acc_new = alpha * acc_old + _dot(v_t, e.astype(BF16))
            new_state.append((m_new, l_new, acc_new))
            for jn in todo[j + ahead:j + ahead + 1]:
                s_scr[1 - buf, jn] = _dot(k_next, rhs_scr[jn])
        for j in range(n_chain):
            m_scr[j], l_scr[j], acc_scr[j] = new_state[j]

    k_first = keys_with_block(0)
    for j in range(n_chain):
        s_scr[0, j] = _dot(k_first, rhs_scr[j])

    def pair_step(c, carry):
        off = pl.multiple_of(c * 2 * ch, 2 * ch)
        fold(off, 0, False)
        fold(off + ch, 1, False)
        return carry

    n_full = (start + tq - 1) // ch
    lax.fori_loop(0, n_full // 2, pair_step, 0)
    rest = pl.multiple_of((n_full // 2) * 2 * ch, 2 * ch)

    @pl.when(n_full % 2 == 1)
    def _():
        fold(rest, 0, False)
        fold(rest + ch, 1, True)

    @pl.when(n_full % 2 == 0)
    def _():
        fold(rest, 0, True)

    gt = gt_ref[0]
    for g in range(g_):
        j = g * tq // cw
        lanes = slice(g * tq - j * cw, g * tq - j * cw + tq)
        o_s = jnp.where(m_scr[j][:, lanes] > SCORE_FLOOR,
                        acc_scr[j][:, lanes] * (1.0 / jnp.maximum(l_scr[j][:, lanes], 1e-30)), 0.0)
        cols = slice(g * tq, (g + 1) * tq)
        o_t = (gt[3 * g:3 * g + 1] * o_c[:, cols] + gt[3 * g + 1:3 * g + 2] * o_s
               + gt[3 * g + 2:3 * g + 3] * o_w[:, cols])
        o_ref[:, g * dk:(g + 1) * dk] = o_t.T.astype(BF16)


def _nsa_attention(raw, rot, plain, cmp_kv, gates, w_up, w_down, layer):
    s = raw.shape[1]
    hk, g_, dk, tq = NSA_KV_HEADS, NSA_GROUP, NSA_HEAD_DIM, NSA_TQ
    nc = s // CMP_STRIDE
    nsel = s // SEL_BLOCK
    one_hot = (jnp.arange(s)[:, None] >> SEL_SHIFT == jnp.arange(nsel)[None, :]).astype(BF16)
    gates_t = gates[:, :3 * NSA_HEADS].T.reshape(hk, 3 * g_, s)
    qspec = pl.BlockSpec((g_, tq, dk), lambda h, i: (h, i, 0))
    col = lambda base: pl.BlockSpec((None, s, dk), lambda h, i: (base + h, 0, 0))
    cmp_spec = lambda j: pl.BlockSpec((1, 1, nc, dk), lambda h, i: (j, h, 0, 0))
    cw = NSA_CHAIN_LANES
    assert cw % tq == 0 and (g_ * tq) % cw == 0 and NSA_SEL_CHUNK % tq == 0
    n_chain = g_ * tq // cw
    n_q = s // tq
    n_prog = hk * n_q
    per_call = w_up.shape[0] // N_A_LAYERS
    wu2 = w_up.reshape(-1, w_up.shape[-1])
    wd2 = w_down.reshape(-1, w_down.shape[-1])
    ru, rd = per_call * w_up.shape[1] // n_prog, per_call * w_down.shape[1] // n_prog
    cast_in = lambda rows, cols: pl.BlockSpec((rows, cols), lambda h, i: (layer * n_prog + h * n_q + i, 0))
    cast_out = lambda rows, cols: pl.BlockSpec((rows, cols), lambda h, i: (h * n_q + i, 0))
    o, wu_b, wd_b = pl.pallas_call(
        functools.partial(_nsa_attn_kernel, seq=s),
        name="nsa_attention",
        grid=(hk, n_q),
        in_specs=[qspec, qspec, cmp_spec(0), cmp_spec(1),
                  col(NSA_HEADS),
                  pl.BlockSpec((s, nsel), lambda h, i: (0, 0)),
                  col(2 * hk),
                  col(NSA_HEADS + hk), col(3 * hk),
                  pl.BlockSpec((1, 3 * g_, tq), lambda h, i: (h, 0, i)),
                  cast_in(ru, wu2.shape[1]), cast_in(rd, wd2.shape[1])],
        out_specs=[pl.BlockSpec((tq, g_ * dk), lambda h, i: (i, h)),
                   cast_out(ru, wu2.shape[1]), cast_out(rd, wd2.shape[1])],
        out_shape=[jax.ShapeDtypeStruct((s, NSA_HEADS * dk), BF16),
                   jax.ShapeDtypeStruct((ru * n_prog, wu2.shape[1]), BF16),
                   jax.ShapeDtypeStruct((rd * n_prog, wd2.shape[1]), BF16)],
        scratch_shapes=[pltpu.VMEM((dk, s), BF16), pltpu.VMEM((dk, s), BF16),
                        pltpu.VMEM((n_chain, dk + nsel, cw), BF16),
                        pltpu.VMEM((2, n_chain, min(NSA_SEL_CHUNK, s), cw), F32),
                        pltpu.VMEM((n_chain, 1, cw), F32),
                        pltpu.VMEM((n_chain, 1, cw), F32),
                        pltpu.VMEM((n_chain, dk, cw), F32)],
        compiler_params=_params("parallel", "arbitrary"),
    )(raw, rot, cmp_kv, cmp_kv, rot, one_hot, plain, rot, plain, gates_t, wu2, wd2)
    return o, wu_b.reshape(per_call, *w_up.shape[1:]), wd_b.reshape(per_call, *w_down.shape[1:])


def _diff_attn_kernel(q_ref, k_ref, v_ref, lam_ref, sg_ref, o_ref, qq_scr, vt_scr, s_scr, m_scr,
                      l_scr, acc_scr, *, lambda_init):
    g_, tq, d = DIFF_GROUP, DIFF_TQ, DIFF_HEAD_DIM
    n_chain = 2 * g_
    qb = pl.program_id(1)

    @pl.when(qb == 0)
    def _():
        rows = 2 * tq
        for r in range(0, v_ref.shape[0], rows):
            vt_scr[:, r:r + rows] = v_ref[r:r + rows, :].T

    first_half = lax.broadcasted_iota(jnp.int32, (2 * d, 1), 0) < d
    for g in range(g_):
        q_t = q_ref[g].T
        zero = jnp.zeros_like(q_t)
        qq_scr[2 * g] = jnp.where(first_half, q_t, zero)
        qq_scr[2 * g + 1] = jnp.where(first_half, zero, q_t)
    m_scr[...] = jnp.full(m_scr.shape, NEG_BIG, F32)
    l_scr[...] = jnp.zeros(l_scr.shape, F32)
    acc_scr[...] = jnp.zeros(acc_scr.shape, F32)

    ch = 2 * tq
    n_full = qb // 2

    def fold(off, buf, diag_width=None):
        diagonal = diag_width is not None
        width = diag_width if diagonal else ch
        off = pl.multiple_of(off, ch)
        v_t = vt_scr[:, pl.ds(off, width)]
        state = [(m_scr[j], l_scr[j], acc_scr[j]) for j in range(n_chain)]
        if diagonal:
            key = lax.broadcasted_iota(jnp.int32, (width, tq), 0) + (off - qb * tq)
            causal = key <= lax.broadcasted_iota(jnp.int32, (width, tq), 1)
            todo = []
        else:
            k_next = k_ref[pl.ds(pl.multiple_of(off + ch, ch), ch), :]
            todo = list(range(n_chain))
        ahead = 3
        for j in todo[:ahead]:
            s_scr[1 - buf, j] = _dot(k_next, qq_scr[j])
        new_state = []
        for j in range(n_chain):
            s = s_scr[buf, j, :width]
            if diagonal:
                s = jnp.where(causal, s, NEG_BIG)
            m_old, l_old, acc_old = state[j]
            m_new = jnp.maximum(m_old, jnp.max(s, axis=0, keepdims=True))
            alpha = jnp.exp2(m_old - m_new)
            p = jnp.exp2(s - m_new)
            l_new = alpha * l_old + jnp.sum(p, axis=0, keepdims=True)
            acc_new = alpha * acc_old + _dot(v_t, p.astype(BF16))
            new_state.append((m_new, l_new, acc_new))
            for jn in todo[j + ahead:j + ahead + 1]:
                s_scr[1 - buf, jn] = _dot(k_next, qq_scr[jn])
        for j in range(n_chain):
            m_scr[j], l_scr[j], acc_scr[j] = new_state[j]

    k_first = k_ref[:ch, :]
    for j in range(n_chain):
        s_scr[0, j] = _dot(k_first, qq_scr[j])

    def pair_step(c, carry):
        off = pl.multiple_of(c * 2 * ch, 2 * ch)
        fold(off, 0)
        fold(off + ch, 1)
        return carry

    lax.fori_loop(0, n_full // 2, pair_step, 0)
    rest = pl.multiple_of((n_full // 2) * 2 * ch, 2 * ch)

    for rem in range(4):
        @pl.when(qb % 4 == rem)
        def _(rem=rem):
            diag_width = ch if rem % 2 == 1 else tq
            if rem // 2 == 1:
                fold(rest, 0)
                fold(rest + ch, 1, diag_width)
            else:
                fold(rest, 0, diag_width)

    lv = lam_ref[...]
    lam = (jnp.exp(jnp.sum(lv[0:1] * lv[1:2], axis=-1, keepdims=True))
           - jnp.exp(jnp.sum(lv[2:3] * lv[3:4], axis=-1, keepdims=True)) + lambda_init)
    sub_g = sg_ref[...]
    for g in range(g_):
        o1 = acc_scr[2 * g] / jnp.maximum(l_scr[2 * g], 1e-30)
        o2 = acc_scr[2 * g + 1] / jnp.maximum(l_scr[2 * g + 1], 1e-30)
        a = o1 - lam * o2
        ms = jnp.mean(a * a, axis=0, keepdims=True)
        a = a * lax.rsqrt(ms + NORM_EPS) * sub_g * (1.0 - lambda_init)
        o_ref[:, g * 2 * d:(g + 1) * 2 * d] = a.T.astype(BF16)


def _diff_attention(q_rot, k_rot, v, lam_vecs, subln_g, lambda_init):
    s = q_rot.shape[1]
    hk, g_, d, tq = DIFF_KV_HEADS, DIFF_GROUP, DIFF_HEAD_DIM, DIFF_TQ
    return pl.pallas_call(
        functools.partial(_diff_attn_kernel, lambda_init=lambda_init),
        name="diff_attention",
        grid=(hk, s // tq),
        in_specs=[pl.BlockSpec((g_, tq, 2 * d), lambda h, i: (h, i, 0)),
                  pl.BlockSpec((None, s, 2 * d), lambda h, i: (h, 0, 0)),
                  pl.BlockSpec((None, s, 2 * d), lambda h, i: (hk + h, 0, 0)),
                  pl.BlockSpec((4, d), lambda h, i: (0, 0)),
                  pl.BlockSpec((2 * d, 1), lambda h, i: (0, 0))],
        out_specs=pl.BlockSpec((tq, g_ * 2 * d), lambda h, i: (i, h)),
        out_shape=jax.ShapeDtypeStruct((s, DIFF_HEADS * 2 * d), BF16),
        scratch_shapes=[pltpu.VMEM((2 * g_, 2 * d, tq), BF16),
                        pltpu.VMEM((2 * d, s), BF16),
                        pltpu.VMEM((2, 2 * g_, 2 * tq, tq), F32),
                        pltpu.VMEM((2 * g_, 1, tq), F32),
                        pltpu.VMEM((2 * g_, 1, tq), F32),
                        pltpu.VMEM((2 * g_, 2 * d, tq), F32)],
        compiler_params=_params("parallel", "arbitrary"),
    )(q_rot, k_rot, v, lam_vecs, subln_g.reshape(2 * d, 1))


def _rope_tables(seq, head_dim, reps):
    rot = head_dim // ROPE_FRACTION
    half = rot // 2
    inv = 1.0 / (ROPE_THETA ** (jnp.arange(0, rot, 2, dtype=F32) / rot))
    ang = jnp.arange(seq, dtype=F32)[:, None] * inv[None, :]
    cos, sin = jnp.cos(ang), jnp.sin(ang)
    rest = head_dim - rot
    zeros_h = jnp.zeros((seq, half), F32)
    c = jnp.concatenate([cos, cos, jnp.ones((seq, rest), F32)], axis=-1)
    sa = jnp.concatenate([-sin, zeros_h, jnp.zeros((seq, rest), F32)], axis=-1)
    sb = jnp.concatenate([zeros_h, sin, jnp.zeros((seq, rest), F32)], axis=-1)
    return tuple(jnp.tile(t, (1, reps)) for t in (c, sa, sb)), half


def _nsa_weights(w_in):
    hk, dk = NSA_KV_HEADS, NSA_HEAD_DIM
    qw, kvw = NSA_HEADS * dk, hk * dk
    part = lambda i: w_in[:, :, qw + i * kvw:qw + (i + 1) * kvw]
    w_q = w_in[:, :, :qw] * (dk ** -0.5 * LOG2E)
    w_rope = jnp.concatenate([w_q, part(2), part(4)], axis=2).astype(BF16)
    n_gate = 3 * NSA_HEADS
    w_gate = jnp.pad(w_in[:, :, qw + 6 * kvw:], ((0, 0), (0, 0), (0, LANES - n_gate)))
    w_plain = jnp.concatenate([part(0), part(1), part(3), part(5), w_gate], axis=2).astype(BF16)
    return w_rope, w_plain


def _nsa_layer(x, norm_g, proj_w, cmp_pos, cmp_w1, cmp_w2, w_out, mlp_w, layer, tables, shift):
    s = x.shape[0]
    hk, dk = NSA_KV_HEADS, NSA_HEAD_DIM
    w_rope, w_plain = proj_w
    raw, rot = _norm_proj_rope(x, norm_g, w_rope, tables, widx=layer, shift=shift, want_raw=True)
    plain, gates = _norm_proj_gated(x, norm_g, w_plain, widx=layer)

    nc = s // CMP_STRIDE
    t2 = plain[:2 * hk].reshape(2, hk, nc, CMP_STRIDE * dk)
    cmp_kv = _compress(t2, cmp_pos.reshape(2, 2, CMP_STRIDE * dk), cmp_w1.astype(BF16),
                       cmp_w2.astype(BF16))
    o, w_up, w_down = _nsa_attention(raw, rot, plain, cmp_kv, gates, *mlp_w, layer)
    return _matmul_res(o, w_out, x, widx=layer), (w_up, w_down)


def kernel(x, attn_norm_g, mlp_norm_g, final_norm_g, nsa_w_in, nsa_cmp_pos, nsa_cmp_w1, nsa_cmp_w2, nsa_w_out, kv_norm_g, kv_w_shared, diff_w_q, diff_lambda, diff_subln_g, diff_w_out, mlp_w_up, mlp_w_down):
    b, s, d = x.shape
    tables_a, shift_a = _rope_tables(s, NSA_HEAD_DIM, 1)
    tables_b, shift_b = _rope_tables(s, DIFF_HEAD_DIM, 2)
    assert DEPTH == 2 * N_A_LAYERS
    nsa_out, diff_out = nsa_w_out.astype(BF16), diff_w_out.astype(BF16)
    diff_q = (diff_w_q * (DIFF_HEAD_DIM ** -0.5 * LOG2E)).astype(BF16)
    nsa_proj = _nsa_weights(nsa_w_in)
    outs = []
    for bi in range(b):
        xs = x[bi]
        k_sh = v_sh = None
        mixer_w = {}
        for layer in range(DEPTH):
            if layer < N_A_LAYERS:
                xs, mixer_w[layer] = _nsa_layer(xs, attn_norm_g[layer], nsa_proj, nsa_cmp_pos[layer],
                                                nsa_cmp_w1[layer], nsa_cmp_w2[layer], nsa_out,
                                                (mlp_w_up, mlp_w_down), layer, tables_a, shift_a)
            else:
                j = layer - N_A_LAYERS
                if j == 0:
                    v_sh, k_sh = _norm_proj_rope(xs, kv_norm_g, kv_w_shared.astype(BF16), tables_b,
                                                 shift=shift_b, want_raw=True)
                lambda_init = 0.8 - 0.6 * math.exp(-0.3 * layer)
                q = _norm_proj_rope(xs, attn_norm_g[layer], diff_q, tables_b, widx=j,
                                    shift=shift_b, want_raw=False)
                o = _diff_attention(q, k_sh, v_sh, diff_lambda[j], diff_subln_g[j], lambda_init)
                xs = _matmul_res(o, diff_out, xs, widx=j)
            w_up, w_down = mixer_w[layer // 2]
            xs = _mlp(xs, mlp_norm_g[layer], w_up, w_down, layer % 2, final_norm_g,
                      final_norm=(layer == DEPTH - 1))
        outs.append(xs[None])
    return outs[0] if b == 1 else jnp.concatenate(outs, axis=0)
```

```python
import functools
import math

import jax
import jax.numpy as jnp
from jax import lax
from jax.experimental import pallas as pl
from jax.experimental.pallas import tpu as pltpu

F32 = jnp.float32
BF16 = jnp.bfloat16

D_MODEL = 2048
DEPTH = 4
N_A_LAYERS = DEPTH // 2

NSA_HEADS = 16
NSA_KV_HEADS = 4
NSA_GROUP = NSA_HEADS // NSA_KV_HEADS
NSA_HEAD_DIM = D_MODEL // NSA_HEADS
CMP_BLOCK = 32
CMP_STRIDE = 16
CMP_HIDDEN = 4 * NSA_HEAD_DIM
SEL_BLOCK = 64
SEL_SHIFT = SEL_BLOCK.bit_length() - 1
SEL_TOPK = 16
WINDOW = 512
FORCE_SCORE = 1.0e4

DIFF_HEADS = 16
DIFF_KV_HEADS = 4
DIFF_GROUP = DIFF_HEADS // DIFF_KV_HEADS
DIFF_HEAD_DIM = D_MODEL // (2 * DIFF_HEADS)

D_FF = 4 * D_MODEL
ROPE_THETA = 500000.0
ROPE_FRACTION = 4
NORM_EPS = 1e-6

LANES = 128
VMEM_LIMIT = 48 * 1024 * 1024
VMEM_BUDGET = 40 * 1024 * 1024
NEG_BIG = -1.0e30
SCORE_FLOOR = -1.0e28
LOG2E = 1.4426950408889634

NSA_TQ = 256
NSA_SEL_CHUNK = 512
NSA_CHAIN_LANES = 256
DIFF_TQ = 256
ROW_TILE = 512
COL_TILE = 512
FF_TILE = 1024


def _params(*sem):
    return pltpu.CompilerParams(dimension_semantics=sem, vmem_limit_bytes=VMEM_LIMIT)


def _dot(a, b):
    return jnp.dot(a, b, preferred_element_type=F32)


def _rms(x, g):
    ms = jnp.mean(x * x, axis=-1, keepdims=True)
    return x * lax.rsqrt(ms + NORM_EPS) * g


def _proj_tiles(s, k, n, row_bytes):
    weight = 2 * k * n * 2
    for tm in (ROW_TILE, ROW_TILE // 2):
        if weight + tm * row_bytes <= VMEM_BUDGET:
            return min(tm, s), n
    return min(ROW_TILE, s), min(COL_TILE, n)


def _weight_spec(w, widx, k, tn):
    if w.ndim == 2:
        return pl.BlockSpec((k, tn), lambda i, j: (0, j))
    return pl.BlockSpec((None, k, tn), lambda i, j: (widx, 0, j))


def _normed(x_ref, g_ref, h_scr, col_tiles):
    if col_tiles == 1:
        return _rms(x_ref[...], g_ref[...]).astype(BF16)

    @pl.when(pl.program_id(1) == 0)
    def _():
        h_scr[...] = _rms(x_ref[...], g_ref[...]).astype(BF16)

    return h_scr[...]


def _store_col_blocks(o_ref, acc):
    for blk in range(acc.shape[1] // LANES):
        o_ref[blk] = acc[:, blk * LANES:(blk + 1) * LANES].astype(o_ref.dtype)


def _blocked_out(s, n, tm, tn, dtype):
    spec = pl.BlockSpec((tn // LANES, tm, LANES), lambda i, j: (j, i, 0))
    return spec, jax.ShapeDtypeStruct((n // LANES, s, LANES), dtype)


def _norm_proj_kernel(x_ref, g_ref, w_ref, o_ref, gate_ref):
    acc = _dot(_rms(x_ref[...], g_ref[...]).astype(BF16), w_ref[...])
    n_main = o_ref.shape[0] * LANES
    _store_col_blocks(o_ref, acc[:, :n_main])
    gate_ref[...] = jax.nn.sigmoid(acc[:, n_main:])


def _norm_proj_gated(x, g, w, *, widx=None, gate_cols=LANES):
    s, d = x.shape
    n_all = w.shape[-1]
    n = n_all - gate_cols
    tm, tn = _proj_tiles(s, d, n_all, 2 * d * 4 + d * 2 + n_all * 4 + 2 * n * 2 + 2 * gate_cols * 4)
    assert tn == n_all, "the gate split needs the whole weight in one column tile"
    out_spec, out_shape = _blocked_out(s, n, tm, n, BF16)
    return pl.pallas_call(
        _norm_proj_kernel,
        name="norm_proj",
        grid=(s // tm, 1),
        in_specs=[pl.BlockSpec((tm, d), lambda i, j: (i, 0)),
                  pl.BlockSpec((1, d), lambda i, j: (0, 0)),
                  _weight_spec(w, widx, d, tn)],
        out_specs=[out_spec, pl.BlockSpec((tm, gate_cols), lambda i, j: (i, 0))],
        out_shape=[out_shape, jax.ShapeDtypeStruct((s, gate_cols), F32)],
        compiler_params=_params("parallel", "arbitrary"),
    )(x, g.reshape(1, d), w)


def _norm_proj_rope_kernel(x_ref, g_ref, w_ref, c_ref, sa_ref, sb_ref, *rest, shift, want_raw,
                           col_tiles):
    if want_raw:
        raw_ref, rot_ref, h_scr = rest
    else:
        rot_ref, h_scr = rest

    acc = _dot(_normed(x_ref, g_ref, h_scr, col_tiles), w_ref[...])
    if want_raw:
        _store_col_blocks(raw_ref, acc)
    c, sa, sb = c_ref[...], sa_ref[...], sb_ref[...]
    for blk in range(acc.shape[1] // LANES):
        a = acc[:, blk * LANES:(blk + 1) * LANES]
        r = a * c + pltpu.roll(a, LANES - shift, 1) * sa + pltpu.roll(a, shift, 1) * sb
        rot_ref[blk] = r.astype(BF16)


def _norm_proj_rope(x, g, w, tables, *, shift, want_raw, widx=None):
    s, d = x.shape
    n = w.shape[-1]
    n_out = 2 if want_raw else 1
    tm, tn = _proj_tiles(s, d, n, 2 * d * 4 + d * 2 + n * 4 + n_out * 2 * n * 2 + 3 * 2 * LANES * 4)
    tab_spec = pl.BlockSpec((tm, LANES), lambda i, j: (i, 0))
    out_spec, out_sd = _blocked_out(s, n, tm, tn, BF16)
    return pl.pallas_call(
        functools.partial(_norm_proj_rope_kernel, shift=shift, want_raw=want_raw,
                          col_tiles=n // tn),
        name="norm_proj_rope",
        grid=(s // tm, n // tn),
        in_specs=[pl.BlockSpec((tm, d), lambda i, j: (i, 0)),
                  pl.BlockSpec((1, d), lambda i, j: (0, 0)),
                  _weight_spec(w, widx, d, tn),
                  tab_spec, tab_spec, tab_spec],
        out_specs=[out_spec, out_spec] if want_raw else out_spec,
        out_shape=[out_sd, out_sd] if want_raw else out_sd,
        scratch_shapes=[pltpu.VMEM((tm, d), BF16)],
        compiler_params=_params("parallel", "arbitrary"),
    )(x, g.reshape(1, d), w, *tables)


def _matmul_res_kernel(a_ref, w_ref, r_ref, o_ref):
    o_ref[...] = r_ref[...] + _dot(a_ref[...], w_ref[...])


def _matmul_res(a, w, res, *, widx=None):
    s, k = a.shape
    n = w.shape[-1]
    tm, tn = _proj_tiles(s, k, n, 2 * k * 2 + 2 * 2 * n * 4 + n * 4)
    return pl.pallas_call(
        _matmul_res_kernel,
        name="matmul_res",
        grid=(s // tm, n // tn),
        in_specs=[pl.BlockSpec((tm, k), lambda i, j: (i, 0)),
                  _weight_spec(w, widx, k, tn),
                  pl.BlockSpec((tm, tn), lambda i, j: (i, j))],
        out_specs=pl.BlockSpec((tm, tn), lambda i, j: (i, j)),
        out_shape=jax.ShapeDtypeStruct((s, n), F32),
        compiler_params=_params("parallel", "arbitrary"),
    )(a, w, res)


def _mlp_kernel(x_ref, g_ref, wu_ref, wd_ref, fg_ref, o_ref, h_scr, acc_scr, *, final_norm):
    j = pl.program_id(1)

    @pl.when(j == 0)
    def _():
        h_scr[...] = _rms(x_ref[...], g_ref[...]).astype(BF16)
        acc_scr[...] = jnp.zeros_like(acc_scr)

    u = _dot(h_scr[...], wu_ref[...])
    u = jnp.square(jnp.maximum(u, 0.0)).astype(BF16)
    acc_scr[...] += _dot(u, wd_ref[...])

    @pl.when(j == pl.num_programs(1) - 1)
    def _():
        y = x_ref[...] + acc_scr[...]
        if final_norm:
            y = _rms(y, fg_ref[...])
        o_ref[...] = y


def _mlp(x, g, w_up, w_down, layer, final_g, *, final_norm):
    s, d = x.shape
    ff = w_up.shape[-1]
    tm, tf = min(ROW_TILE, s), min(FF_TILE, ff)
    return pl.pallas_call(
        functools.partial(_mlp_kernel, final_norm=final_norm),
        name="sq_relu_mlp",
        grid=(s // tm, ff // tf),
        in_specs=[pl.BlockSpec((tm, d), lambda i, j: (i, 0)),
                  pl.BlockSpec((1, d), lambda i, j: (0, 0)),
                  pl.BlockSpec((None, d, tf), lambda i, j: (layer, 0, j)),
                  pl.BlockSpec((None, tf, d), lambda i, j: (layer, j, 0)),
                  pl.BlockSpec((1, d), lambda i, j: (0, 0))],
        out_specs=pl.BlockSpec((tm, d), lambda i, j: (i, 0)),
        out_shape=jax.ShapeDtypeStruct((s, d), F32),
        scratch_shapes=[pltpu.VMEM((tm, d), BF16), pltpu.VMEM((tm, d), F32)],
        compiler_params=_params("parallel", "arbitrary"),
    )(x, g.reshape(1, d), w_up, w_down, final_g.reshape(1, d))


def _compress_kernel(t_ref, pos_ref, w1_ref, w2_ref, o_ref):
    t = t_ref[0, 0].astype(F32)
    pos = pos_ref[0]
    half = t.shape[1]
    a = _dot((t + pos[0:1]).astype(BF16), w1_ref[0, :half, :])
    b = _dot((t + pos[1:2]).astype(BF16), w1_ref[0, half:, :])
    nc = t.shape[0]
    hid = jax.nn.gelu(a + pltpu.roll(b, nc - 1, 0))
    o_ref[0, 0] = _dot(hid.astype(BF16), w2_ref[0]).astype(BF16)


def _compress(t2, pos2, w1, w2):
    _, hk, nc, width = t2.shape
    dk = w2.shape[-1]
    return pl.pallas_call(
        _compress_kernel,
        name="nsa_compress",
        grid=(2, hk),
        in_specs=[pl.BlockSpec((1, 1, nc, width), lambda j, h: (j, h, 0, 0)),
                  pl.BlockSpec((1, 2, width), lambda j, h: (j, 0, 0)),
                  pl.BlockSpec((1, 2 * width, CMP_HIDDEN), lambda j, h: (j, 0, 0)),
                  pl.BlockSpec((1, CMP_HIDDEN, dk), lambda j, h: (j, 0, 0))],
        out_specs=pl.BlockSpec((1, 1, nc, dk), lambda j, h: (j, h, 0, 0)),
        out_shape=jax.ShapeDtypeStruct((2, hk, nc, dk), BF16),
        compiler_params=_params("parallel", "arbitrary"),
    )(t2, pos2, w1, w2)


def _nsa_attn_kernel(qr_ref, qo_ref, kc_ref, vc_ref, ks_ref, oh_ref, vs_ref, kw_ref, vw_ref, gt_ref,
                     wu_ref, wd_ref, o_ref, wu_out_ref, wd_out_ref,
                     vst_scr, vwt_scr, rhs_scr, s_scr, m_scr, l_scr, acc_scr, *, seq):
    g_, tq, dk = NSA_GROUP, NSA_TQ, NSA_HEAD_DIM
    cw = NSA_CHAIN_LANES
    n_chain = g_ * tq // cw
    nc = seq // CMP_STRIDE
    nsel = seq // SEL_BLOCK
    ch = min(NSA_SEL_CHUNK, seq)
    wlen = WINDOW + tq
    qb = pl.program_id(1)
    start = qb * tq
    t_row = start + lax.broadcasted_iota(jnp.int32, (1, tq), 1)

    wu_out_ref[...] = wu_ref[...].astype(BF16)
    wd_out_ref[...] = wd_ref[...].astype(BF16)

    @pl.when(qb == 0)
    def _():
        for r in range(0, seq, ch):
            vst_scr[:, r:r + ch] = vs_ref[r:r + ch, :].T
            vwt_scr[:, r:r + ch] = vw_ref[r:r + ch, :].T

    def heads_on_lanes(ref):
        return jnp.concatenate([ref[g].T for g in range(g_)], axis=1)

    def tile_heads(x, n):
        return jnp.concatenate([x] * n, axis=1)

    def softmax_keys(s2):
        m = jnp.max(s2, axis=0, keepdims=True)
        m = jnp.where(m == -jnp.inf, 0.0, m)
        e = jnp.exp2(s2 - m)
        return e, 1.0 / jnp.maximum(jnp.sum(e, axis=0, keepdims=True), 1e-30)

    qr_t = heads_on_lanes(qr_ref)
    qo_t = heads_on_lanes(qo_ref)

    n_end = lax.broadcasted_iota(jnp.int32, (nc, tq), 0) * CMP_STRIDE + (CMP_BLOCK - 1)
    bias_c = jnp.where(n_end <= t_row, 0.0, -jnp.inf)
    e_c, r_c = softmax_keys(_dot(kc_ref[0, 0], qr_t) + tile_heads(bias_c, g_))
    p_c = e_c * r_c
    o_c = _dot(vc_ref[0, 0].T, p_c.astype(BF16))

    w0 = pl.multiple_of(jnp.maximum(start - WINDOW, 0), tq)
    kpos = w0 + lax.broadcasted_iota(jnp.int32, (wlen, tq), 0)
    bias_w = jnp.where((kpos <= t_row) & (kpos > t_row - WINDOW), 0.0, -jnp.inf)
    e_w, r_w = softmax_keys(_dot(kw_ref[pl.ds(w0, wlen), :], qo_t) + tile_heads(bias_w, g_))
    o_w = _dot(vwt_scr[:, pl.ds(w0, wlen)], e_w.astype(BF16)) * r_w

    p_sum = p_c[:, :tq]
    for g in range(1, g_):
        p_sum = p_sum + p_c[:, g * tq:(g + 1) * tq]
    n_i = lax.broadcasted_iota(jnp.int32, (nsel, nc), 1) * CMP_STRIDE
    s_i = lax.broadcasted_iota(jnp.int32, (nsel, nc), 0) * SEL_BLOCK
    overlap = jnp.where((n_i < s_i + SEL_BLOCK) & (n_i + CMP_BLOCK > s_i), 1.0, 0.0).astype(BF16)
    p_hi = p_sum.astype(BF16)
    p_r1 = p_sum - p_hi.astype(F32)
    p_mid = p_r1.astype(BF16)
    p_lo = (p_r1 - p_mid.astype(F32)).astype(BF16)
    imp = _dot(overlap, p_hi) + _dot(overlap, p_mid) + _dot(overlap, p_lo)

    ids = lax.broadcasted_iota(jnp.int32, (nsel, tq), 0)
    cur = t_row >> SEL_SHIFT
    forced = (ids == 0) | (ids == cur) | (ids == cur - 1)
    imp = jnp.where(forced, FORCE_SCORE, imp)
    work = jnp.where(ids * SEL_BLOCK <= t_row, imp, -1.0)

    sel = jnp.zeros((nsel, tq), F32)
    for _ in range(min(SEL_TOPK, nsel)):
        mx = jnp.max(work, axis=0, keepdims=True)
        first = jnp.min(jnp.where(work == mx, ids, nsel), axis=0, keepdims=True)
        pick = ids == first
        sel = jnp.where(pick, 1.0, sel)
        work = jnp.where(pick, -3.0e38, work)

    neg_sel = tile_heads(jnp.where(sel > 0.5, 0.0, NEG_BIG).astype(BF16), cw // tq)
    for j in range(n_chain):
        rhs_scr[j] = jnp.concatenate([qo_t[:, j * cw:(j + 1) * cw], neg_sel], axis=0)
    m_scr[...] = jnp.full(m_scr.shape, NEG_BIG, F32)
    l_scr[...] = jnp.zeros(l_scr.shape, F32)
    acc_scr[...] = jnp.zeros(acc_scr.shape, F32)

    def keys_with_block(off):
        return jnp.concatenate([ks_ref[pl.ds(off, ch), :], oh_ref[pl.ds(off, ch), :]], axis=1)

    def fold(off, buf, diag_width=None):
        diagonal = diag_width is not None
        width = diag_width if diagonal else ch
        off = pl.multiple_of(off, ch)
        v_t = vst_scr[:, pl.ds(off, width)]
        state = [(m_scr[j], l_scr[j], acc_scr[j]) for j in range(n_chain)]
        if diagonal:
            key = off + lax.broadcasted_iota(jnp.int32, (width, tq), 0)
            causal = tile_heads(jnp.where(key <= t_row, 0.0, NEG_BIG), cw // tq)
            todo = []
        else:
            k_next = keys_with_block(pl.multiple_of(off + ch, ch))
            todo = list(range(n_chain))
        ahead = 3
        for j in todo[:ahead]:
            s_scr[1 - buf, j] = _dot(k_next, rhs_scr[j])
        new_state = []
        for j in range(n_chain):
            s = s_scr[buf, j, :width]
            if diagonal:
                s = s + causal
            m_old, l_old, acc_old = state[j]
            m_new = jnp.maximum(m_old, jnp.max(s, axis=0, keepdims=True))
            alpha = jnp.exp2(m_old - m_new)
            e = jnp.exp2(s - m_new)
            l_new = alpha * l_old + jnp.sum(e, axis=0, keepdims=True)
            acc_new = alpha * acc_old + _dot(v_t, e.astype(BF16))
            new_state.append((m_new, l_new, acc_new))
            for jn in todo[j + ahead:j + ahead + 1]:
                s_scr[1 - buf, jn] = _dot(k_next, rhs_scr[jn])
        for j in range(n_chain):
            m_scr[j], l_scr[j], acc_scr[j] = new_state[j]

    k_first = keys_with_block(0)
    for j in range(n_chain):
        s_scr[0, j] = _dot(k_first, rhs_scr[j])

    def pair_step(c, carry):
        off = pl.multiple_of(c * 2 * ch, 2 * ch)
        fold(off, 0)
        fold(off + ch, 1)
        return carry

    n_full = (start + tq - 1) // ch
    lax.fori_loop(0, n_full // 2, pair_step, 0)
    rest = pl.multiple_of((n_full // 2) * 2 * ch, 2 * ch)

    tiles_per_chunk = ch // tq
    for rem in range(2 * tiles_per_chunk):
        @pl.when(qb % (2 * tiles_per_chunk) == rem)
        def _(rem=rem):
            diag_width = (rem % tiles_per_chunk + 1) * tq
            if rem // tiles_per_chunk == 1:
                fold(rest, 0)
                fold(rest + ch, 1, diag_width)
            else:
                fold(rest, 0, diag_width)

    gt = gt_ref[0]
    for g in range(g_):
        j = g * tq // cw
        lanes = slice(g * tq - j * cw, g * tq - j * cw + tq)
        o_s = jnp.where(m_scr[j][:, lanes] > SCORE_FLOOR,
                        acc_scr[j][:, lanes] * (1.0 / jnp.maximum(l_scr[j][:, lanes], 1e-30)), 0.0)
        cols = slice(g * tq, (g + 1) * tq)
        o_t = (gt[3 * g:3 * g + 1] * o_c[:, cols] + gt[3 * g + 1:3 * g + 2] * o_s
               + gt[3 * g + 2:3 * g + 3] * o_w[:, cols])
        o_ref[:, g * dk:(g + 1) * dk] = o_t.T.astype(BF16)


def _nsa_attention(raw, rot, plain, cmp_kv, gates, w_up, w_down, layer):
    s = raw.shape[1]
    hk, g_, dk, tq = NSA_KV_HEADS, NSA_GROUP, NSA_HEAD_DIM, NSA_TQ
    nc = s // CMP_STRIDE
    nsel = s // SEL_BLOCK
    one_hot = (jnp.arange(s)[:, None] >> SEL_SHIFT == jnp.arange(nsel)[None, :]).astype(BF16)
    gates_t = gates[:, :3 * NSA_HEADS].T.reshape(hk, 3 * g_, s)
    qspec = pl.BlockSpec((g_, tq, dk), lambda h, i: (h, i, 0))
    col = lambda base: pl.BlockSpec((None, s, dk), lambda h, i: (base + h, 0, 0))
    cmp_spec = lambda j: pl.BlockSpec((1, 1, nc, dk), lambda h, i: (j, h, 0, 0))
    cw = NSA_CHAIN_LANES
    assert cw % tq == 0 and (g_ * tq) % cw == 0 and NSA_SEL_CHUNK % tq == 0
    n_chain = g_ * tq // cw
    n_q = s // tq
    n_prog = hk * n_q
    per_call = w_up.shape[0] // N_A_LAYERS
    wu2 = w_up.reshape(-1, w_up.shape[-1])
    wd2 = w_down.reshape(-1, w_down.shape[-1])
    ru, rd = per_call * w_up.shape[1] // n_prog, per_call * w_down.shape[1] // n_prog
    cast_in = lambda rows, cols: pl.BlockSpec((rows, cols), lambda h, i: (layer * n_prog + h * n_q + i, 0))
    cast_out = lambda rows, cols: pl.BlockSpec((rows, cols), lambda h, i: (h * n_q + i, 0))
    o, wu_b, wd_b = pl.pallas_call(
        functools.partial(_nsa_attn_kernel, seq=s),
        name="nsa_attention",
        grid=(hk, n_q),
        in_specs=[qspec, qspec, cmp_spec(0), cmp_spec(1),
                  col(NSA_HEADS),
                  pl.BlockSpec((s, nsel), lambda h, i: (0, 0)),
                  col(2 * hk),
                  col(NSA_HEADS + hk), col(3 * hk),
                  pl.BlockSpec((1, 3 * g_, tq), lambda h, i: (h, 0, i)),
                  cast_in(ru, wu2.shape[1]), cast_in(rd, wd2.shape[1])],
        out_specs=[pl.BlockSpec((tq, g_ * dk), lambda h, i: (i, h)),
                   cast_out(ru, wu2.shape[1]), cast_out(rd, wd2.shape[1])],
        out_shape=[jax.ShapeDtypeStruct((s, NSA_HEADS * dk), BF16),
                   jax.ShapeDtypeStruct((ru * n_prog, wu2.shape[1]), BF16),
                   jax.ShapeDtypeStruct((rd * n_prog, wd2.shape[1]), BF16)],
        scratch_shapes=[pltpu.VMEM((dk, s), BF16), pltpu.VMEM((dk, s), BF16),
                        pltpu.VMEM((n_chain, dk + nsel, cw), BF16),
                        pltpu.VMEM((2, n_chain, min(NSA_SEL_CHUNK, s), cw), F32),
                        pltpu.VMEM((n_chain, 1, cw), F32),
                        pltpu.VMEM((n_chain, 1, cw), F32),
                        pltpu.VMEM((n_chain, dk, cw), F32)],
        compiler_params=_params("parallel", "arbitrary"),
    )(raw, rot, cmp_kv, cmp_kv, rot, one_hot, plain, rot, plain, gates_t, wu2, wd2)
    return o, wu_b.reshape(per_call, *w_up.shape[1:]), wd_b.reshape(per_call, *w_down.shape[1:])


def _diff_attn_kernel(q_ref, k_ref, v_ref, lam_ref, sg_ref, o_ref, qq_scr, vt_scr, s_scr, m_scr,
                      l_scr, acc_scr, *, lambda_init):
    g_, tq, d = DIFF_GROUP, DIFF_TQ, DIFF_HEAD_DIM
    n_chain = 2 * g_
    qb = pl.program_id(1)

    @pl.when(qb == 0)
    def _():
        rows = 2 * tq
        for r in range(0, v_ref.shape[0], rows):
            vt_scr[:, r:r + rows] = v_ref[r:r + rows, :].T

    first_half = lax.broadcasted_iota(jnp.int32, (2 * d, 1), 0) < d
    for g in range(g_):
        q_t = q_ref[g].T
        zero = jnp.zeros_like(q_t)
        qq_scr[2 * g] = jnp.where(first_half, q_t, zero)
        qq_scr[2 * g + 1] = jnp.where(first_half, zero, q_t)
    m_scr[...] = jnp.full(m_scr.shape, NEG_BIG, F32)
    l_scr[...] = jnp.zeros(l_scr.shape, F32)
    acc_scr[...] = jnp.zeros(acc_scr.shape, F32)

    ch = 2 * tq
    n_full = qb // 2

    def fold(off, buf, diag_width=None):
        diagonal = diag_width is not None
        width = diag_width if diagonal else ch
        off = pl.multiple_of(off, ch)
        v_t = vt_scr[:, pl.ds(off, width)]
        state = [(m_scr[j], l_scr[j], acc_scr[j]) for j in range(n_chain)]
        if diagonal:
            key = lax.broadcasted_iota(jnp.int32, (width, tq), 0) + (off - qb * tq)
            causal = key <= lax.broadcasted_iota(jnp.int32, (width, tq), 1)
            todo = []
        else:
            k_next = k_ref[pl.ds(pl.multiple_of(off + ch, ch), ch), :]
            todo = list(range(n_chain))
        ahead = 3
        for j in todo[:ahead]:
            s_scr[1 - buf, j] = _dot(k_next, qq_scr[j])
        new_state = []
        for j in range(n_chain):
            s = s_scr[buf, j, :width]
            if diagonal:
                s = jnp.where(causal, s, NEG_BIG)
            m_old, l_old, acc_old = state[j]
            m_new = jnp.maximum(m_old, jnp.max(s, axis=0, keepdims=True))
            alpha = jnp.exp2(m_old - m_new)
            p = jnp.exp2(s - m_new)
            l_new = alpha * l_old + jnp.sum(p, axis=0, keepdims=True)
            acc_new = alpha * acc_old + _dot(v_t, p.astype(BF16))
            new_state.append((m_new, l_new, acc_new))
            for jn in todo[j + ahead:j + ahead + 1]:
                s_scr[1 - buf, jn] = _dot(k_next, qq_scr[jn])
        for j in range(n_chain):
            m_scr[j], l_scr[j], acc_scr[j] = new_state[j]

    k_first = k_ref[:ch, :]
    for j in range(n_chain):
        s_scr[0, j] = _dot(k_first, qq_scr[j])

    def pair_step(c, carry):
        off = pl.multiple_of(c * 2 * ch, 2 * ch)
        fold(off, 0)
        fold(off + ch, 1)
        return carry

    lax.fori_loop(0, n_full // 2, pair_step, 0)
    rest = pl.multiple_of((n_full // 2) * 2 * ch, 2 * ch)

    for rem in range(4):
        @pl.when(qb % 4 == rem)
        def _(rem=rem):
            diag_width = ch if rem % 2 == 1 else tq
            if rem // 2 == 1:
                fold(rest, 0)
                fold(rest + ch, 1, diag_width)
            else:
                fold(rest, 0, diag_width)

    lv = lam_ref[...]
    lam = (jnp.exp(jnp.sum(lv[0:1] * lv[1:2], axis=-1, keepdims=True))
           - jnp.exp(jnp.sum(lv[2:3] * lv[3:4], axis=-1, keepdims=True)) + lambda_init)
    sub_g = sg_ref[...]
    for g in range(g_):
        o1 = acc_scr[2 * g] / jnp.maximum(l_scr[2 * g], 1e-30)
        o2 = acc_scr[2 * g + 1] / jnp.maximum(l_scr[2 * g + 1], 1e-30)
        a = o1 - lam * o2
        ms = jnp.mean(a * a, axis=0, keepdims=True)
        a = a * lax.rsqrt(ms + NORM_EPS) * sub_g * (1.0 - lambda_init)
        o_ref[:, g * 2 * d:(g + 1) * 2 * d] = a.T.astype(BF16)


def _diff_attention(q_rot, k_rot, v, lam_vecs, subln_g, lambda_init):
    s = q_rot.shape[1]
    hk, g_, d, tq = DIFF_KV_HEADS, DIFF_GROUP, DIFF_HEAD_DIM, DIFF_TQ
    return pl.pallas_call(
        functools.partial(_diff_attn_kernel, lambda_init=lambda_init),
        name="diff_attention",
        grid=(hk, s // tq),
        in_specs=[pl.BlockSpec((g_, tq, 2 * d), lambda h, i: (h, i, 0)),
                  pl.BlockSpec((None, s, 2 * d), lambda h, i: (h, 0, 0)),
                  pl.BlockSpec((None, s, 2 * d), lambda h, i: (hk + h, 0, 0)),
                  pl.BlockSpec((4, d), lambda h, i: (0, 0)),
                  pl.BlockSpec((2 * d, 1), lambda h, i: (0, 0))],
        out_specs=pl.BlockSpec((tq, g_ * 2 * d), lambda h, i: (i, h)),
        out_shape=jax.ShapeDtypeStruct((s, DIFF_HEADS * 2 * d), BF16),
        scratch_shapes=[pltpu.VMEM((2 * g_, 2 * d, tq), BF16),
                        pltpu.VMEM((2 * d, s), BF16),
                        pltpu.VMEM((2, 2 * g_, 2 * tq, tq), F32),
                        pltpu.VMEM((2 * g_, 1, tq), F32),
                        pltpu.VMEM((2 * g_, 1, tq), F32),
                        pltpu.VMEM((2 * g_, 2 * d, tq), F32)],
        compiler_params=_params("parallel", "arbitrary"),
    )(q_rot, k_rot, v, lam_vecs, subln_g.reshape(2 * d, 1))


def _rope_tables(seq, head_dim, reps):
    rot = head_dim // ROPE_FRACTION
    half = rot // 2
    inv = 1.0 / (ROPE_THETA ** (jnp.arange(0, rot, 2, dtype=F32) / rot))
    ang = jnp.arange(seq, dtype=F32)[:, None] * inv[None, :]
    cos, sin = jnp.cos(ang), jnp.sin(ang)
    rest = head_dim - rot
    zeros_h = jnp.zeros((seq, half), F32)
    c = jnp.concatenate([cos, cos, jnp.ones((seq, rest), F32)], axis=-1)
    sa = jnp.concatenate([-sin, zeros_h, jnp.zeros((seq, rest), F32)], axis=-1)
    sb = jnp.concatenate([zeros_h, sin, jnp.zeros((seq, rest), F32)], axis=-1)
    return tuple(jnp.tile(t, (1, reps)) for t in (c, sa, sb)), half


def _nsa_weights(w_in):
    hk, dk = NSA_KV_HEADS, NSA_HEAD_DIM
    qw, kvw = NSA_HEADS * dk, hk * dk
    part = lambda i: w_in[:, :, qw + i * kvw:qw + (i + 1) * kvw]
    w_q = w_in[:, :, :qw] * (dk ** -0.5 * LOG2E)
    w_rope = jnp.concatenate([w_q, part(2), part(4)], axis=2).astype(BF16)
    n_gate = 3 * NSA_HEADS
    w_gate = jnp.pad(w_in[:, :, qw + 6 * kvw:], ((0, 0), (0, 0), (0, LANES - n_gate)))
    w_plain = jnp.concatenate([part(0), part(1), part(3), part(5), w_gate], axis=2).astype(BF16)
    return w_rope, w_plain


def _nsa_layer(x, norm_g, proj_w, cmp_pos, cmp_w1, cmp_w2, w_out, mlp_w, layer, tables, shift):
    s = x.shape[0]
    hk, dk = NSA_KV_HEADS, NSA_HEAD_DIM
    w_rope, w_plain = proj_w
    raw, rot = _norm_proj_rope(x, norm_g, w_rope, tables, widx=layer, shift=shift, want_raw=True)
    plain, gates = _norm_proj_gated(x, norm_g, w_plain, widx=layer)

    nc = s // CMP_STRIDE
    t2 = plain[:2 * hk].reshape(2, hk, nc, CMP_STRIDE * dk)
    cmp_kv = _compress(t2, cmp_pos.reshape(2, 2, CMP_STRIDE * dk), cmp_w1.astype(BF16),
                       cmp_w2.astype(BF16))
    o, w_up, w_down = _nsa_attention(raw, rot, plain, cmp_kv, gates, *mlp_w, layer)
    return _matmul_res(o, w_out, x, widx=layer), (w_up, w_down)


def kernel(x, attn_norm_g, mlp_norm_g, final_norm_g, nsa_w_in, nsa_cmp_pos, nsa_cmp_w1, nsa_cmp_w2, nsa_w_out, kv_norm_g, kv_w_shared, diff_w_q, diff_lambda, diff_subln_g, diff_w_out, mlp_w_up, mlp_w_down):
    b, s, d = x.shape
    tables_a, shift_a = _rope_tables(s, NSA_HEAD_DIM, 1)
    tables_b, shift_b = _rope_tables(s, DIFF_HEAD_DIM, 2)
    assert DEPTH == 2 * N_A_LAYERS
    nsa_out, diff_out = nsa_w_out.astype(BF16), diff_w_out.astype(BF16)
    diff_q = (diff_w_q * (DIFF_HEAD_DIM ** -0.5 * LOG2E)).astype(BF16)
    nsa_proj = _nsa_weights(nsa_w_in)
    outs = []
    for bi in range(b):
        xs = x[bi]
        k_sh = v_sh = None
        mixer_w = {}
        for layer in range(DEPTH):
            if layer < N_A_LAYERS:
                xs, mixer_w[layer] = _nsa_layer(xs, attn_norm_g[layer], nsa_proj, nsa_cmp_pos[layer],
                                                nsa_cmp_w1[layer], nsa_cmp_w2[layer], nsa_out,
                                                (mlp_w_up, mlp_w_down), layer, tables_a, shift_a)
            else:
                j = layer - N_A_LAYERS
                if j == 0:
                    v_sh, k_sh = _norm_proj_rope(xs, kv_norm_g, kv_w_shared.astype(BF16), tables_b,
                                                 shift=shift_b, want_raw=True)
                lambda_init = 0.8 - 0.6 * math.exp(-0.3 * layer)
                q = _norm_proj_rope(xs, attn_norm_g[layer], diff_q, tables_b, widx=j,
                                    shift=shift_b, want_raw=False)
                o = _diff_attention(q, k_sh, v_sh, diff_lambda[j], diff_subln_g[j], lambda_init)
                xs = _matmul_res(o, diff_out, xs, widx=j)
            w_up, w_down = mixer_w[layer // 2]
            xs = _mlp(xs, mlp_norm_g[layer], w_up, w_down, layer % 2, final_norm_g,
                      final_norm=(layer == DEPTH - 1))
        outs.append(xs[None])
    return outs[0] if b == 1 else jnp.concatenate(outs, axis=0)
```

```python
import functools
import math

import jax
import jax.numpy as jnp
from jax import lax
from jax.experimental import pallas as pl
from jax.experimental.pallas import tpu as pltpu

F32 = jnp.float32
BF16 = jnp.bfloat16

D_MODEL = 2048
DEPTH = 4
N_A_LAYERS = DEPTH // 2

NSA_HEADS = 16
NSA_KV_HEADS = 4
NSA_GROUP = NSA_HEADS // NSA_KV_HEADS
NSA_HEAD_DIM = D_MODEL // NSA_HEADS
CMP_BLOCK = 32
CMP_STRIDE = 16
CMP_HIDDEN = 4 * NSA_HEAD_DIM
SEL_BLOCK = 64
SEL_SHIFT = SEL_BLOCK.bit_length() - 1
SEL_TOPK = 16
WINDOW = 512
FORCE_SCORE = 1.0e4

DIFF_HEADS = 16
DIFF_KV_HEADS = 4
DIFF_GROUP = DIFF_HEADS // DIFF_KV_HEADS
DIFF_HEAD_DIM = D_MODEL // (2 * DIFF_HEADS)

D_FF = 4 * D_MODEL
ROPE_THETA = 500000.0
ROPE_FRACTION = 4
NORM_EPS = 1e-6

LANES = 128
VMEM_LIMIT = 48 * 1024 * 1024
VMEM_BUDGET = 40 * 1024 * 1024
NEG_BIG = -1.0e30
SCORE_FLOOR = -1.0e28
LOG2E = 1.4426950408889634

NSA_TQ = 256
NSA_SEL_CHUNK = 512
NSA_CHAIN_LANES = 256
DIFF_TQ = 256
ROW_TILE = 512
COL_TILE = 512
FF_TILE = 1024


def _params(*sem):
    return pltpu.CompilerParams(dimension_semantics=sem, vmem_limit_bytes=VMEM_LIMIT)


def _dot(a, b):
    return jnp.dot(a, b, preferred_element_type=F32)


def _rms(x, g):
    ms = jnp.mean(x * x, axis=-1, keepdims=True)
    return x * lax.rsqrt(ms + NORM_EPS) * g


def _proj_tiles(s, k, n, row_bytes):
    weight = 2 * k * n * 2
    for tm in (ROW_TILE, ROW_TILE // 2):
        if weight + tm * row_bytes <= VMEM_BUDGET:
            return min(tm, s), n
    return min(ROW_TILE, s), min(COL_TILE, n)


def _weight_spec(w, widx, k, tn):
    if w.ndim == 2:
        return pl.BlockSpec((k, tn), lambda i, j: (0, j))
    return pl.BlockSpec((None, k, tn), lambda i, j: (widx, 0, j))


def _normed(x_ref, g_ref, h_scr, col_tiles):
    if col_tiles == 1:
        return _rms(x_ref[...], g_ref[...]).astype(BF16)

    @pl.when(pl.program_id(1) == 0)
    def _():
        h_scr[...] = _rms(x_ref[...], g_ref[...]).astype(BF16)

    return h_scr[...]


def _store_col_blocks(o_ref, acc):
    for blk in range(acc.shape[1] // LANES):
        o_ref[blk] = acc[:, blk * LANES:(blk + 1) * LANES].astype(o_ref.dtype)


def _blocked_out(s, n, tm, tn, dtype):
    spec = pl.BlockSpec((tn // LANES, tm, LANES), lambda i, j: (j, i, 0))
    return spec, jax.ShapeDtypeStruct((n // LANES, s, LANES), dtype)


def _norm_proj_kernel(x_ref, g_ref, w_ref, o_ref, gate_ref):
    acc = _dot(_rms(x_ref[...], g_ref[...]).astype(BF16), w_ref[...])
    n_main = o_ref.shape[0] * LANES
    _store_col_blocks(o_ref, acc[:, :n_main])
    gate_ref[...] = jax.nn.sigmoid(acc[:, n_main:])


def _norm_proj_gated(x, g, w, *, widx=None, gate_cols=LANES):
    s, d = x.shape
    n_all = w.shape[-1]
    n = n_all - gate_cols
    tm, tn = _proj_tiles(s, d, n_all, 2 * d * 4 + d * 2 + n_all * 4 + 2 * n * 2 + 2 * gate_cols * 4)
    assert tn == n_all, "the gate split needs the whole weight in one column tile"
    out_spec, out_shape = _blocked_out(s, n, tm, n, BF16)
    return pl.pallas_call(
        _norm_proj_kernel,
        name="norm_proj",
        grid=(s // tm, 1),
        in_specs=[pl.BlockSpec((tm, d), lambda i, j: (i, 0)),
                  pl.BlockSpec((1, d), lambda i, j: (0, 0)),
                  _weight_spec(w, widx, d, tn)],
        out_specs=[out_spec, pl.BlockSpec((tm, gate_cols), lambda i, j: (i, 0))],
        out_shape=[out_shape, jax.ShapeDtypeStruct((s, gate_cols), F32)],
        compiler_params=_params("parallel", "arbitrary"),
    )(x, g.reshape(1, d), w)


def _norm_proj_rope_kernel(x_ref, g_ref, w_ref, c_ref, sa_ref, sb_ref, *rest, shift, want_raw,
                           col_tiles):
    if want_raw:
        raw_ref, rot_ref, h_scr = rest
    else:
        rot_ref, h_scr = rest

    acc = _dot(_normed(x_ref, g_ref, h_scr, col_tiles), w_ref[...])
    if want_raw:
        _store_col_blocks(raw_ref, acc)
    c, sa, sb = c_ref[...], sa_ref[...], sb_ref[...]
    for blk in range(acc.shape[1] // LANES):
        a = acc[:, blk * LANES:(blk + 1) * LANES]
        r = a * c + pltpu.roll(a, LANES - shift, 1) * sa + pltpu.roll(a, shift, 1) * sb
        rot_ref[blk] = r.astype(BF16)


def _norm_proj_rope(x, g, w, tables, *, shift, want_raw, widx=None):
    s, d = x.shape
    n = w.shape[-1]
    n_out = 2 if want_raw else 1
    tm, tn = _proj_tiles(s, d, n, 2 * d * 4 + d * 2 + n * 4 + n_out * 2 * n * 2 + 3 * 2 * LANES * 4)
    tab_spec = pl.BlockSpec((tm, LANES), lambda i, j: (i, 0))
    out_spec, out_sd = _blocked_out(s, n, tm, tn, BF16)
    return pl.pallas_call(
        functools.partial(_norm_proj_rope_kernel, shift=shift, want_raw=want_raw,
                          col_tiles=n // tn),
        name="norm_proj_rope",
        grid=(s // tm, n // tn),
        in_specs=[pl.BlockSpec((tm, d), lambda i, j: (i, 0)),
                  pl.BlockSpec((1, d), lambda i, j: (0, 0)),
                  _weight_spec(w, widx, d, tn),
                  tab_spec, tab_spec, tab_spec],
        out_specs=[out_spec, out_spec] if want_raw else out_spec,
        out_shape=[out_sd, out_sd] if want_raw else out_sd,
        scratch_shapes=[pltpu.VMEM((tm, d), BF16)],
        compiler_params=_params("parallel", "arbitrary"),
    )(x, g.reshape(1, d), w, *tables)


def _matmul_res_kernel(a_ref, w_ref, r_ref, o_ref):
    o_ref[...] = r_ref[...] + _dot(a_ref[...], w_ref[...])


def _matmul_res(a, w, res, *, widx=None):
    s, k = a.shape
    n = w.shape[-1]
    tm, tn = _proj_tiles(s, k, n, 2 * k * 2 + 2 * 2 * n * 4 + n * 4)
    return pl.pallas_call(
        _matmul_res_kernel,
        name="matmul_res",
        grid=(s // tm, n // tn),
        in_specs=[pl.BlockSpec((tm, k), lambda i, j: (i, 0)),
                  _weight_spec(w, widx, k, tn),
                  pl.BlockSpec((tm, tn), lambda i, j: (i, j))],
        out_specs=pl.BlockSpec((tm, tn), lambda i, j: (i, j)),
        out_shape=jax.ShapeDtypeStruct((s, n), F32),
        compiler_params=_params("parallel", "arbitrary"),
    )(a, w, res)


def _mlp_kernel(x_ref, g_ref, wu_ref, wd_ref, fg_ref, o_ref, h_scr, acc_scr, *, final_norm):
    j = pl.program_id(1)

    @pl.when(j == 0)
    def _():
        h_scr[...] = _rms(x_ref[...], g_ref[...]).astype(BF16)
        acc_scr[...] = jnp.zeros_like(acc_scr)

    u = _dot(h_scr[...], wu_ref[...])
    u = jnp.square(jnp.maximum(u, 0.0)).astype(BF16)
    acc_scr[...] += _dot(u, wd_ref[...])

    @pl.when(j == pl.num_programs(1) - 1)
    def _():
        y = x_ref[...] + acc_scr[...]
        if final_norm:
            y = _rms(y, fg_ref[...])
        o_ref[...] = y


def _mlp(x, g, w_up, w_down, layer, final_g, *, final_norm):
    s, d = x.shape
    ff = w_up.shape[-1]
    tm, tf = min(ROW_TILE, s), min(FF_TILE, ff)
    return pl.pallas_call(
        functools.partial(_mlp_kernel, final_norm=final_norm),
        name="sq_relu_mlp",
        grid=(s // tm, ff // tf),
        in_specs=[pl.BlockSpec((tm, d), lambda i, j: (i, 0)),
                  pl.BlockSpec((1, d), lambda i, j: (0, 0)),
                  pl.BlockSpec((None, d, tf), lambda i, j: (layer, 0, j)),
                  pl.BlockSpec((None, tf, d), lambda i, j: (layer, j, 0)),
                  pl.BlockSpec((1, d), lambda i, j: (0, 0))],
        out_specs=pl.BlockSpec((tm, d), lambda i, j: (i, 0)),
        out_shape=jax.ShapeDtypeStruct((s, d), F32),
        scratch_shapes=[pltpu.VMEM((tm, d), BF16), pltpu.VMEM((tm, d), F32)],
        compiler_params=_params("parallel", "arbitrary"),
    )(x, g.reshape(1, d), w_up, w_down, final_g.reshape(1, d))


def _compress_kernel(t_ref, pos_ref, w1_ref, w2_ref, o_ref):
    t = t_ref[0, 0].astype(F32)
    pos = pos_ref[0]
    half = t.shape[1]
    a = _dot((t + pos[0:1]).astype(BF16), w1_ref[0, :half, :])
    b = _dot((t + pos[1:2]).astype(BF16), w1_ref[0, half:, :])
    nc = t.shape[0]
    hid = jax.nn.gelu(a + pltpu.roll(b, nc - 1, 0))
    o_ref[0, 0] = _dot(hid.astype(BF16), w2_ref[0]).astype(BF16)


def _compress(t2, pos2, w1, w2):
    _, hk, nc, width = t2.shape
    dk = w2.shape[-1]
    return pl.pallas_call(
        _compress_kernel,
        name="nsa_compress",
        grid=(2, hk),
        in_specs=[pl.BlockSpec((1, 1, nc, width), lambda j, h: (j, h, 0, 0)),
                  pl.BlockSpec((1, 2, width), lambda j, h: (j, 0, 0)),
                  pl.BlockSpec((1, 2 * width, CMP_HIDDEN), lambda j, h: (j, 0, 0)),
                  pl.BlockSpec((1, CMP_HIDDEN, dk), lambda j, h: (j, 0, 0))],
        out_specs=pl.BlockSpec((1, 1, nc, dk), lambda j, h: (j, h, 0, 0)),
        out_shape=jax.ShapeDtypeStruct((2, hk, nc, dk), BF16),
        compiler_params=_params("parallel", "arbitrary"),
    )(t2, pos2, w1, w2)


def _nsa_attn_kernel(qr_ref, qo_ref, kc_ref, vc_ref, ks_ref, oh_ref, vs_ref, kw_ref, vw_ref, gt_ref,
                     *refs, seq, cast_scales):
    n_cast = len(cast_scales)
    cast_src, o_ref, cast_dst = refs[:n_cast], refs[n_cast], refs[n_cast + 1:2 * n_cast + 1]
    vst_scr, vwt_scr, rhs_scr, s_scr, m_scr, l_scr, acc_scr = refs[2 * n_cast + 1:]
    g_, tq, dk = NSA_GROUP, NSA_TQ, NSA_HEAD_DIM
    cw = NSA_CHAIN_LANES
    n_chain = g_ * tq // cw
    nc = seq // CMP_STRIDE
    nsel = seq // SEL_BLOCK
    ch = min(NSA_SEL_CHUNK, seq)
    wlen = WINDOW + tq
    qb = pl.program_id(1)
    start = qb * tq
    t_row = start + lax.broadcasted_iota(jnp.int32, (1, tq), 1)

    for src, dst, scale in zip(cast_src, cast_dst, cast_scales):
        w = src[...]
        dst[...] = (w if scale == 1.0 else w * scale).astype(BF16)

    @pl.when(qb == 0)
    def _():
        for r in range(0, seq, ch):
            vst_scr[:, r:r + ch] = vs_ref[r:r + ch, :].T
            vwt_scr[:, r:r + ch] = vw_ref[r:r + ch, :].T

    def heads_on_lanes(ref):
        return jnp.concatenate([ref[g].T for g in range(g_)], axis=1)

    def tile_heads(x, n):
        return jnp.concatenate([x] * n, axis=1)

    def softmax_keys(s2):
        m = jnp.max(s2, axis=0, keepdims=True)
        m = jnp.where(m == -jnp.inf, 0.0, m)
        e = jnp.exp2(s2 - m)
        return e, 1.0 / jnp.maximum(jnp.sum(e, axis=0, keepdims=True), 1e-30)

    qr_t = heads_on_lanes(qr_ref)
    qo_t = heads_on_lanes(qo_ref)

    n_end = lax.broadcasted_iota(jnp.int32, (nc, tq), 0) * CMP_STRIDE + (CMP_BLOCK - 1)
    bias_c = jnp.where(n_end <= t_row, 0.0, -jnp.inf)
    e_c, r_c = softmax_keys(_dot(kc_ref[0, 0], qr_t) + tile_heads(bias_c, g_))
    p_c = e_c * r_c
    o_c = _dot(vc_ref[0, 0].T, p_c.astype(BF16))

    w0 = pl.multiple_of(jnp.maximum(start - WINDOW, 0), tq)
    kpos = w0 + lax.broadcasted_iota(jnp.int32, (wlen, tq), 0)
    bias_w = jnp.where((kpos <= t_row) & (kpos > t_row - WINDOW), 0.0, -jnp.inf)
    e_w, r_w = softmax_keys(_dot(kw_ref[pl.ds(w0, wlen), :], qo_t) + tile_heads(bias_w, g_))
    o_w = _dot(vwt_scr[:, pl.ds(w0, wlen)], e_w.astype(BF16)) * r_w

    p_sum = p_c[:, :tq]
    for g in range(1, g_):
        p_sum = p_sum + p_c[:, g * tq:(g + 1) * tq]
    n_i = lax.broadcasted_iota(jnp.int32, (nsel, nc), 1) * CMP_STRIDE
    s_i = lax.broadcasted_iota(jnp.int32, (nsel, nc), 0) * SEL_BLOCK
    overlap = jnp.where((n_i < s_i + SEL_BLOCK) & (n_i + CMP_BLOCK > s_i), 1.0, 0.0).astype(BF16)
    p_hi = p_sum.astype(BF16)
    p_r1 = p_sum - p_hi.astype(F32)
    p_mid = p_r1.astype(BF16)
    p_lo = (p_r1 - p_mid.astype(F32)).astype(BF16)
    imp = _dot(overlap, p_hi) + _dot(overlap, p_mid) + _dot(overlap, p_lo)

    ids = lax.broadcasted_iota(jnp.int32, (nsel, tq), 0)
    cur = t_row >> SEL_SHIFT
    forced = (ids == 0) | (ids == cur) | (ids == cur - 1)
    imp = jnp.where(forced, FORCE_SCORE, imp)
    work = jnp.where(ids * SEL_BLOCK <= t_row, imp, -1.0)

    sel = jnp.zeros((nsel, tq), F32)
    for _ in range(min(SEL_TOPK, nsel)):
        mx = jnp.max(work, axis=0, keepdims=True)
        first = jnp.min(jnp.where(work == mx, ids, nsel), axis=0, keepdims=True)
        pick = ids == first
        sel = jnp.where(pick, 1.0, sel)
        work = jnp.where(pick, -3.0e38, work)

    neg_sel = tile_heads(jnp.where(sel > 0.5, 0.0, NEG_BIG).astype(BF16), cw // tq)
    for j in range(n_chain):
        rhs_scr[j] = jnp.concatenate([qo_t[:, j * cw:(j + 1) * cw], neg_sel], axis=0)
    m_scr[...] = jnp.full(m_scr.shape, NEG_BIG, F32)
    l_scr[...] = jnp.zeros(l_scr.shape, F32)
    acc_scr[...] = jnp.zeros(acc_scr.shape, F32)

    def keys_with_block(off):
        return jnp.concatenate([ks_ref[pl.ds(off, ch), :], oh_ref[pl.ds(off, ch), :]], axis=1)

    def fold(off, buf, diag_width=None):
        diagonal = diag_width is not None
        width = diag_width if diagonal else ch
        off = pl.multiple_of(off, ch)
        v_t = vst_scr[:, pl.ds(off, width)]
        state = [(m_scr[j], l_scr[j], acc_scr[j]) for j in range(n_chain)]
        if diagonal:
            key = off + lax.broadcasted_iota(jnp.int32, (width, tq), 0)
            causal = tile_heads(jnp.where(key <= t_row, 0.0, NEG_BIG), cw // tq)
            todo = []
        else:
            k_next = keys_with_block(pl.multiple_of(off + ch, ch))
            todo = list(range(n_chain))
        ahead = 3
        for j in todo[:ahead]:
            s_scr[1 - buf, j] = _dot(k_next, rhs_scr[j])
        new_state = []
        for j in range(n_chain):
            s = s_scr[buf, j, :width]
            if diagonal:
                s = s + causal
            m_old, l_old, acc_old = state[j]
            m_new = jnp.maximum(m_old, jnp.max(s, axis=0, keepdims=True))
            alpha = jnp.exp2(m_old - m_new)
            e = jnp.exp2(s - m_new)
            l_new = alpha * l_old + jnp.sum(e, axis=0, keepdims=True)
            acc_new = alpha * acc_old + _dot(v_t, e.astype(BF16))
            new_state.append((m_new, l_new, acc_new))
            for jn in todo[j + ahead:j + ahead + 1]:
                s_scr[1 - buf, jn] = _dot(k_next, rhs_scr[jn])
        for j in range(n_chain):
            m_scr[j], l_scr[j], acc_scr[j] = new_state[j]

    k_first = keys_with_block(0)
    for j in range(n_chain):
        s_scr[0, j] = _dot(k_first, rhs_scr[j])

    def pair_step(c, carry):
        off = pl.multiple_of(c * 2 * ch, 2 * ch)
        fold(off, 0)
        fold(off + ch, 1)
        return carry

    n_full = (start + tq - 1) // ch
    lax.fori_loop(0, n_full // 2, pair_step, 0)
    rest = pl.multiple_of((n_full // 2) * 2 * ch, 2 * ch)

    tiles_per_chunk = ch // tq
    for rem in range(2 * tiles_per_chunk):
        @pl.when(qb % (2 * tiles_per_chunk) == rem)
        def _(rem=rem):
            diag_width = (rem % tiles_per_chunk + 1) * tq
            if rem // tiles_per_chunk == 1:
                fold(rest, 0)
                fold(rest + ch, 1, diag_width)
            else:
                fold(rest, 0, diag_width)

    gt = gt_ref[0]
    for g in range(g_):
        j = g * tq // cw
        lanes = slice(g * tq - j * cw, g * tq - j * cw + tq)
        o_s = jnp.where(m_scr[j][:, lanes] > SCORE_FLOOR,
                        acc_scr[j][:, lanes] * (1.0 / jnp.maximum(l_scr[j][:, lanes], 1e-30)), 0.0)
        cols = slice(g * tq, (g + 1) * tq)
        o_t = (gt[3 * g:3 * g + 1] * o_c[:, cols] + gt[3 * g + 1:3 * g + 2] * o_s
               + gt[3 * g + 2:3 * g + 3] * o_w[:, cols])
        o_ref[:, g * dk:(g + 1) * dk] = o_t.T.astype(BF16)


def _nsa_attention(raw, rot, plain, cmp_kv, gates, casts):
    s = raw.shape[1]
    hk, g_, dk, tq = NSA_KV_HEADS, NSA_GROUP, NSA_HEAD_DIM, NSA_TQ
    nc = s // CMP_STRIDE
    nsel = s // SEL_BLOCK
    one_hot = (jnp.arange(s)[:, None] >> SEL_SHIFT == jnp.arange(nsel)[None, :]).astype(BF16)
    gates_t = gates[:, :3 * NSA_HEADS].T.reshape(hk, 3 * g_, s)
    qspec = pl.BlockSpec((g_, tq, dk), lambda h, i: (h, i, 0))
    col = lambda base: pl.BlockSpec((None, s, dk), lambda h, i: (base + h, 0, 0))
    cmp_spec = lambda j: pl.BlockSpec((1, 1, nc, dk), lambda h, i: (j, h, 0, 0))
    cw = NSA_CHAIN_LANES
    assert cw % tq == 0 and (g_ * tq) % cw == 0 and NSA_SEL_CHUNK % tq == 0
    n_chain = g_ * tq // cw
    n_q = s // tq
    n_prog = hk * n_q
    cast_args, cast_in, cast_out, cast_shapes = [], [], [], []
    for w, first, count, _ in casts:
        rows, cols = count * w.shape[1] // n_prog, w.shape[-1]
        base = first * w.shape[1] // rows
        cast_args.append(w.reshape(-1, cols))
        cast_in.append(pl.BlockSpec((rows, cols), lambda h, i, base=base: (base + h * n_q + i, 0)))
        cast_out.append(pl.BlockSpec((rows, cols), lambda h, i: (h * n_q + i, 0)))
        cast_shapes.append(jax.ShapeDtypeStruct((rows * n_prog, cols), BF16))
    outs = pl.pallas_call(
        functools.partial(_nsa_attn_kernel, seq=s, cast_scales=tuple(c[3] for c in casts)),
        name="nsa_attention",
        grid=(hk, n_q),
        in_specs=[qspec, qspec, cmp_spec(0), cmp_spec(1),
                  col(NSA_HEADS),
                  pl.BlockSpec((s, nsel), lambda h, i: (0, 0)),
                  col(2 * hk),
                  col(NSA_HEADS + hk), col(3 * hk),
                  pl.BlockSpec((1, 3 * g_, tq), lambda h, i: (h, 0, i))] + cast_in,
        out_specs=[pl.BlockSpec((tq, g_ * dk), lambda h, i: (i, h))] + cast_out,
        out_shape=[jax.ShapeDtypeStruct((s, NSA_HEADS * dk), BF16)] + cast_shapes,
        scratch_shapes=[pltpu.VMEM((dk, s), BF16), pltpu.VMEM((dk, s), BF16),
                        pltpu.VMEM((n_chain, dk + nsel, cw), BF16),
                        pltpu.VMEM((2, n_chain, min(NSA_SEL_CHUNK, s), cw), F32),
                        pltpu.VMEM((n_chain, 1, cw), F32),
                        pltpu.VMEM((n_chain, 1, cw), F32),
                        pltpu.VMEM((n_chain, dk, cw), F32)],
        compiler_params=_params("parallel", "arbitrary"),
    )(raw, rot, cmp_kv, cmp_kv, rot, one_hot, plain, rot, plain, gates_t, *cast_args)
    return outs[0], [c.reshape(count, *w.shape[1:]) for c, (w, _, count, _) in zip(outs[1:], casts)]


def _diff_attn_kernel(q_ref, k_ref, v_ref, lam_ref, sg_ref, o_ref, qq_scr, vt_scr, s_scr, m_scr,
                      l_scr, acc_scr, *, lambda_init):
    g_, tq, d = DIFF_GROUP, DIFF_TQ, DIFF_HEAD_DIM
    n_chain = 2 * g_
    qb = pl.program_id(1)

    @pl.when(qb == 0)
    def _():
        rows = 2 * tq
        for r in range(0, v_ref.shape[0], rows):
            vt_scr[:, r:r + rows] = v_ref[r:r + rows, :].T

    first_half = lax.broadcasted_iota(jnp.int32, (2 * d, 1), 0) < d
    for g in range(g_):
        q_t = q_ref[g].T
        zero = jnp.zeros_like(q_t)
        qq_scr[2 * g] = jnp.where(first_half, q_t, zero)
        qq_scr[2 * g + 1] = jnp.where(first_half, zero, q_t)
    m_scr[...] = jnp.full(m_scr.shape, NEG_BIG, F32)
    l_scr[...] = jnp.zeros(l_scr.shape, F32)
    acc_scr[...] = jnp.zeros(acc_scr.shape, F32)

    ch = 2 * tq
    n_full = qb // 2

    def fold(off, buf, diag_width=None):
        diagonal = diag_width is not None
        width = diag_width if diagonal else ch
        off = pl.multiple_of(off, ch)
        v_t = vt_scr[:, pl.ds(off, width)]
        state = [(m_scr[j], l_scr[j], acc_scr[j]) for j in range(n_chain)]
        if diagonal:
            key = lax.broadcasted_iota(jnp.int32, (width, tq), 0) + (off - qb * tq)
            causal = key <= lax.broadcasted_iota(jnp.int32, (width, tq), 1)
            todo = []
        else:
            k_next = k_ref[pl.ds(pl.multiple_of(off + ch, ch), ch), :]
            todo = list(range(n_chain))
        ahead = 3
        for j in todo[:ahead]:
            s_scr[1 - buf, j] = _dot(k_next, qq_scr[j])
        new_state = []
        for j in range(n_chain):
            s = s_scr[buf, j, :width]
            if diagonal:
                s = jnp.where(causal, s, NEG_BIG)
            m_old, l_old, acc_old = state[j]
            m_new = jnp.maximum(m_old, jnp.max(s, axis=0, keepdims=True))
            alpha = jnp.exp2(m_old - m_new)
            p = jnp.exp2(s - m_new)
            l_new = alpha * l_old + jnp.sum(p, axis=0, keepdims=True)
            acc_new = alpha * acc_old + _dot(v_t, p.astype(BF16))
            new_state.append((m_new, l_new, acc_new))
            for jn in todo[j + ahead:j + ahead + 1]:
                s_scr[1 - buf, jn] = _dot(k_next, qq_scr[jn])
        for j in range(n_chain):
            m_scr[j], l_scr[j], acc_scr[j] = new_state[j]

    k_first = k_ref[:ch, :]
    for j in range(n_chain):
        s_scr[0, j] = _dot(k_first, qq_scr[j])

    def pair_step(c, carry):
        off = pl.multiple_of(c * 2 * ch, 2 * ch)
        fold(off, 0)
        fold(off + ch, 1)
        return carry

    lax.fori_loop(0, n_full // 2, pair_step, 0)
    rest = pl.multiple_of((n_full // 2) * 2 * ch, 2 * ch)

    for rem in range(4):
        @pl.when(qb % 4 == rem)
        def _(rem=rem):
            diag_width = ch if rem % 2 == 1 else tq
            if rem // 2 == 1:
                fold(rest, 0)
                fold(rest + ch, 1, diag_width)
            else:
                fold(rest, 0, diag_width)

    lv = lam_ref[...]
    lam = (jnp.exp(jnp.sum(lv[0:1] * lv[1:2], axis=-1, keepdims=True))
           - jnp.exp(jnp.sum(lv[2:3] * lv[3:4], axis=-1, keepdims=True)) + lambda_init)
    sub_g = sg_ref[...]
    for g in range(g_):
        o1 = acc_scr[2 * g] / jnp.maximum(l_scr[2 * g], 1e-30)
        o2 = acc_scr[2 * g + 1] / jnp.maximum(l_scr[2 * g + 1], 1e-30)
        a = o1 - lam * o2
        ms = jnp.mean(a * a, axis=0, keepdims=True)
        a = a * lax.rsqrt(ms + NORM_EPS) * sub_g * (1.0 - lambda_init)
        o_ref[:, g * 2 * d:(g + 1) * 2 * d] = a.T.astype(BF16)


def _diff_attention(q_rot, k_rot, v, lam_vecs, subln_g, lambda_init):
    s = q_rot.shape[1]
    hk, g_, d, tq = DIFF_KV_HEADS, DIFF_GROUP, DIFF_HEAD_DIM, DIFF_TQ
    return pl.pallas_call(
        functools.partial(_diff_attn_kernel, lambda_init=lambda_init),
        name="diff_attention",
        grid=(hk, s // tq),
        in_specs=[pl.BlockSpec((g_, tq, 2 * d), lambda h, i: (h, i, 0)),
                  pl.BlockSpec((None, s, 2 * d), lambda h, i: (h, 0, 0)),
                  pl.BlockSpec((None, s, 2 * d), lambda h, i: (hk + h, 0, 0)),
                  pl.BlockSpec((4, d), lambda h, i: (0, 0)),
                  pl.BlockSpec((2 * d, 1), lambda h, i: (0, 0))],
        out_specs=pl.BlockSpec((tq, g_ * 2 * d), lambda h, i: (i, h)),
        out_shape=jax.ShapeDtypeStruct((s, DIFF_HEADS * 2 * d), BF16),
        scratch_shapes=[pltpu.VMEM((2 * g_, 2 * d, tq), BF16),
                        pltpu.VMEM((2 * d, s), BF16),
                        pltpu.VMEM((2, 2 * g_, 2 * tq, tq), F32),
                        pltpu.VMEM((2 * g_, 1, tq), F32),
                        pltpu.VMEM((2 * g_, 1, tq), F32),
                        pltpu.VMEM((2 * g_, 2 * d, tq), F32)],
        compiler_params=_params("parallel", "arbitrary"),
    )(q_rot, k_rot, v, lam_vecs, subln_g.reshape(2 * d, 1))


def _rope_tables(seq, head_dim, reps):
    rot = head_dim // ROPE_FRACTION
    half = rot // 2
    inv = 1.0 / (ROPE_THETA ** (jnp.arange(0, rot, 2, dtype=F32) / rot))
    ang = jnp.arange(seq, dtype=F32)[:, None] * inv[None, :]
    cos, sin = jnp.cos(ang), jnp.sin(ang)
    rest = head_dim - rot
    zeros_h = jnp.zeros((seq, half), F32)
    c = jnp.concatenate([cos, cos, jnp.ones((seq, rest), F32)], axis=-1)
    sa = jnp.concatenate([-sin, zeros_h, jnp.zeros((seq, rest), F32)], axis=-1)
    sb = jnp.concatenate([zeros_h, sin, jnp.zeros((seq, rest), F32)], axis=-1)
    return tuple(jnp.tile(t, (1, reps)) for t in (c, sa, sb)), half


def _nsa_weights(w_in):
    hk, dk = NSA_KV_HEADS, NSA_HEAD_DIM
    qw, kvw = NSA_HEADS * dk, hk * dk
    part = lambda i: w_in[:, :, qw + i * kvw:qw + (i + 1) * kvw]
    w_q = w_in[:, :, :qw] * (dk ** -0.5 * LOG2E)
    w_rope = jnp.concatenate([w_q, part(2), part(4)], axis=2).astype(BF16)
    n_gate = 3 * NSA_HEADS
    w_gate = jnp.pad(w_in[:, :, qw + 6 * kvw:], ((0, 0), (0, 0), (0, LANES - n_gate)))
    w_plain = jnp.concatenate([part(0), part(1), part(3), part(5), w_gate], axis=2).astype(BF16)
    return w_rope, w_plain


def _nsa_mixer(x, norm_g, proj_w, cmp_pos, cmp_w1, cmp_w2, casts, layer, tables, shift):
    s = x.shape[0]
    hk, dk = NSA_KV_HEADS, NSA_HEAD_DIM
    w_rope, w_plain = proj_w
    raw, rot = _norm_proj_rope(x, norm_g, w_rope, tables, widx=layer, shift=shift, want_raw=True)
    plain, gates = _norm_proj_gated(x, norm_g, w_plain, widx=layer)

    nc = s // CMP_STRIDE
    t2 = plain[:2 * hk].reshape(2, hk, nc, CMP_STRIDE * dk)
    cmp_kv = _compress(t2, cmp_pos.reshape(2, 2, CMP_STRIDE * dk), cmp_w1.astype(BF16),
                       cmp_w2.astype(BF16))
    return _nsa_attention(raw, rot, plain, cmp_kv, gates, casts)


def kernel(x, attn_norm_g, mlp_norm_g, final_norm_g, nsa_w_in, nsa_cmp_pos, nsa_cmp_w1, nsa_cmp_w2, nsa_w_out, kv_norm_g, kv_w_shared, diff_w_q, diff_lambda, diff_subln_g, diff_w_out, mlp_w_up, mlp_w_down):
    b, s, d = x.shape
    tables_a, shift_a = _rope_tables(s, NSA_HEAD_DIM, 1)
    tables_b, shift_b = _rope_tables(s, DIFF_HEAD_DIM, 2)
    assert DEPTH == 2 * N_A_LAYERS and N_A_LAYERS == 2
    side_casts = {layer: [(mlp_w_up, 2 * layer, 2, 1.0), (mlp_w_down, 2 * layer, 2, 1.0)]
                  for layer in range(N_A_LAYERS)}
    side_casts[0] += [(nsa_w_out, 0, N_A_LAYERS, 1.0), (diff_w_out, 0, DEPTH - N_A_LAYERS, 1.0),
                      (diff_w_q, 0, DEPTH - N_A_LAYERS, DIFF_HEAD_DIM ** -0.5 * LOG2E)]
    nsa_proj = _nsa_weights(nsa_w_in)
    outs = []
    for bi in range(b):
        xs = x[bi]
        k_sh = v_sh = None
        mixer_w = {}
        for layer in range(DEPTH):
            if layer < N_A_LAYERS:
                o, cast = _nsa_mixer(xs, attn_norm_g[layer], nsa_proj, nsa_cmp_pos[layer],
                                     nsa_cmp_w1[layer], nsa_cmp_w2[layer], side_casts[layer],
                                     layer, tables_a, shift_a)
                mixer_w[layer] = cast[:2]
                if layer == 0:
                    nsa_out, diff_out, diff_q = cast[2:]
                xs = _matmul_res(o, nsa_out, xs, widx=layer)
            else:
                j = layer - N_A_LAYERS
                if j == 0:
                    v_sh, k_sh = _norm_proj_rope(xs, kv_norm_g, kv_w_shared.astype(BF16), tables_b,
                                                 shift=shift_b, want_raw=True)
                lambda_init = 0.8 - 0.6 * math.exp(-0.3 * layer)
                q = _norm_proj_rope(xs, attn_norm_g[layer], diff_q, tables_b, widx=j,
                                    shift=shift_b, want_raw=False)
                o = _diff_attention(q, k_sh, v_sh, diff_lambda[j], diff_subln_g[j], lambda_init)
                xs = _matmul_res(o, diff_out, xs, widx=j)
            w_up, w_down = mixer_w[layer // 2]
            xs = _mlp(xs, mlp_norm_g[layer], w_up, w_down, layer % 2, final_norm_g,
                      final_norm=(layer == DEPTH - 1))
        outs.append(xs[None])
    return outs[0] if b == 1 else jnp.concatenate(outs, axis=0)
```

```python
import functools
import math

import jax
import jax.numpy as jnp
from jax import lax
from jax.experimental import pallas as pl
from jax.experimental.pallas import tpu as pltpu

F32 = jnp.float32
BF16 = jnp.bfloat16

D_MODEL = 2048
DEPTH = 4
N_A_LAYERS = DEPTH // 2

NSA_HEADS = 16
NSA_KV_HEADS = 4
NSA_GROUP = NSA_HEADS // NSA_KV_HEADS
NSA_HEAD_DIM = D_MODEL // NSA_HEADS
CMP_BLOCK = 32
CMP_STRIDE = 16
CMP_HIDDEN = 4 * NSA_HEAD_DIM
SEL_BLOCK = 64
SEL_SHIFT = SEL_BLOCK.bit_length() - 1
SEL_TOPK = 16
WINDOW = 512
FORCE_SCORE = 1.0e4

DIFF_HEADS = 16
DIFF_KV_HEADS = 4
DIFF_GROUP = DIFF_HEADS // DIFF_KV_HEADS
DIFF_HEAD_DIM = D_MODEL // (2 * DIFF_HEADS)

D_FF = 4 * D_MODEL
ROPE_THETA = 500000.0
ROPE_FRACTION = 4
NORM_EPS = 1e-6

LANES = 128
VMEM_LIMIT = 48 * 1024 * 1024
VMEM_BUDGET = 40 * 1024 * 1024
NEG_BIG = -1.0e30
SCORE_FLOOR = -1.0e28
LOG2E = 1.4426950408889634

NSA_TQ = 256
NSA_SEL_CHUNK = 512
NSA_CHAIN_LANES = 256
DIFF_TQ = 256
ROW_TILE = 512
COL_TILE = 512
FF_TILE = 1024


def _params(*sem):
    return pltpu.CompilerParams(dimension_semantics=sem, vmem_limit_bytes=VMEM_LIMIT)


def _dot(a, b):
    return jnp.dot(a, b, preferred_element_type=F32)


def _rms(x, g):
    ms = jnp.mean(x * x, axis=-1, keepdims=True)
    return x * lax.rsqrt(ms + NORM_EPS) * g


def _proj_tiles(s, k, n, row_bytes):
    weight = 2 * k * n * 2
    for tm in (ROW_TILE, ROW_TILE // 2):
        if weight + tm * row_bytes <= VMEM_BUDGET:
            return min(tm, s), n
    return min(ROW_TILE, s), min(COL_TILE, n)


def _weight_spec(w, widx, k, tn):
    if w.ndim == 2:
        return pl.BlockSpec((k, tn), lambda i, j: (0, j))
    return pl.BlockSpec((None, k, tn), lambda i, j: (widx, 0, j))


def _normed(x_ref, g_ref, h_scr, col_tiles):
    if col_tiles == 1:
        return _rms(x_ref[...], g_ref[...]).astype(BF16)

    @pl.when(pl.program_id(1) == 0)
    def _():
        h_scr[...] = _rms(x_ref[...], g_ref[...]).astype(BF16)

    return h_scr[...]


def _store_col_blocks(o_ref, acc):
    for blk in range(acc.shape[1] // LANES):
        o_ref[blk] = acc[:, blk * LANES:(blk + 1) * LANES].astype(o_ref.dtype)


def _blocked_out(s, n, tm, tn, dtype):
    spec = pl.BlockSpec((tn // LANES, tm, LANES), lambda i, j: (j, i, 0))
    return spec, jax.ShapeDtypeStruct((n // LANES, s, LANES), dtype)


def _norm_proj_kernel(x_ref, g_ref, w_ref, o_ref, gate_ref):
    acc = _dot(_rms(x_ref[...], g_ref[...]).astype(BF16), w_ref[...])
    n_main = o_ref.shape[0] * LANES
    _store_col_blocks(o_ref, acc[:, :n_main])
    gate_ref[...] = jax.nn.sigmoid(acc[:, n_main:])


def _norm_proj_gated(x, g, w, *, widx=None, gate_cols=LANES):
    s, d = x.shape
    n_all = w.shape[-1]
    n = n_all - gate_cols
    tm, tn = _proj_tiles(s, d, n_all, 2 * d * 4 + d * 2 + n_all * 4 + 2 * n * 2 + 2 * gate_cols * 4)
    assert tn == n_all, "the gate split needs the whole weight in one column tile"
    out_spec, out_shape = _blocked_out(s, n, tm, n, BF16)
    return pl.pallas_call(
        _norm_proj_kernel,
        name="norm_proj",
        grid=(s // tm, 1),
        in_specs=[pl.BlockSpec((tm, d), lambda i, j: (i, 0)),
                  pl.BlockSpec((1, d), lambda i, j: (0, 0)),
                  _weight_spec(w, widx, d, tn)],
        out_specs=[out_spec, pl.BlockSpec((tm, gate_cols), lambda i, j: (i, 0))],
        out_shape=[out_shape, jax.ShapeDtypeStruct((s, gate_cols), F32)],
        compiler_params=_params("parallel", "arbitrary"),
    )(x, g.reshape(1, d), w)


def _norm_proj_rope_kernel(x_ref, g_ref, w_ref, c_ref, sa_ref, sb_ref, *rest, shift, want_raw,
                           col_tiles):
    if want_raw:
        raw_ref, rot_ref, h_scr = rest
    else:
        rot_ref, h_scr = rest

    acc = _dot(_normed(x_ref, g_ref, h_scr, col_tiles), w_ref[...])
    if want_raw:
        _store_col_blocks(raw_ref, acc)
    c, sa, sb = c_ref[...], sa_ref[...], sb_ref[...]
    for blk in range(acc.shape[1] // LANES):
        a = acc[:, blk * LANES:(blk + 1) * LANES]
        r = a * c + pltpu.roll(a, LANES - shift, 1) * sa + pltpu.roll(a, shift, 1) * sb
        rot_ref[blk] = r.astype(BF16)


def _norm_proj_rope(x, g, w, tables, *, shift, want_raw, widx=None):
    s, d = x.shape
    n = w.shape[-1]
    n_out = 2 if want_raw else 1
    tm, tn = _proj_tiles(s, d, n, 2 * d * 4 + d * 2 + n * 4 + n_out * 2 * n * 2 + 3 * 2 * LANES * 4)
    tab_spec = pl.BlockSpec((tm, LANES), lambda i, j: (i, 0))
    out_spec, out_sd = _blocked_out(s, n, tm, tn, BF16)
    return pl.pallas_call(
        functools.partial(_norm_proj_rope_kernel, shift=shift, want_raw=want_raw,
                          col_tiles=n // tn),
        name="norm_proj_rope",
        grid=(s // tm, n // tn),
        in_specs=[pl.BlockSpec((tm, d), lambda i, j: (i, 0)),
                  pl.BlockSpec((1, d), lambda i, j: (0, 0)),
                  _weight_spec(w, widx, d, tn),
                  tab_spec, tab_spec, tab_spec],
        out_specs=[out_spec, out_spec] if want_raw else out_spec,
        out_shape=[out_sd, out_sd] if want_raw else out_sd,
        scratch_shapes=[pltpu.VMEM((tm, d), BF16)],
        compiler_params=_params("parallel", "arbitrary"),
    )(x, g.reshape(1, d), w, *tables)


def _matmul_res_kernel(a_ref, w_ref, r_ref, o_ref):
    o_ref[...] = r_ref[...] + _dot(a_ref[...], w_ref[...])


def _matmul_res(a, w, res, *, widx=None):
    s, k = a.shape
    n = w.shape[-1]
    tm, tn = _proj_tiles(s, k, n, 2 * k * 2 + 2 * 2 * n * 4 + n * 4)
    return pl.pallas_call(
        _matmul_res_kernel,
        name="matmul_res",
        grid=(s // tm, n // tn),
        in_specs=[pl.BlockSpec((tm, k), lambda i, j: (i, 0)),
                  _weight_spec(w, widx, k, tn),
                  pl.BlockSpec((tm, tn), lambda i, j: (i, j))],
        out_specs=pl.BlockSpec((tm, tn), lambda i, j: (i, j)),
        out_shape=jax.ShapeDtypeStruct((s, n), F32),
        compiler_params=_params("parallel", "arbitrary"),
    )(a, w, res)


def _mlp_kernel(x_ref, g_ref, wu_ref, wd_ref, fg_ref, o_ref, h_scr, acc_scr, *, final_norm):
    j = pl.program_id(1)

    @pl.when(j == 0)
    def _():
        h_scr[...] = _rms(x_ref[...], g_ref[...]).astype(BF16)
        acc_scr[...] = jnp.zeros_like(acc_scr)

    u = _dot(h_scr[...], wu_ref[...])
    u = jnp.square(jnp.maximum(u, 0.0)).astype(BF16)
    acc_scr[...] += _dot(u, wd_ref[...])

    @pl.when(j == pl.num_programs(1) - 1)
    def _():
        y = x_ref[...] + acc_scr[...]
        if final_norm:
            y = _rms(y, fg_ref[...])
        o_ref[...] = y


def _mlp(x, g, w_up, w_down, layer, final_g, *, final_norm):
    s, d = x.shape
    ff = w_up.shape[-1]
    tm, tf = min(ROW_TILE, s), min(FF_TILE, ff)
    return pl.pallas_call(
        functools.partial(_mlp_kernel, final_norm=final_norm),
        name="sq_relu_mlp",
        grid=(s // tm, ff // tf),
        in_specs=[pl.BlockSpec((tm, d), lambda i, j: (i, 0)),
                  pl.BlockSpec((1, d), lambda i, j: (0, 0)),
                  pl.BlockSpec((None, d, tf), lambda i, j: (layer, 0, j)),
                  pl.BlockSpec((None, tf, d), lambda i, j: (layer, j, 0)),
                  pl.BlockSpec((1, d), lambda i, j: (0, 0))],
        out_specs=pl.BlockSpec((tm, d), lambda i, j: (i, 0)),
        out_shape=jax.ShapeDtypeStruct((s, d), F32),
        scratch_shapes=[pltpu.VMEM((tm, d), BF16), pltpu.VMEM((tm, d), F32)],
        compiler_params=_params("parallel", "arbitrary"),
    )(x, g.reshape(1, d), w_up, w_down, final_g.reshape(1, d))


def _compress_kernel(t_ref, pos_ref, w1_ref, w2_ref, o_ref):
    t = t_ref[0, 0].astype(F32)
    pos = pos_ref[0]
    half = t.shape[1]
    a = _dot((t + pos[0:1]).astype(BF16), w1_ref[0, :half, :])
    b = _dot((t + pos[1:2]).astype(BF16), w1_ref[0, half:, :])
    nc = t.shape[0]
    hid = jax.nn.gelu(a + pltpu.roll(b, nc - 1, 0))
    o_ref[0, 0] = _dot(hid.astype(BF16), w2_ref[0]).astype(BF16)


def _compress(t2, pos2, w1, w2):
    _, hk, nc, width = t2.shape
    dk = w2.shape[-1]
    return pl.pallas_call(
        _compress_kernel,
        name="nsa_compress",
        grid=(2, hk),
        in_specs=[pl.BlockSpec((1, 1, nc, width), lambda j, h: (j, h, 0, 0)),
                  pl.BlockSpec((1, 2, width), lambda j, h: (j, 0, 0)),
                  pl.BlockSpec((1, 2 * width, CMP_HIDDEN), lambda j, h: (j, 0, 0)),
                  pl.BlockSpec((1, CMP_HIDDEN, dk), lambda j, h: (j, 0, 0))],
        out_specs=pl.BlockSpec((1, 1, nc, dk), lambda j, h: (j, h, 0, 0)),
        out_shape=jax.ShapeDtypeStruct((2, hk, nc, dk), BF16),
        compiler_params=_params("parallel", "arbitrary"),
    )(t2, pos2, w1, w2)


def _nsa_attn_kernel(qr_ref, qo_ref, kc_ref, vc_ref, ks_ref, oh_ref, vs_ref, kw_ref, vw_ref, gt_ref,
                     *refs, seq, cast_scales):
    n_cast = len(cast_scales)
    cast_src, o_ref, cast_dst = refs[:n_cast], refs[n_cast], refs[n_cast + 1:2 * n_cast + 1]
    vst_scr, vwt_scr, rhs_scr, s_scr, m_scr, l_scr, acc_scr = refs[2 * n_cast + 1:]
    g_, tq, dk = NSA_GROUP, NSA_TQ, NSA_HEAD_DIM
    cw = NSA_CHAIN_LANES
    n_chain = g_ * tq // cw
    nc = seq // CMP_STRIDE
    nsel = seq // SEL_BLOCK
    ch = min(NSA_SEL_CHUNK, seq)
    wlen = WINDOW + tq
    qb = pl.program_id(1)
    start = qb * tq
    t_row = start + lax.broadcasted_iota(jnp.int32, (1, tq), 1)

    for src, dst, scale in zip(cast_src, cast_dst, cast_scales):
        w = src[...]
        dst[...] = (w if scale == 1.0 else w * scale).astype(BF16)

    @pl.when(qb == 0)
    def _():
        for r in range(0, seq, ch):
            vst_scr[:, r:r + ch] = vs_ref[r:r + ch, :].T
            vwt_scr[:, r:r + ch] = vw_ref[r:r + ch, :].T

    def heads_on_lanes(ref):
        return jnp.concatenate([ref[g].T for g in range(g_)], axis=1)

    def tile_heads(x, n):
        return jnp.concatenate([x] * n, axis=1)

    def softmax_keys(s2):
        m = jnp.max(s2, axis=0, keepdims=True)
        m = jnp.where(m == -jnp.inf, 0.0, m)
        e = jnp.exp2(s2 - m)
        return e, 1.0 / jnp.maximum(jnp.sum(e, axis=0, keepdims=True), 1e-30)

    qr_t = heads_on_lanes(qr_ref)
    qo_t = heads_on_lanes(qo_ref)

    n_end = lax.broadcasted_iota(jnp.int32, (nc, tq), 0) * CMP_STRIDE + (CMP_BLOCK - 1)
    bias_c = jnp.where(n_end <= t_row, 0.0, -jnp.inf)
    w0 = pl.multiple_of(jnp.maximum(start - WINDOW, 0), tq)
    kpos = w0 + lax.broadcasted_iota(jnp.int32, (wlen, tq), 0)
    bias_w = jnp.where((kpos <= t_row) & (kpos > t_row - WINDOW), 0.0, -jnp.inf)

    head = lambda x, g: x[:, g * tq:(g + 1) * tq]
    kc, vc_t = kc_ref[0, 0], vc_ref[0, 0].T
    kw, vw_t = kw_ref[pl.ds(w0, wlen), :], vwt_scr[:, pl.ds(w0, wlen)]
    scores = [lambda g=g: _dot(kc, head(qr_t, g)) for g in range(g_)]
    scores += [lambda g=g: _dot(kw, head(qo_t, g)) for g in range(g_)]
    ahead = 2
    ready = [f() for f in scores[:ahead]]
    o_c, o_w, p_sum = [], [], None
    for i in range(2 * g_):
        if i + ahead < 2 * g_:
            ready.append(scores[i + ahead]())
        if i < g_:
            e_c, r_c = softmax_keys(ready[i] + bias_c)
            p_c = e_c * r_c
            o_c.append(_dot(vc_t, p_c.astype(BF16)))
            p_sum = p_c if p_sum is None else p_sum + p_c
        else:
            e_w, r_w = softmax_keys(ready[i] + bias_w)
            o_w.append(_dot(vw_t, e_w.astype(BF16)) * r_w)

    n_i = lax.broadcasted_iota(jnp.int32, (nsel, nc), 1) * CMP_STRIDE
    s_i = lax.broadcasted_iota(jnp.int32, (nsel, nc), 0) * SEL_BLOCK
    overlap = jnp.where((n_i < s_i + SEL_BLOCK) & (n_i + CMP_BLOCK > s_i), 1.0, 0.0).astype(BF16)
    p_hi = p_sum.astype(BF16)
    p_r1 = p_sum - p_hi.astype(F32)
    p_mid = p_r1.astype(BF16)
    p_lo = (p_r1 - p_mid.astype(F32)).astype(BF16)
    imp = _dot(overlap, p_hi) + _dot(overlap, p_mid) + _dot(overlap, p_lo)

    ids = lax.broadcasted_iota(jnp.int32, (nsel, tq), 0)
    cur = t_row >> SEL_SHIFT
    forced = (ids == 0) | (ids == cur) | (ids == cur - 1)
    imp = jnp.where(forced, FORCE_SCORE, imp)
    work = jnp.where(ids * SEL_BLOCK <= t_row, imp, -1.0)

    sel = jnp.zeros((nsel, tq), F32)
    for _ in range(min(SEL_TOPK, nsel)):
        mx = jnp.max(work, axis=0, keepdims=True)
        first = jnp.min(jnp.where(work == mx, ids, nsel), axis=0, keepdims=True)
        pick = ids == first
        sel = jnp.where(pick, 1.0, sel)
        work = jnp.where(pick, -3.0e38, work)

    neg_sel = tile_heads(jnp.where(sel > 0.5, 0.0, NEG_BIG).astype(BF16), cw // tq)
    for j in range(n_chain):
        rhs_scr[j] = jnp.concatenate([qo_t[:, j * cw:(j + 1) * cw], neg_sel], axis=0)
    m_scr[...] = jnp.full(m_scr.shape, NEG_BIG, F32)
    l_scr[...] = jnp.zeros(l_scr.shape, F32)
    acc_scr[...] = jnp.zeros(acc_scr.shape, F32)

    def keys_with_block(off):
        return jnp.concatenate([ks_ref[pl.ds(off, ch), :], oh_ref[pl.ds(off, ch), :]], axis=1)

    def fold(off, buf, diag_width=None):
        diagonal = diag_width is not None
        width = diag_width if diagonal else ch
        off = pl.multiple_of(off, ch)
        v_t = vst_scr[:, pl.ds(off, width)]
        state = [(m_scr[j], l_scr[j], acc_scr[j]) for j in range(n_chain)]
        if diagonal:
            key = off + lax.broadcasted_iota(jnp.int32, (width, tq), 0)
            causal = tile_heads(jnp.where(key <= t_row, 0.0, NEG_BIG), cw // tq)
            todo = []
        else:
            k_next = keys_with_block(pl.multiple_of(off + ch, ch))
            todo = list(range(n_chain))
        ahead = 3
        for j in todo[:ahead]:
            s_scr[1 - buf, j] = _dot(k_next, rhs_scr[j])
        new_state = []
        for j in range(n_chain):
            s = s_scr[buf, j, :width]
            if diagonal:
                s = s + causal
            m_old, l_old, acc_old = state[j]
            m_new = jnp.maximum(m_old, jnp.max(s, axis=0, keepdims=True))
            alpha = jnp.exp2(m_old - m_new)
            e = jnp.exp2(s - m_new)
            l_new = alpha * l_old + jnp.sum(e, axis=0, keepdims=True)
            acc_new = alpha * acc_old + _dot(v_t, e.astype(BF16))
            new_state.append((m_new, l_new, acc_new))
            for jn in todo[j + ahead:j + ahead + 1]:
                s_scr[1 - buf, jn] = _dot(k_next, rhs_scr[jn])
        for j in range(n_chain):
            m_scr[j], l_scr[j], acc_scr[j] = new_state[j]

    k_first = keys_with_block(0)
    for j in range(n_chain):
        s_scr[0, j] = _dot(k_first, rhs_scr[j])

    def pair_step(c, carry):
        off = pl.multiple_of(c * 2 * ch, 2 * ch)
        fold(off, 0)
        fold(off + ch, 1)
        return carry

    n_full = (start + tq - 1) // ch
    lax.fori_loop(0, n_full // 2, pair_step, 0)
    rest = pl.multiple_of((n_full // 2) * 2 * ch, 2 * ch)

    tiles_per_chunk = ch // tq
    for rem in range(2 * tiles_per_chunk):
        @pl.when(qb % (2 * tiles_per_chunk) == rem)
        def _(rem=rem):
            diag_width = (rem % tiles_per_chunk + 1) * tq
            if rem // tiles_per_chunk == 1:
                fold(rest, 0)
                fold(rest + ch, 1, diag_width)
            else:
                fold(rest, 0, diag_width)

    gt = gt_ref[0]
    for g in range(g_):
        j = g * tq // cw
        lanes = slice(g * tq - j * cw, g * tq - j * cw + tq)
        o_s = jnp.where(m_scr[j][:, lanes] > SCORE_FLOOR,
                        acc_scr[j][:, lanes] * (1.0 / jnp.maximum(l_scr[j][:, lanes], 1e-30)), 0.0)
        o_t = (gt[3 * g:3 * g + 1] * o_c[g] + gt[3 * g + 1:3 * g + 2] * o_s
               + gt[3 * g + 2:3 * g + 3] * o_w[g])
        o_ref[:, g * dk:(g + 1) * dk] = o_t.T.astype(BF16)


def _nsa_attention(raw, rot, plain, cmp_kv, gates, casts):
    s = raw.shape[1]
    hk, g_, dk, tq = NSA_KV_HEADS, NSA_GROUP, NSA_HEAD_DIM, NSA_TQ
    nc = s // CMP_STRIDE
    nsel = s // SEL_BLOCK
    one_hot = (jnp.arange(s)[:, None] >> SEL_SHIFT == jnp.arange(nsel)[None, :]).astype(BF16)
    gates_t = gates[:, :3 * NSA_HEADS].T.reshape(hk, 3 * g_, s)
    qspec = pl.BlockSpec((g_, tq, dk), lambda h, i: (h, i, 0))
    col = lambda base: pl.BlockSpec((None, s, dk), lambda h, i: (base + h, 0, 0))
    cmp_spec = lambda j: pl.BlockSpec((1, 1, nc, dk), lambda h, i: (j, h, 0, 0))
    cw = NSA_CHAIN_LANES
    assert cw % tq == 0 and (g_ * tq) % cw == 0 and NSA_SEL_CHUNK % tq == 0
    n_chain = g_ * tq // cw
    n_q = s // tq
    n_prog = hk * n_q
    cast_args, cast_in, cast_out, cast_shapes = [], [], [], []
    for w, first, count, _ in casts:
        rows, cols = count * w.shape[1] // n_prog, w.shape[-1]
        base = first * w.shape[1] // rows
        cast_args.append(w.reshape(-1, cols))
        cast_in.append(pl.BlockSpec((rows, cols), lambda h, i, base=base: (base + h * n_q + i, 0)))
        cast_out.append(pl.BlockSpec((rows, cols), lambda h, i: (h * n_q + i, 0)))
        cast_shapes.append(jax.ShapeDtypeStruct((rows * n_prog, cols), BF16))
    outs = pl.pallas_call(
        functools.partial(_nsa_attn_kernel, seq=s, cast_scales=tuple(c[3] for c in casts)),
        name="nsa_attention",
        grid=(hk, n_q),
        in_specs=[qspec, qspec, cmp_spec(0), cmp_spec(1),
                  col(NSA_HEADS),
                  pl.BlockSpec((s, nsel), lambda h, i: (0, 0)),
                  col(2 * hk),
                  col(NSA_HEADS + hk), col(3 * hk),
                  pl.BlockSpec((1, 3 * g_, tq), lambda h, i: (h, 0, i))] + cast_in,
        out_specs=[pl.BlockSpec((tq, g_ * dk), lambda h, i: (i, h))] + cast_out,
        out_shape=[jax.ShapeDtypeStruct((s, NSA_HEADS * dk), BF16)] + cast_shapes,
        scratch_shapes=[pltpu.VMEM((dk, s), BF16), pltpu.VMEM((dk, s), BF16),
                        pltpu.VMEM((n_chain, dk + nsel, cw), BF16),
                        pltpu.VMEM((2, n_chain, min(NSA_SEL_CHUNK, s), cw), F32),
                        pltpu.VMEM((n_chain, 1, cw), F32),
                        pltpu.VMEM((n_chain, 1, cw), F32),
                        pltpu.VMEM((n_chain, dk, cw), F32)],
        compiler_params=_params("parallel", "arbitrary"),
    )(raw, rot, cmp_kv, cmp_kv, rot, one_hot, plain, rot, plain, gates_t, *cast_args)
    return outs[0], [c.reshape(count, *w.shape[1:]) for c, (w, _, count, _) in zip(outs[1:], casts)]


def _diff_attn_kernel(q_ref, k_ref, v_ref, lam_ref, sg_ref, o_ref, qq_scr, vt_scr, s_scr, m_scr,
                      l_scr, acc_scr, *, lambda_init):
    g_, tq, d = DIFF_GROUP, DIFF_TQ, DIFF_HEAD_DIM
    n_chain = 2 * g_
    qb = pl.program_id(1)

    @pl.when(qb == 0)
    def _():
        rows = 2 * tq
        for r in range(0, v_ref.shape[0], rows):
            vt_scr[:, r:r + rows] = v_ref[r:r + rows, :].T

    first_half = lax.broadcasted_iota(jnp.int32, (2 * d, 1), 0) < d
    for g in range(g_):
        q_t = q_ref[g].T
        zero = jnp.zeros_like(q_t)
        qq_scr[2 * g] = jnp.where(first_half, q_t, zero)
        qq_scr[2 * g + 1] = jnp.where(first_half, zero, q_t)
    m_scr[...] = jnp.full(m_scr.shape, NEG_BIG, F32)
    l_scr[...] = jnp.zeros(l_scr.shape, F32)
    acc_scr[...] = jnp.zeros(acc_scr.shape, F32)

    ch = 2 * tq
    n_full = qb // 2

    def fold(off, buf, diag_width=None):
        diagonal = diag_width is not None
        width = diag_width if diagonal else ch
        off = pl.multiple_of(off, ch)
        v_t = vt_scr[:, pl.ds(off, width)]
        state = [(m_scr[j], l_scr[j], acc_scr[j]) for j in range(n_chain)]
        if diagonal:
            key = lax.broadcasted_iota(jnp.int32, (width, tq), 0) + (off - qb * tq)
            causal = key <= lax.broadcasted_iota(jnp.int32, (width, tq), 1)
            todo = []
        else:
            k_next = k_ref[pl.ds(pl.multiple_of(off + ch, ch), ch), :]
            todo = list(range(n_chain))
        ahead = 3
        for j in todo[:ahead]:
            s_scr[1 - buf, j] = _dot(k_next, qq_scr[j])
        new_state = []
        for j in range(n_chain):
            s = s_scr[buf, j, :width]
            if diagonal:
                s = jnp.where(causal, s, NEG_BIG)
            m_old, l_old, acc_old = state[j]
            m_new = jnp.maximum(m_old, jnp.max(s, axis=0, keepdims=True))
            alpha = jnp.exp2(m_old - m_new)
            p = jnp.exp2(s - m_new)
            l_new = alpha * l_old + jnp.sum(p, axis=0, keepdims=True)
            acc_new = alpha * acc_old + _dot(v_t, p.astype(BF16))
            new_state.append((m_new, l_new, acc_new))
            for jn in todo[j + ahead:j + ahead + 1]:
                s_scr[1 - buf, jn] = _dot(k_next, qq_scr[jn])
        for j in range(n_chain):
            m_scr[j], l_scr[j], acc_scr[j] = new_state[j]

    k_first = k_ref[:ch, :]
    for j in range(n_chain):
        s_scr[0, j] = _dot(k_first, qq_scr[j])

    def pair_step(c, carry):
        off = pl.multiple_of(c * 2 * ch, 2 * ch)
        fold(off, 0)
        fold(off + ch, 1)
        return carry

    lax.fori_loop(0, n_full // 2, pair_step, 0)
    rest = pl.multiple_of((n_full // 2) * 2 * ch, 2 * ch)

    for rem in range(4):
        @pl.when(qb % 4 == rem)
        def _(rem=rem):
            diag_width = ch if rem % 2 == 1 else tq
            if rem // 2 == 1:
                fold(rest, 0)
                fold(rest + ch, 1, diag_width)
            else:
                fold(rest, 0, diag_width)

    lv = lam_ref[...]
    lam = (jnp.exp(jnp.sum(lv[0:1] * lv[1:2], axis=-1, keepdims=True))
           - jnp.exp(jnp.sum(lv[2:3] * lv[3:4], axis=-1, keepdims=True)) + lambda_init)
    sub_g = sg_ref[...]
    for g in range(g_):
        o1 = acc_scr[2 * g] / jnp.maximum(l_scr[2 * g], 1e-30)
        o2 = acc_scr[2 * g + 1] / jnp.maximum(l_scr[2 * g + 1], 1e-30)
        a = o1 - lam * o2
        ms = jnp.mean(a * a, axis=0, keepdims=True)
        a = a * lax.rsqrt(ms + NORM_EPS) * sub_g * (1.0 - lambda_init)
        o_ref[:, g * 2 * d:(g + 1) * 2 * d] = a.T.astype(BF16)


def _diff_attention(q_rot, k_rot, v, lam_vecs, subln_g, lambda_init):
    s = q_rot.shape[1]
    hk, g_, d, tq = DIFF_KV_HEADS, DIFF_GROUP, DIFF_HEAD_DIM, DIFF_TQ
    return pl.pallas_call(
        functools.partial(_diff_attn_kernel, lambda_init=lambda_init),
        name="diff_attention",
        grid=(hk, s // tq),
        in_specs=[pl.BlockSpec((g_, tq, 2 * d), lambda h, i: (h, i, 0)),
                  pl.BlockSpec((None, s, 2 * d), lambda h, i: (h, 0, 0)),
                  pl.BlockSpec((None, s, 2 * d), lambda h, i: (hk + h, 0, 0)),
                  pl.BlockSpec((4, d), lambda h, i: (0, 0)),
                  pl.BlockSpec((2 * d, 1), lambda h, i: (0, 0))],
        out_specs=pl.BlockSpec((tq, g_ * 2 * d), lambda h, i: (i, h)),
        out_shape=jax.ShapeDtypeStruct((s, DIFF_HEADS * 2 * d), BF16),
        scratch_shapes=[pltpu.VMEM((2 * g_, 2 * d, tq), BF16),
                        pltpu.VMEM((2 * d, s), BF16),
                        pltpu.VMEM((2, 2 * g_, 2 * tq, tq), F32),
                        pltpu.VMEM((2 * g_, 1, tq), F32),
                        pltpu.VMEM((2 * g_, 1, tq), F32),
                        pltpu.VMEM((2 * g_, 2 * d, tq), F32)],
        compiler_params=_params("parallel", "arbitrary"),
    )(q_rot, k_rot, v, lam_vecs, subln_g.reshape(2 * d, 1))


def _rope_tables(seq, head_dim, reps):
    rot = head_dim // ROPE_FRACTION
    half = rot // 2
    inv = 1.0 / (ROPE_THETA ** (jnp.arange(0, rot, 2, dtype=F32) / rot))
    ang = jnp.arange(seq, dtype=F32)[:, None] * inv[None, :]
    cos, sin = jnp.cos(ang), jnp.sin(ang)
    rest = head_dim - rot
    zeros_h = jnp.zeros((seq, half), F32)
    c = jnp.concatenate([cos, cos, jnp.ones((seq, rest), F32)], axis=-1)
    sa = jnp.concatenate([-sin, zeros_h, jnp.zeros((seq, rest), F32)], axis=-1)
    sb = jnp.concatenate([zeros_h, sin, jnp.zeros((seq, rest), F32)], axis=-1)
    return tuple(jnp.tile(t, (1, reps)) for t in (c, sa, sb)), half


def _nsa_weights(w_in):
    hk, dk = NSA_KV_HEADS, NSA_HEAD_DIM
    qw, kvw = NSA_HEADS * dk, hk * dk
    part = lambda i: w_in[:, :, qw + i * kvw:qw + (i + 1) * kvw]
    w_q = w_in[:, :, :qw] * (dk ** -0.5 * LOG2E)
    w_rope = jnp.concatenate([w_q, part(2), part(4)], axis=2).astype(BF16)
    n_gate = 3 * NSA_HEADS
    w_gate = jnp.pad(w_in[:, :, qw + 6 * kvw:], ((0, 0), (0, 0), (0, LANES - n_gate)))
    w_plain = jnp.concatenate([part(0), part(1), part(3), part(5), w_gate], axis=2).astype(BF16)
    return w_rope, w_plain


def _nsa_mixer(x, norm_g, proj_w, cmp_pos, cmp_w1, cmp_w2, casts, layer, tables, shift):
    s = x.shape[0]
    hk, dk = NSA_KV_HEADS, NSA_HEAD_DIM
    w_rope, w_plain = proj_w
    raw, rot = _norm_proj_rope(x, norm_g, w_rope, tables, widx=layer, shift=shift, want_raw=True)
    plain, gates = _norm_proj_gated(x, norm_g, w_plain, widx=layer)

    nc = s // CMP_STRIDE
    t2 = plain[:2 * hk].reshape(2, hk, nc, CMP_STRIDE * dk)
    cmp_kv = _compress(t2, cmp_pos.reshape(2, 2, CMP_STRIDE * dk), cmp_w1.astype(BF16),
                       cmp_w2.astype(BF16))
    return _nsa_attention(raw, rot, plain, cmp_kv, gates, casts)


def kernel(x, attn_norm_g, mlp_norm_g, final_norm_g, nsa_w_in, nsa_cmp_pos, nsa_cmp_w1, nsa_cmp_w2, nsa_w_out, kv_norm_g, kv_w_shared, diff_w_q, diff_lambda, diff_subln_g, diff_w_out, mlp_w_up, mlp_w_down):
    b, s, d = x.shape
    tables_a, shift_a = _rope_tables(s, NSA_HEAD_DIM, 1)
    tables_b, shift_b = _rope_tables(s, DIFF_HEAD_DIM, 2)
    assert DEPTH == 2 * N_A_LAYERS and N_A_LAYERS == 2
    side_casts = {layer: [(mlp_w_up, 2 * layer, 2, 1.0), (mlp_w_down, 2 * layer, 2, 1.0)]
                  for layer in range(N_A_LAYERS)}
    side_casts[0] += [(nsa_w_out, 0, N_A_LAYERS, 1.0), (diff_w_out, 0, DEPTH - N_A_LAYERS, 1.0),
                      (diff_w_q, 0, DEPTH - N_A_LAYERS, DIFF_HEAD_DIM ** -0.5 * LOG2E)]
    nsa_proj = _nsa_weights(nsa_w_in)
    outs = []
    for bi in range(b):
        xs = x[bi]
        k_sh = v_sh = None
        mixer_w = {}
        for layer in range(DEPTH):
            if layer < N_A_LAYERS:
                o, cast = _nsa_mixer(xs, attn_norm_g[layer], nsa_proj, nsa_cmp_pos[layer],
                                     nsa_cmp_w1[layer], nsa_cmp_w2[layer], side_casts[layer],
                                     layer, tables_a, shift_a)
                mixer_w[layer] = cast[:2]
                if layer == 0:
                    nsa_out, diff_out, diff_q = cast[2:]
                xs = _matmul_res(o, nsa_out, xs, widx=layer)
            else:
                j = layer - N_A_LAYERS
                if j == 0:
                    v_sh, k_sh = _norm_proj_rope(xs, kv_norm_g, kv_w_shared.astype(BF16), tables_b,
                                                 shift=shift_b, want_raw=True)
                lambda_init = 0.8 - 0.6 * math.exp(-0.3 * layer)
                q = _norm_proj_rope(xs, attn_norm_g[layer], diff_q, tables_b, widx=j,
                                    shift=shift_b, want_raw=False)
                o = _diff_attention(q, k_sh, v_sh, diff_lambda[j], diff_subln_g[j], lambda_init)
                xs = _matmul_res(o, diff_out, xs, widx=j)
            w_up, w_down = mixer_w[layer // 2]
            xs = _mlp(xs, mlp_norm_g[layer], w_up, w_down, layer % 2, final_norm_g,
                      final_norm=(layer == DEPTH - 1))
        outs.append(xs[None])
    return outs[0] if b == 1 else jnp.concatenate(outs, axis=0)
```

```python
import functools
import math

import jax
import jax.numpy as jnp
from jax import lax
from jax.experimental import pallas as pl
from jax.experimental.pallas import tpu as pltpu

F32 = jnp.float32
BF16 = jnp.bfloat16

D_MODEL = 2048
DEPTH = 4
N_A_LAYERS = DEPTH // 2

NSA_HEADS = 16
NSA_KV_HEADS = 4
NSA_GROUP = NSA_HEADS // NSA_KV_HEADS
NSA_HEAD_DIM = D_MODEL // NSA_HEADS
CMP_BLOCK = 32
CMP_STRIDE = 16
CMP_HIDDEN = 4 * NSA_HEAD_DIM
SEL_BLOCK = 64
SEL_SHIFT = SEL_BLOCK.bit_length() - 1
SEL_TOPK = 16
WINDOW = 512
FORCE_SCORE = 1.0e4

DIFF_HEADS = 16
DIFF_KV_HEADS = 4
DIFF_GROUP = DIFF_HEADS // DIFF_KV_HEADS
DIFF_HEAD_DIM = D_MODEL // (2 * DIFF_HEADS)

D_FF = 4 * D_MODEL
ROPE_THETA = 500000.0
ROPE_FRACTION = 4
NORM_EPS = 1e-6

LANES = 128
VMEM_LIMIT = 48 * 1024 * 1024
VMEM_BUDGET = 40 * 1024 * 1024
NEG_BIG = -1.0e30
SCORE_FLOOR = -1.0e28
LOG2E = 1.4426950408889634

NSA_TQ = 256
NSA_SEL_CHUNK = 512
NSA_CHAIN_LANES = 256
DIFF_TQ = 256
ROW_TILE = 512
COL_TILE = 512
FF_TILE = 1024


def _params(*sem):
    return pltpu.CompilerParams(dimension_semantics=sem, vmem_limit_bytes=VMEM_LIMIT)


def _dot(a, b):
    return jnp.dot(a, b, preferred_element_type=F32)


def _rms(x, g):
    ms = jnp.mean(x * x, axis=-1, keepdims=True)
    return x * lax.rsqrt(ms + NORM_EPS) * g


def _proj_tiles(s, k, n, row_bytes):
    weight = 2 * k * n * 2
    for tm in (ROW_TILE, ROW_TILE // 2):
        if weight + tm * row_bytes <= VMEM_BUDGET:
            return min(tm, s), n
    return min(ROW_TILE, s), min(COL_TILE, n)


def _weight_spec(w, widx, k, tn):
    if w.ndim == 2:
        return pl.BlockSpec((k, tn), lambda i, j: (0, j))
    return pl.BlockSpec((None, k, tn), lambda i, j: (widx, 0, j))


def _normed(x_ref, g_ref, h_scr, col_tiles):
    if col_tiles == 1:
        return _rms(x_ref[...], g_ref[...]).astype(BF16)

    @pl.when(pl.program_id(1) == 0)
    def _():
        h_scr[...] = _rms(x_ref[...], g_ref[...]).astype(BF16)

    return h_scr[...]


def _store_col_blocks(o_ref, acc):
    for blk in range(acc.shape[1] // LANES):
        o_ref[blk] = acc[:, blk * LANES:(blk + 1) * LANES].astype(o_ref.dtype)


def _blocked_out(s, n, tm, tn, dtype):
    spec = pl.BlockSpec((tn // LANES, tm, LANES), lambda i, j: (j, i, 0))
    return spec, jax.ShapeDtypeStruct((n // LANES, s, LANES), dtype)


def _norm_proj_kernel(x_ref, g_ref, w_ref, o_ref, gate_ref):
    acc = _dot(_rms(x_ref[...], g_ref[...]).astype(BF16), w_ref[...])
    n_main = o_ref.shape[0] * LANES
    _store_col_blocks(o_ref, acc[:, :n_main])
    gate_ref[...] = jax.nn.sigmoid(acc[:, n_main:])


def _norm_proj_gated(x, g, w, *, widx=None, gate_cols=LANES):
    s, d = x.shape
    n_all = w.shape[-1]
    n = n_all - gate_cols
    tm, tn = _proj_tiles(s, d, n_all, 2 * d * 4 + d * 2 + n_all * 4 + 2 * n * 2 + 2 * gate_cols * 4)
    assert tn == n_all, "the gate split needs the whole weight in one column tile"
    out_spec, out_shape = _blocked_out(s, n, tm, n, BF16)
    return pl.pallas_call(
        _norm_proj_kernel,
        name="norm_proj",
        grid=(s // tm, 1),
        in_specs=[pl.BlockSpec((tm, d), lambda i, j: (i, 0)),
                  pl.BlockSpec((1, d), lambda i, j: (0, 0)),
                  _weight_spec(w, widx, d, tn)],
        out_specs=[out_spec, pl.BlockSpec((tm, gate_cols), lambda i, j: (i, 0))],
        out_shape=[out_shape, jax.ShapeDtypeStruct((s, gate_cols), F32)],
        compiler_params=_params("parallel", "arbitrary"),
    )(x, g.reshape(1, d), w)


def _norm_proj_rope_kernel(x_ref, g_ref, w_ref, c_ref, sa_ref, sb_ref, *rest, shift, want_raw,
                           col_tiles):
    if want_raw:
        raw_ref, rot_ref, h_scr = rest
    else:
        rot_ref, h_scr = rest

    acc = _dot(_normed(x_ref, g_ref, h_scr, col_tiles), w_ref[...])
    if want_raw:
        _store_col_blocks(raw_ref, acc)
    c, sa, sb = c_ref[...], sa_ref[...], sb_ref[...]
    for blk in range(acc.shape[1] // LANES):
        a = acc[:, blk * LANES:(blk + 1) * LANES]
        r = a * c + pltpu.roll(a, LANES - shift, 1) * sa + pltpu.roll(a, shift, 1) * sb
        rot_ref[blk] = r.astype(BF16)


def _norm_proj_rope(x, g, w, tables, *, shift, want_raw, widx=None):
    s, d = x.shape
    n = w.shape[-1]
    n_out = 2 if want_raw else 1
    tm, tn = _proj_tiles(s, d, n, 2 * d * 4 + d * 2 + n * 4 + n_out * 2 * n * 2 + 3 * 2 * LANES * 4)
    tab_spec = pl.BlockSpec((tm, LANES), lambda i, j: (i, 0))
    out_spec, out_sd = _blocked_out(s, n, tm, tn, BF16)
    return pl.pallas_call(
        functools.partial(_norm_proj_rope_kernel, shift=shift, want_raw=want_raw,
                          col_tiles=n // tn),
        name="norm_proj_rope",
        grid=(s // tm, n // tn),
        in_specs=[pl.BlockSpec((tm, d), lambda i, j: (i, 0)),
                  pl.BlockSpec((1, d), lambda i, j: (0, 0)),
                  _weight_spec(w, widx, d, tn),
                  tab_spec, tab_spec, tab_spec],
        out_specs=[out_spec, out_spec] if want_raw else out_spec,
        out_shape=[out_sd, out_sd] if want_raw else out_sd,
        scratch_shapes=[pltpu.VMEM((tm, d), BF16)],
        compiler_params=_params("parallel", "arbitrary"),
    )(x, g.reshape(1, d), w, *tables)


def _matmul_res_kernel(a_ref, w_ref, r_ref, o_ref):
    o_ref[...] = r_ref[...] + _dot(a_ref[...], w_ref[...])


def _matmul_res(a, w, res, *, widx=None):
    s, k = a.shape
    n = w.shape[-1]
    tm, tn = _proj_tiles(s, k, n, 2 * k * 2 + 2 * 2 * n * 4 + n * 4)
    return pl.pallas_call(
        _matmul_res_kernel,
        name="matmul_res",
        grid=(s // tm, n // tn),
        in_specs=[pl.BlockSpec((tm, k), lambda i, j: (i, 0)),
                  _weight_spec(w, widx, k, tn),
                  pl.BlockSpec((tm, tn), lambda i, j: (i, j))],
        out_specs=pl.BlockSpec((tm, tn), lambda i, j: (i, j)),
        out_shape=jax.ShapeDtypeStruct((s, n), F32),
        compiler_params=_params("parallel", "arbitrary"),
    )(a, w, res)


def _mlp_kernel(x_ref, g_ref, wu_ref, wd_ref, fg_ref, o_ref, h_scr, *, final_norm):
    j = pl.program_id(1)

    @pl.when(j == 0)
    def _():
        x = x_ref[...]
        h_scr[...] = _rms(x, g_ref[...]).astype(BF16)
        o_ref[...] = x

    u = _dot(h_scr[...], wu_ref[...])
    u = jnp.square(jnp.maximum(u, 0.0)).astype(BF16)
    o_ref[...] += _dot(u, wd_ref[...])

    if final_norm:
        @pl.when(j == pl.num_programs(1) - 1)
        def _():
            o_ref[...] = _rms(o_ref[...], fg_ref[...])


def _mlp(x, g, w_up, w_down, layer, final_g, *, final_norm):
    s, d = x.shape
    ff = w_up.shape[-1]
    tm, tf = min(ROW_TILE, s), min(FF_TILE, ff)
    return pl.pallas_call(
        functools.partial(_mlp_kernel, final_norm=final_norm),
        name="sq_relu_mlp",
        grid=(s // tm, ff // tf),
        in_specs=[pl.BlockSpec((tm, d), lambda i, j: (i, 0)),
                  pl.BlockSpec((1, d), lambda i, j: (0, 0)),
                  pl.BlockSpec((None, d, tf), lambda i, j: (layer, 0, j)),
                  pl.BlockSpec((None, tf, d), lambda i, j: (layer, j, 0)),
                  pl.BlockSpec((1, d), lambda i, j: (0, 0))],
        out_specs=pl.BlockSpec((tm, d), lambda i, j: (i, 0)),
        out_shape=jax.ShapeDtypeStruct((s, d), F32),
        scratch_shapes=[pltpu.VMEM((tm, d), BF16)],
        compiler_params=_params("parallel", "arbitrary"),
    )(x, g.reshape(1, d), w_up, w_down, final_g.reshape(1, d))


def _compress_kernel(t_ref, pos_ref, w1_ref, w2_ref, o_ref):
    t = t_ref[0, 0].astype(F32)
    pos = pos_ref[0]
    half = t.shape[1]
    a = _dot((t + pos[0:1]).astype(BF16), w1_ref[0, :half, :])
    b = _dot((t + pos[1:2]).astype(BF16), w1_ref[0, half:, :])
    nc = t.shape[0]
    hid = jax.nn.gelu(a + pltpu.roll(b, nc - 1, 0))
    o_ref[0, 0] = _dot(hid.astype(BF16), w2_ref[0]).astype(BF16)


def _compress(t2, pos2, w1, w2):
    _, hk, nc, width = t2.shape
    dk = w2.shape[-1]
    return pl.pallas_call(
        _compress_kernel,
        name="nsa_compress",
        grid=(2, hk),
        in_specs=[pl.BlockSpec((1, 1, nc, width), lambda j, h: (j, h, 0, 0)),
                  pl.BlockSpec((1, 2, width), lambda j, h: (j, 0, 0)),
                  pl.BlockSpec((1, 2 * width, CMP_HIDDEN), lambda j, h: (j, 0, 0)),
                  pl.BlockSpec((1, CMP_HIDDEN, dk), lambda j, h: (j, 0, 0))],
        out_specs=pl.BlockSpec((1, 1, nc, dk), lambda j, h: (j, h, 0, 0)),
        out_shape=jax.ShapeDtypeStruct((2, hk, nc, dk), BF16),
        compiler_params=_params("parallel", "arbitrary"),
    )(t2, pos2, w1, w2)


def _nsa_attn_kernel(qr_ref, qo_ref, kc_ref, vc_ref, ks_ref, oh_ref, vs_ref, kw_ref, vw_ref, gt_ref,
                     *refs, seq, cast_scales):
    n_cast = len(cast_scales)
    cast_src, o_ref, cast_dst = refs[:n_cast], refs[n_cast], refs[n_cast + 1:2 * n_cast + 1]
    vst_scr, vwt_scr, rhs_scr, s_scr, m_scr, l_scr, acc_scr = refs[2 * n_cast + 1:]
    g_, tq, dk = NSA_GROUP, NSA_TQ, NSA_HEAD_DIM
    cw = NSA_CHAIN_LANES
    n_chain = g_ * tq // cw
    nc = seq // CMP_STRIDE
    nsel = seq // SEL_BLOCK
    ch = min(NSA_SEL_CHUNK, seq)
    wlen = WINDOW + tq
    qb = pl.program_id(1)
    start = qb * tq
    t_row = start + lax.broadcasted_iota(jnp.int32, (1, tq), 1)

    for src, dst, scale in zip(cast_src, cast_dst, cast_scales):
        w = src[...]
        dst[...] = (w if scale == 1.0 else w * scale).astype(BF16)

    @pl.when(qb == 0)
    def _():
        for r in range(0, seq, ch):
            vst_scr[:, r:r + ch] = vs_ref[r:r + ch, :].T
            vwt_scr[:, r:r + ch] = vw_ref[r:r + ch, :].T

    def heads_on_lanes(ref):
        return jnp.concatenate([ref[g].T for g in range(g_)], axis=1)

    def tile_heads(x, n):
        return jnp.concatenate([x] * n, axis=1)

    def softmax_keys(s2):
        m = jnp.max(s2, axis=0, keepdims=True)
        m = jnp.where(m == -jnp.inf, 0.0, m)
        e = jnp.exp2(s2 - m)
        return e, 1.0 / jnp.maximum(jnp.sum(e, axis=0, keepdims=True), 1e-30)

    qr_t = heads_on_lanes(qr_ref)
    qo_t = heads_on_lanes(qo_ref)

    n_end = lax.broadcasted_iota(jnp.int32, (nc, tq), 0) * CMP_STRIDE + (CMP_BLOCK - 1)
    bias_c = jnp.where(n_end <= t_row, 0.0, -jnp.inf)
    e_c, r_c = softmax_keys(_dot(kc_ref[0, 0], qr_t) + tile_heads(bias_c, g_))
    p_c = e_c * r_c
    o_c = _dot(vc_ref[0, 0].T, p_c.astype(BF16))

    w0 = pl.multiple_of(jnp.maximum(start - WINDOW, 0), tq)
    kpos = w0 + lax.broadcasted_iota(jnp.int32, (wlen, tq), 0)
    bias_w = jnp.where((kpos <= t_row) & (kpos > t_row - WINDOW), 0.0, -jnp.inf)
    e_w, r_w = softmax_keys(_dot(kw_ref[pl.ds(w0, wlen), :], qo_t) + tile_heads(bias_w, g_))
    o_w = _dot(vwt_scr[:, pl.ds(w0, wlen)], e_w.astype(BF16)) * r_w

    p_sum = p_c[:, :tq]
    for g in range(1, g_):
        p_sum = p_sum + p_c[:, g * tq:(g + 1) * tq]
    n_i = lax.broadcasted_iota(jnp.int32, (nsel, nc), 1) * CMP_STRIDE
    s_i = lax.broadcasted_iota(jnp.int32, (nsel, nc), 0) * SEL_BLOCK
    overlap = jnp.where((n_i < s_i + SEL_BLOCK) & (n_i + CMP_BLOCK > s_i), 1.0, 0.0).astype(BF16)
    p_hi = p_sum.astype(BF16)
    p_r1 = p_sum - p_hi.astype(F32)
    p_mid = p_r1.astype(BF16)
    p_lo = (p_r1 - p_mid.astype(F32)).astype(BF16)
    imp = _dot(overlap, p_hi) + _dot(overlap, p_mid) + _dot(overlap, p_lo)

    ids = lax.broadcasted_iota(jnp.int32, (nsel, tq), 0)
    cur = t_row >> SEL_SHIFT
    forced = (ids == 0) | (ids == cur) | (ids == cur - 1)
    imp = jnp.where(forced, FORCE_SCORE, imp)
    work = jnp.where(ids * SEL_BLOCK <= t_row, imp, -1.0)

    sel = jnp.zeros((nsel, tq), F32)
    for _ in range(min(SEL_TOPK, nsel)):
        mx = jnp.max(work, axis=0, keepdims=True)
        first = jnp.min(jnp.where(work == mx, ids, nsel), axis=0, keepdims=True)
        pick = ids == first
        sel = jnp.where(pick, 1.0, sel)
        work = jnp.where(pick, -3.0e38, work)

    neg_sel = tile_heads(jnp.where(sel > 0.5, 0.0, NEG_BIG).astype(BF16), cw // tq)
    for j in range(n_chain):
        rhs_scr[j] = jnp.concatenate([qo_t[:, j * cw:(j + 1) * cw], neg_sel], axis=0)
    m_scr[...] = jnp.full(m_scr.shape, NEG_BIG, F32)
    l_scr[...] = jnp.zeros(l_scr.shape, F32)
    acc_scr[...] = jnp.zeros(acc_scr.shape, F32)

    def keys_with_block(off):
        return jnp.concatenate([ks_ref[pl.ds(off, ch), :], oh_ref[pl.ds(off, ch), :]], axis=1)

    def fold(off, buf, diag_width=None):
        diagonal = diag_width is not None
        width = diag_width if diagonal else ch
        off = pl.multiple_of(off, ch)
        v_t = vst_scr[:, pl.ds(off, width)]
        state = [(m_scr[j], l_scr[j], acc_scr[j]) for j in range(n_chain)]
        if diagonal:
            key = off + lax.broadcasted_iota(jnp.int32, (width, tq), 0)
            causal = tile_heads(jnp.where(key <= t_row, 0.0, NEG_BIG), cw // tq)
            todo = []
        else:
            k_next = keys_with_block(pl.multiple_of(off + ch, ch))
            todo = list(range(n_chain))
        ahead = 3
        for j in todo[:ahead]:
            s_scr[1 - buf, j] = _dot(k_next, rhs_scr[j])
        new_state = []
        for j in range(n_chain):
            s = s_scr[buf, j, :width]
            if diagonal:
                s = s + causal
            m_old, l_old, acc_old = state[j]
            m_new = jnp.maximum(m_old, jnp.max(s, axis=0, keepdims=True))
            alpha = jnp.exp2(m_old - m_new)
            e = jnp.exp2(s - m_new)
            l_new = alpha * l_old + jnp.sum(e, axis=0, keepdims=True)
            acc_new = alpha * acc_old + _dot(v_t, e.astype(BF16))
            new_state.append((m_new, l_new, acc_new))
            for jn in todo[j + ahead:j + ahead + 1]:
                s_scr[1 - buf, jn] = _dot(k_next, rhs_scr[jn])
        for j in range(n_chain):
            m_scr[j], l_scr[j], acc_scr[j] = new_state[j]

    k_first = keys_with_block(0)
    for j in range(n_chain):
        s_scr[0, j] = _dot(k_first, rhs_scr[j])

    def pair_step(c, carry):
        off = pl.multiple_of(c * 2 * ch, 2 * ch)
        fold(off, 0)
        fold(off + ch, 1)
        return carry

    n_full = (start + tq - 1) // ch
    lax.fori_loop(0, n_full // 2, pair_step, 0)
    rest = pl.multiple_of((n_full // 2) * 2 * ch, 2 * ch)

    tiles_per_chunk = ch // tq
    for rem in range(2 * tiles_per_chunk):
        @pl.when(qb % (2 * tiles_per_chunk) == rem)
        def _(rem=rem):
            diag_width = (rem % tiles_per_chunk + 1) * tq
            if rem // tiles_per_chunk == 1:
                fold(rest, 0)
                fold(rest + ch, 1, diag_width)
            else:
                fold(rest, 0, diag_width)

    gt = gt_ref[0]
    for g in range(g_):
        j = g * tq // cw
        lanes = slice(g * tq - j * cw, g * tq - j * cw + tq)
        o_s = jnp.where(m_scr[j][:, lanes] > SCORE_FLOOR,
                        acc_scr[j][:, lanes] * (1.0 / jnp.maximum(l_scr[j][:, lanes], 1e-30)), 0.0)
        cols = slice(g * tq, (g + 1) * tq)
        o_t = (gt[3 * g:3 * g + 1] * o_c[:, cols] + gt[3 * g + 1:3 * g + 2] * o_s
               + gt[3 * g + 2:3 * g + 3] * o_w[:, cols])
        o_ref[:, g * dk:(g + 1) * dk] = o_t.T.astype(BF16)


def _nsa_attention(raw, rot, plain, cmp_kv, gates, casts):
    s = raw.shape[1]
    hk, g_, dk, tq = NSA_KV_HEADS, NSA_GROUP, NSA_HEAD_DIM, NSA_TQ
    nc = s // CMP_STRIDE
    nsel = s // SEL_BLOCK
    one_hot = (jnp.arange(s)[:, None] >> SEL_SHIFT == jnp.arange(nsel)[None, :]).astype(BF16)
    gates_t = gates[:, :3 * NSA_HEADS].T.reshape(hk, 3 * g_, s)
    qspec = pl.BlockSpec((g_, tq, dk), lambda h, i: (h, i, 0))
    col = lambda base: pl.BlockSpec((None, s, dk), lambda h, i: (base + h, 0, 0))
    cmp_spec = lambda j: pl.BlockSpec((1, 1, nc, dk), lambda h, i: (j, h, 0, 0))
    cw = NSA_CHAIN_LANES
    assert cw % tq == 0 and (g_ * tq) % cw == 0 and NSA_SEL_CHUNK % tq == 0
    n_chain = g_ * tq // cw
    n_q = s // tq
    n_prog = hk * n_q
    cast_args, cast_in, cast_out, cast_shapes = [], [], [], []
    for w, first, count, _ in casts:
        rows, cols = count * w.shape[1] // n_prog, w.shape[-1]
        base = first * w.shape[1] // rows
        cast_args.append(w.reshape(-1, cols))
        cast_in.append(pl.BlockSpec((rows, cols), lambda h, i, base=base: (base + h * n_q + i, 0)))
        cast_out.append(pl.BlockSpec((rows, cols), lambda h, i: (h * n_q + i, 0)))
        cast_shapes.append(jax.ShapeDtypeStruct((rows * n_prog, cols), BF16))
    outs = pl.pallas_call(
        functools.partial(_nsa_attn_kernel, seq=s, cast_scales=tuple(c[3] for c in casts)),
        name="nsa_attention",
        grid=(hk, n_q),
        in_specs=[qspec, qspec, cmp_spec(0), cmp_spec(1),
                  col(NSA_HEADS),
                  pl.BlockSpec((s, nsel), lambda h, i: (0, 0)),
                  col(2 * hk),
                  col(NSA_HEADS + hk), col(3 * hk),
                  pl.BlockSpec((1, 3 * g_, tq), lambda h, i: (h, 0, i))] + cast_in,
        out_specs=[pl.BlockSpec((tq, g_ * dk), lambda h, i: (i, h))] + cast_out,
        out_shape=[jax.ShapeDtypeStruct((s, NSA_HEADS * dk), BF16)] + cast_shapes,
        scratch_shapes=[pltpu.VMEM((dk, s), BF16), pltpu.VMEM((dk, s), BF16),
                        pltpu.VMEM((n_chain, dk + nsel, cw), BF16),
                        pltpu.VMEM((2, n_chain, min(NSA_SEL_CHUNK, s), cw), F32),
                        pltpu.VMEM((n_chain, 1, cw), F32),
                        pltpu.VMEM((n_chain, 1, cw), F32),
                        pltpu.VMEM((n_chain, dk, cw), F32)],
        compiler_params=_params("parallel", "arbitrary"),
    )(raw, rot, cmp_kv, cmp_kv, rot, one_hot, plain, rot, plain, gates_t, *cast_args)
    return outs[0], [c.reshape(count, *w.shape[1:]) for c, (w, _, count, _) in zip(outs[1:], casts)]


def _diff_attn_kernel(q_ref, k_ref, v_ref, lam_ref, sg_ref, o_ref, qq_scr, vt_scr, s_scr, m_scr,
                      l_scr, acc_scr, *, lambda_init):
    g_, tq, d = DIFF_GROUP, DIFF_TQ, DIFF_HEAD_DIM
    n_chain = 2 * g_
    qb = pl.program_id(1)

    @pl.when(qb == 0)
    def _():
        rows = 2 * tq
        for r in range(0, v_ref.shape[0], rows):
            vt_scr[:, r:r + rows] = v_ref[r:r + rows, :].T

    first_half = lax.broadcasted_iota(jnp.int32, (2 * d, 1), 0) < d
    for g in range(g_):
        q_t = q_ref[g].T
        zero = jnp.zeros_like(q_t)
        qq_scr[2 * g] = jnp.where(first_half, q_t, zero)
        qq_scr[2 * g + 1] = jnp.where(first_half, zero, q_t)
    m_scr[...] = jnp.full(m_scr.shape, NEG_BIG, F32)
    l_scr[...] = jnp.zeros(l_scr.shape, F32)
    acc_scr[...] = jnp.zeros(acc_scr.shape, F32)

    ch = 2 * tq
    n_full = qb // 2

    def fold(off, buf, diag_width=None):
        diagonal = diag_width is not None
        width = diag_width if diagonal else ch
        off = pl.multiple_of(off, ch)
        v_t = vt_scr[:, pl.ds(off, width)]
        state = [(m_scr[j], l_scr[j], acc_scr[j]) for j in range(n_chain)]
        if diagonal:
            key = lax.broadcasted_iota(jnp.int32, (width, tq), 0) + (off - qb * tq)
            causal = key <= lax.broadcasted_iota(jnp.int32, (width, tq), 1)
            todo = []
        else:
            k_next = k_ref[pl.ds(pl.multiple_of(off + ch, ch), ch), :]
            todo = list(range(n_chain))
        ahead = 3
        for j in todo[:ahead]:
            s_scr[1 - buf, j] = _dot(k_next, qq_scr[j])
        new_state = []
        for j in range(n_chain):
            s = s_scr[buf, j, :width]
            if diagonal:
                s = jnp.where(causal, s, NEG_BIG)
            m_old, l_old, acc_old = state[j]
            m_new = jnp.maximum(m_old, jnp.max(s, axis=0, keepdims=True))
            alpha = jnp.exp2(m_old - m_new)
            p = jnp.exp2(s - m_new)
            l_new = alpha * l_old + jnp.sum(p, axis=0, keepdims=True)
            acc_new = alpha * acc_old + _dot(v_t, p.astype(BF16))
            new_state.append((m_new, l_new, acc_new))
            for jn in todo[j + ahead:j + ahead + 1]:
                s_scr[1 - buf, jn] = _dot(k_next, qq_scr[jn])
        for j in range(n_chain):
            m_scr[j], l_scr[j], acc_scr[j] = new_state[j]

    k_first = k_ref[:ch, :]
    for j in range(n_chain):
        s_scr[0, j] = _dot(k_first, qq_scr[j])

    def pair_step(c, carry):
        off = pl.multiple_of(c * 2 * ch, 2 * ch)
        fold(off, 0)
        fold(off + ch, 1)
        return carry

    lax.fori_loop(0, n_full // 2, pair_step, 0)
    rest = pl.multiple_of((n_full // 2) * 2 * ch, 2 * ch)

    for rem in range(4):
        @pl.when(qb % 4 == rem)
        def _(rem=rem):
            diag_width = ch if rem % 2 == 1 else tq
            if rem // 2 == 1:
                fold(rest, 0)
                fold(rest + ch, 1, diag_width)
            else:
                fold(rest, 0, diag_width)

    lv = lam_ref[...]
    lam = (jnp.exp(jnp.sum(lv[0:1] * lv[1:2], axis=-1, keepdims=True))
           - jnp.exp(jnp.sum(lv[2:3] * lv[3:4], axis=-1, keepdims=True)) + lambda_init)
    sub_g = sg_ref[...]
    for g in range(g_):
        o1 = acc_scr[2 * g] / jnp.maximum(l_scr[2 * g], 1e-30)
        o2 = acc_scr[2 * g + 1] / jnp.maximum(l_scr[2 * g + 1], 1e-30)
        a = o1 - lam * o2
        ms = jnp.mean(a * a, axis=0, keepdims=True)
        a = a * lax.rsqrt(ms + NORM_EPS) * sub_g * (1.0 - lambda_init)
        o_ref[:, g * 2 * d:(g + 1) * 2 * d] = a.T.astype(BF16)


def _diff_attention(q_rot, k_rot, v, lam_vecs, subln_g, lambda_init):
    s = q_rot.shape[1]
    hk, g_, d, tq = DIFF_KV_HEADS, DIFF_GROUP, DIFF_HEAD_DIM, DIFF_TQ
    return pl.pallas_call(
        functools.partial(_diff_attn_kernel, lambda_init=lambda_init),
        name="diff_attention",
        grid=(hk, s // tq),
        in_specs=[pl.BlockSpec((g_, tq, 2 * d), lambda h, i: (h, i, 0)),
                  pl.BlockSpec((None, s, 2 * d), lambda h, i: (h, 0, 0)),
                  pl.BlockSpec((None, s, 2 * d), lambda h, i: (hk + h, 0, 0)),
                  pl.BlockSpec((4, d), lambda h, i: (0, 0)),
                  pl.BlockSpec((2 * d, 1), lambda h, i: (0, 0))],
        out_specs=pl.BlockSpec((tq, g_ * 2 * d), lambda h, i: (i, h)),
        out_shape=jax.ShapeDtypeStruct((s, DIFF_HEADS * 2 * d), BF16),
        scratch_shapes=[pltpu.VMEM((2 * g_, 2 * d, tq), BF16),
                        pltpu.VMEM((2 * d, s), BF16),
                        pltpu.VMEM((2, 2 * g_, 2 * tq, tq), F32),
                        pltpu.VMEM((2 * g_, 1, tq), F32),
                        pltpu.VMEM((2 * g_, 1, tq), F32),
                        pltpu.VMEM((2 * g_, 2 * d, tq), F32)],
        compiler_params=_params("parallel", "arbitrary"),
    )(q_rot, k_rot, v, lam_vecs, subln_g.reshape(2 * d, 1))


def _rope_tables(seq, head_dim, reps):
    rot = head_dim // ROPE_FRACTION
    half = rot // 2
    inv = 1.0 / (ROPE_THETA ** (jnp.arange(0, rot, 2, dtype=F32) / rot))
    ang = jnp.arange(seq, dtype=F32)[:, None] * inv[None, :]
    cos, sin = jnp.cos(ang), jnp.sin(ang)
    rest = head_dim - rot
    zeros_h = jnp.zeros((seq, half), F32)
    c = jnp.concatenate([cos, cos, jnp.ones((seq, rest), F32)], axis=-1)
    sa = jnp.concatenate([-sin, zeros_h, jnp.zeros((seq, rest), F32)], axis=-1)
    sb = jnp.concatenate([zeros_h, sin, jnp.zeros((seq, rest), F32)], axis=-1)
    return tuple(jnp.tile(t, (1, reps)) for t in (c, sa, sb)), half


def _nsa_weights(w_in):
    hk, dk = NSA_KV_HEADS, NSA_HEAD_DIM
    qw, kvw = NSA_HEADS * dk, hk * dk
    part = lambda i: w_in[:, :, qw + i * kvw:qw + (i + 1) * kvw]
    w_q = w_in[:, :, :qw] * (dk ** -0.5 * LOG2E)
    w_rope = jnp.concatenate([w_q, part(2), part(4)], axis=2).astype(BF16)
    n_gate = 3 * NSA_HEADS
    w_gate = jnp.pad(w_in[:, :, qw + 6 * kvw:], ((0, 0), (0, 0), (0, LANES - n_gate)))
    w_plain = jnp.concatenate([part(0), part(1), part(3), part(5), w_gate], axis=2).astype(BF16)
    return w_rope, w_plain


def _nsa_mixer(x, norm_g, proj_w, cmp_pos, cmp_w1, cmp_w2, casts, layer, tables, shift):
    s = x.shape[0]
    hk, dk = NSA_KV_HEADS, NSA_HEAD_DIM
    w_rope, w_plain = proj_w
    raw, rot = _norm_proj_rope(x, norm_g, w_rope, tables, widx=layer, shift=shift, want_raw=True)
    plain, gates = _norm_proj_gated(x, norm_g, w_plain, widx=layer)

    nc = s // CMP_STRIDE
    t2 = plain[:2 * hk].reshape(2, hk, nc, CMP_STRIDE * dk)
    cmp_kv = _compress(t2, cmp_pos.reshape(2, 2, CMP_STRIDE * dk), cmp_w1.astype(BF16),
                       cmp_w2.astype(BF16))
    return _nsa_attention(raw, rot, plain, cmp_kv, gates, casts)


def kernel(x, attn_norm_g, mlp_norm_g, final_norm_g, nsa_w_in, nsa_cmp_pos, nsa_cmp_w1, nsa_cmp_w2, nsa_w_out, kv_norm_g, kv_w_shared, diff_w_q, diff_lambda, diff_subln_g, diff_w_out, mlp_w_up, mlp_w_down):
    b, s, d = x.shape
    tables_a, shift_a = _rope_tables(s, NSA_HEAD_DIM, 1)
    tables_b, shift_b = _rope_tables(s, DIFF_HEAD_DIM, 2)
    assert DEPTH == 2 * N_A_LAYERS and N_A_LAYERS == 2
    side_casts = {layer: [(mlp_w_up, 2 * layer, 2, 1.0), (mlp_w_down, 2 * layer, 2, 1.0)]
                  for layer in range(N_A_LAYERS)}
    side_casts[0] += [(nsa_w_out, 0, N_A_LAYERS, 1.0), (diff_w_out, 0, DEPTH - N_A_LAYERS, 1.0),
                      (diff_w_q, 0, DEPTH - N_A_LAYERS, DIFF_HEAD_DIM ** -0.5 * LOG2E)]
    nsa_proj = _nsa_weights(nsa_w_in)
    outs = []
    for bi in range(b):
        xs = x[bi]
        k_sh = v_sh = None
        mixer_w = {}
        for layer in range(DEPTH):
            if layer < N_A_LAYERS:
                o, cast = _nsa_mixer(xs, attn_norm_g[layer], nsa_proj, nsa_cmp_pos[layer],
                                     nsa_cmp_w1[layer], nsa_cmp_w2[layer], side_casts[layer],
                                     layer, tables_a, shift_a)
                mixer_w[layer] = cast[:2]
                if layer == 0:
                    nsa_out, diff_out, diff_q = cast[2:]
                xs = _matmul_res(o, nsa_out, xs, widx=layer)
            else:
                j = layer - N_A_LAYERS
                if j == 0:
                    v_sh, k_sh = _norm_proj_rope(xs, kv_norm_g, kv_w_shared.astype(BF16), tables_b,
                                                 shift=shift_b, want_raw=True)
                lambda_init = 0.8 - 0.6 * math.exp(-0.3 * layer)
                q = _norm_proj_rope(xs, attn_norm_g[layer], diff_q, tables_b, widx=j,
                                    shift=shift_b, want_raw=False)
                o = _diff_attention(q, k_sh, v_sh, diff_lambda[j], diff_subln_g[j], lambda_init)
                xs = _matmul_res(o, diff_out, xs, widx=j)
            w_up, w_down = mixer_w[layer // 2]
            xs = _mlp(xs, mlp_norm_g[layer], w_up, w_down, layer % 2, final_norm_g,
                      final_norm=(layer == DEPTH - 1))
        outs.append(xs[None])
    return outs[0] if b == 1 else jnp.concatenate(outs, axis=0)
```
